```python
import math
import jax
import jax.numpy as jnp
from jax import lax
import numpy as np

D_MODEL = 1024
BATCH = 16
SEQ = 2048
DEPTH = 4
DEC_BATCH = 8
DEC_SEQ = 32
PAST_LEN = 2048

CHUNK = 64
Q_BLOCK = 128
DK_A = 128
DV_A = 128
H_A = D_MODEL // DV_A
CONV_W = 4
CONV_CH = 2 * H_A * DK_A + H_A * DV_A
DH_B = 64
DHV_B = 2 * DH_B
H_B = D_MODEL // DHV_B
N_BUCKETS = 32
MAX_DIST = 128
D_FF = -(-8 * D_MODEL // (3 * 256)) * 256
ALPHA = (2 * DEPTH) ** 0.25
BETA_INIT = (8 * DEPTH) ** -0.25
LN_EPS = 1e-5
IN_SIZES = (CONV_CH, H_A * DV_A, H_A, H_A, H_B * 2 * DH_B, H_B * 2 * DH_B, H_B * DHV_B, 2 * D_MODEL)
D_IN = sum(IN_SIZES)
IN_SPLITS = tuple(np.cumsum(IN_SIZES)[:-1].tolist())

kernel_name = 'hybrid_deltanet_diffattn_stream_step'

F32 = jnp.float32


def layer_norm(x, g, b):
    xf = x.astype(F32)
    mu = jnp.mean(xf, axis=-1, keepdims=True)
    var = jnp.mean(jnp.square(xf - mu), axis=-1, keepdims=True)
    return ((xf - mu) * lax.rsqrt(var + LN_EPS) * g.astype(F32) + b.astype(F32)).astype(x.dtype)


def rms_norm(x, g):
    xf = x.astype(F32)
    out = xf * lax.rsqrt(jnp.mean(jnp.square(xf), axis=-1, keepdims=True) + LN_EPS) * g.astype(F32)
    return out.astype(x.dtype)


def l2norm(x):
    return x * lax.rsqrt(jnp.sum(jnp.square(x), axis=-1, keepdims=True) + 1e-6)


def t5_bucket(rel):
    nb = N_BUCKETS // 2
    max_exact = nb // 2
    ret = jnp.where(rel > 0, nb, 0)
    n = jnp.abs(rel)
    nf = jnp.maximum(n, 1).astype(F32)
    large = max_exact + (jnp.log(nf / max_exact) / math.log(MAX_DIST / max_exact) * (nb - max_exact)).astype(jnp.int32)
    large = jnp.minimum(large, nb - 1)
    return ret + jnp.where(n < max_exact, n, large)


def causal_conv(u, buf, w):
    T = u.shape[1]
    cat = jnp.concatenate([buf.astype(u.dtype), u], axis=1)
    out = cat[:, 0:T] * w[0]
    for i in range(1, CONV_W):
        out = out + cat[:, i:i + T] * w[i]
    return out, cat[:, cat.shape[1] - (CONV_W - 1):]


def gated_delta_rule(q, k, v, beta, g, s0, chunk):
    B, T, H, DK = q.shape
    DV = v.shape[-1]
    N = T // chunk

    def to_blocks(a):
        a = a.reshape((B, N, chunk) + a.shape[2:])
        return jnp.moveaxis(a, (1, 2), (0, 3))

    qc, kc, vc = to_blocks(q), to_blocks(k), to_blocks(v)
    bc, gc = to_blocks(beta), to_blocks(g)
    G = jnp.cumsum(gc, axis=-1)
    idx = jnp.arange(chunk)
    causal = idx[:, None] >= idx[None, :]
    strict = idx[:, None] > idx[None, :]
    decay = jnp.exp(jnp.where(causal, G[..., :, None] - G[..., None, :], -jnp.inf))
    kb = kc * bc[..., None]
    a_mat = jnp.where(strict, jnp.einsum('nbhid,nbhjd->nbhij', kb, kc) * decay, 0.0)
    lhs = jnp.eye(chunk, dtype=F32) + a_mat
    u_v = lax.linalg.triangular_solve(lhs, vc * bc[..., None], left_side=True, lower=True, unit_diagonal=True)
    w_k = lax.linalg.triangular_solve(lhs, kb * jnp.exp(G)[..., None], left_side=True, lower=True, unit_diagonal=True)
    qk = jnp.einsum('nbhid,nbhjd->nbhij', qc, kc) * decay
    q_g = qc * jnp.exp(G)[..., None]
    k_tail = kc * jnp.exp(G[..., -1:] - G)[..., None]
    g_last = jnp.exp(G[..., -1])

    def step(S, xs):
        u_v_n, w_n, qk_n, q_n, kt_n, gl_n = xs
        u = u_v_n - jnp.einsum('bhlk,bhkv->bhlv', w_n, S)
        o = jnp.einsum('bhlk,bhkv->bhlv', q_n, S) + jnp.einsum('bhij,bhjv->bhiv', qk_n, u)
        S = gl_n[..., None, None] * S + jnp.einsum('bhlk,bhlv->bhkv', kt_n, u)
        return S, o

    s_final, o = lax.scan(step, s0, (u_v, w_k, qk, q_g, k_tail, g_last))
    o = jnp.moveaxis(o, (0, 3), (1, 2)).reshape(B, T, H, DV)
    return o, s_final


def diff_attend(q, k, v, qpos, kpos, lam_val, rel_bias):
    bias = jnp.take(rel_bias, t5_bucket(kpos[None, :] - qpos[:, None]), axis=0)
    bias = jnp.transpose(bias, (2, 0, 1)).astype(F32)
    mask = (kpos[None, :] // CHUNK) <= (qpos[:, None] // CHUNK)
    s = jnp.einsum('bqhmd,bkhmd->bhmqk', q, k).astype(F32) * (DH_B ** -0.5) + bias[None, :, None]
    s = jnp.where(mask, s, -1e30)
    p = jax.nn.softmax(s, axis=-1)
    w = (p[:, :, 0] - lam_val * p[:, :, 1]).astype(v.dtype)
    return jnp.einsum('bhqk,bkhe->bqhe', w, v)


def prompt_diff_attention(q, k, v, lam_val, rel_bias):
    B, T = q.shape[0], q.shape[1]
    nb = T // Q_BLOCK
    qb = jnp.swapaxes(q.reshape(B, nb, Q_BLOCK, H_B, 2, DH_B), 0, 1)
    kpos = jnp.arange(T)

    def one_block(args):
        q_blk, i = args
        qpos = i * Q_BLOCK + jnp.arange(Q_BLOCK)
        return diff_attend(q_blk, k, v, qpos, kpos, lam_val, rel_bias)

    o = lax.map(one_block, (qb, jnp.arange(nb)))
    return jnp.swapaxes(o, 0, 1).reshape(B, T, H_B, DHV_B)


def trunk_layer(x, c, l, conv_buf, s0, past_k, past_v, P):
    Bn, T, _ = x.shape
    mod = jnp.einsum('bd,de->be', jax.nn.silu(c), P['w_ada'][l]) + P['b_ada'][l]
    sh1, sc1, gt1, sh2, sc2, gt2 = [m[:, None, :] for m in jnp.split(mod, 6, axis=-1)]
    h = x * (1.0 + sc1) + sh1
    proj = jnp.einsum('btd,de->bte', h, P['w_in'][l])
    qkv_a, z_a, b_a, a_a, q_b, k_b, v_b, gates = jnp.split(proj, IN_SPLITS, axis=-1)
    conv_out, new_buf = causal_conv(qkv_a, conv_buf, P['conv_w'][l])
    conv_out = jax.nn.silu(conv_out.astype(F32))
    q_a, k_a, v_a = jnp.split(conv_out, [H_A * DK_A, 2 * H_A * DK_A], axis=-1)
    q_a = l2norm(q_a.reshape(Bn, T, H_A, DK_A)) * (DK_A ** -0.5)
    k_a = l2norm(k_a.reshape(Bn, T, H_A, DK_A))
    v_a = v_a.reshape(Bn, T, H_A, DV_A)
    beta = jax.nn.sigmoid(b_a.astype(F32))
    g = -jnp.exp(P['a_log'][l].astype(F32)) * jax.nn.softplus(a_a.astype(F32) + P['dt_bias'][l].astype(F32))
    chunk = CHUNK if past_k is None else T
    o_a, s_new = gated_delta_rule(q_a, k_a, v_a, beta, g, s0.astype(F32), chunk)
    o_a = rms_norm(o_a, P['norm_a'][l]) * jax.nn.silu(z_a.astype(F32).reshape(Bn, T, H_A, DV_A))
    o_a = o_a.reshape(Bn, T, H_A * DV_A).astype(x.dtype)
    lam_init = 0.8 - 0.6 * math.exp(-0.3 * l)
    lp = P['lam'][l].astype(F32)
    lam_val = jnp.exp(jnp.sum(lp[0] * lp[1])) - jnp.exp(jnp.sum(lp[2] * lp[3])) + lam_init
    q_b = q_b.reshape(Bn, T, H_B, 2, DH_B)
    k_b = k_b.reshape(Bn, T, H_B, 2, DH_B)
    v_b = v_b.reshape(Bn, T, H_B, DHV_B)
    if past_k is None:
        o_b = prompt_diff_attention(q_b, k_b, v_b, lam_val, P['rel_bias'])
    else:
        past = past_k.shape[1]
        k_all = jnp.concatenate([past_k.reshape(Bn, past, H_B, 2, DH_B).astype(x.dtype), k_b], axis=1)
        v_all = jnp.concatenate([past_v.astype(x.dtype), v_b], axis=1)
        o_b = diff_attend(q_b, k_all, v_all, past + jnp.arange(T), jnp.arange(past + T), lam_val, P['rel_bias'])
    o_b = rms_norm(o_b, P['subln_g'][l]) * (1.0 - lam_init)
    o_b = o_b.reshape(Bn, T, H_B * DHV_B).astype(x.dtype)
    g_a, g_b = jnp.split(gates, 2, axis=-1)
    merged = jax.nn.sigmoid(g_a) * o_a + jax.nn.sigmoid(g_b) * o_b
    y = jnp.einsum('btd,de->bte', merged, P['w_o'][l])
    x = layer_norm(ALPHA * x + gt1 * y, P['ln1_g'][l], P['ln1_b'][l])
    h = x * (1.0 + sc2) + sh2
    u, v = jnp.split(jnp.einsum('btd,df->btf', h, P['w_ff_in'][l]), 2, axis=-1)
    y = jnp.einsum('btf,fd->btd', jax.nn.silu(u) * v, P['w_ff_out'][l])
    x = layer_norm(ALPHA * x + gt2 * y, P['ln2_g'][l], P['ln2_b'][l])
    return x, new_buf, s_new.astype(x.dtype), k_b.reshape(Bn, T, H_B, 2 * DH_B), v_b


def setup_inputs(seed: int = 0) -> dict:
    key = jax.random.key(seed)
    ks = jax.random.split(key, 32)

    def nrm(k, shape, s):
        return s * jax.random.normal(k, shape, F32)

    D = D_MODEL
    b_ada = nrm(ks[12], (DEPTH, 6 * D), 0.02)
    b_ada = b_ada.at[:, 2 * D:3 * D].add(1.0).at[:, 5 * D:6 * D].add(1.0)
    dt = jnp.exp(jax.random.uniform(ks[16], (DEPTH, H_A), F32, math.log(1e-3), math.log(1e-1)))
    return {
        'x_prompt': nrm(ks[0], (BATCH, SEQ, D), 1.0),
        'x_sample': nrm(ks[1], (DEC_BATCH, DEC_SEQ, D), 1.0),
        'cache_k': nrm(ks[2], (DEPTH, DEC_BATCH, PAST_LEN, H_B, 2 * DH_B), 1.0),
        'cache_v': nrm(ks[3], (DEPTH, DEC_BATCH, PAST_LEN, H_B, DHV_B), 1.0),
        'state_conv': nrm(ks[4], (DEPTH, DEC_BATCH, CONV_W - 1, CONV_CH), 1.0),
        'state_delta': nrm(ks[5], (DEPTH, DEC_BATCH, H_A, DK_A, DV_A), 0.05),
        'c_prompt': nrm(ks[6], (BATCH, D), 1.0),
        'c_sample': nrm(ks[7], (DEC_BATCH, D), 1.0),
        'ln_in_g': 1.0 + nrm(ks[8], (D,), 0.02),
        'ln_in_b': nrm(ks[9], (D,), 0.02),
        'rel_bias': nrm(ks[10], (N_BUCKETS, H_B), 0.5),
        'w_ada': nrm(ks[11], (DEPTH, D, 6 * D), 0.5 * D ** -0.5),
        'b_ada': b_ada,
        'w_in': nrm(ks[13], (DEPTH, D, D_IN), D ** -0.5),
        'conv_w': nrm(ks[14], (DEPTH, CONV_W, CONV_CH), CONV_W ** -0.5),
        'a_log': jnp.log(jax.random.uniform(ks[15], (DEPTH, H_A), F32, 1.0, 16.0)),
        'dt_bias': dt + jnp.log(-jnp.expm1(-dt)),
        'norm_a': 1.0 + nrm(ks[17], (DEPTH, DV_A), 0.02),
        'lam': nrm(ks[18], (DEPTH, 4, DH_B), 0.1),
        'subln_g': 1.0 + nrm(ks[19], (DEPTH, DHV_B), 0.02),
        'w_o': nrm(ks[20], (DEPTH, D, D), D ** -0.5 * BETA_INIT),
        'ln1_g': 1.0 + nrm(ks[21], (DEPTH, D), 0.02),
        'ln1_b': nrm(ks[22], (DEPTH, D), 0.02),
        'w_ff_in': nrm(ks[23], (DEPTH, D, 2 * D_FF), D ** -0.5),
        'w_ff_out': nrm(ks[24], (DEPTH, D_FF, D), D_FF ** -0.5 * BETA_INIT),
        'ln2_g': 1.0 + nrm(ks[25], (DEPTH, D), 0.02),
        'ln2_b': nrm(ks[26], (DEPTH, D), 0.02),
    }


def reference(x_prompt, x_sample, cache_k, cache_v, state_conv, state_delta, c_prompt, c_sample,
              ln_in_g, ln_in_b, rel_bias, w_ada, b_ada, w_in, conv_w, a_log, dt_bias, norm_a,
              lam, subln_g, w_o, ln1_g, ln1_b, w_ff_in, w_ff_out, ln2_g, ln2_b):
    P = {'rel_bias': rel_bias, 'w_ada': w_ada, 'b_ada': b_ada, 'w_in': w_in, 'conv_w': conv_w,
         'a_log': a_log, 'dt_bias': dt_bias, 'norm_a': norm_a, 'lam': lam, 'subln_g': subln_g,
         'w_o': w_o, 'ln1_g': ln1_g, 'ln1_b': ln1_b, 'w_ff_in': w_ff_in, 'w_ff_out': w_ff_out,
         'ln2_g': ln2_g, 'ln2_b': ln2_b}
    xp = layer_norm(x_prompt, ln_in_g, ln_in_b)
    xs = layer_norm(x_sample, ln_in_g, ln_in_b)
    bp = xp.shape[0]
    conv0 = jnp.zeros((bp, CONV_W - 1, CONV_CH), xp.dtype)
    s0 = jnp.zeros((bp, H_A, DK_A, DV_A), F32)
    kp_l, vp_l, cp_l, sp_l = [], [], [], []
    ks_l, vs_l, cs_l, ss_l = [], [], [], []
    for l in range(DEPTH):
        xp, cbp, sp, kp, vp = trunk_layer(xp, c_prompt, l, conv0, s0, None, None, P)
        xs, cbs, ss, kk, vv = trunk_layer(xs, c_sample, l, state_conv[l], state_delta[l], cache_k[l], cache_v[l], P)
        kp_l.append(kp); vp_l.append(vp); cp_l.append(cbp); sp_l.append(sp)
        ks_l.append(kk); vs_l.append(vv); cs_l.append(cbs); ss_l.append(ss)
    return (xp, xs,
            jnp.stack(kp_l), jnp.stack(vp_l), jnp.stack(cp_l), jnp.stack(sp_l),
            jnp.stack(ks_l), jnp.stack(vs_l), jnp.stack(cs_l), jnp.stack(ss_l))
```

```python
import functools
import math

import jax
import jax.numpy as jnp
from jax import lax
from jax.experimental import pallas as pl
from jax.experimental.pallas import tpu as pltpu

F32 = jnp.float32
BF16 = jnp.bfloat16
HIGHEST = lax.Precision.HIGHEST

LN_EPS = 1e-5
CHUNK = 64
HEAD = 128
CONV_W = 4
N_BUCKETS = 32
MAX_DIST = 128
NEG = -1e30
LANES = 128
CONV_PAD = 8
VMEM_LIMIT = 56 * 1024 * 1024

TM_PROJ = 512
TQ = 256


def _cparams(sem):
    return pltpu.CompilerParams(dimension_semantics=sem, vmem_limit_bytes=VMEM_LIMIT)


def _sigmoid(x):
    return 1.0 / (1.0 + jnp.exp(-x))


def _silu(x):
    return x * _sigmoid(x)


def _layer_norm(x, g, b):
    mu = jnp.mean(x, axis=-1, keepdims=True)
    xc = x - mu
    var = jnp.mean(xc * xc, axis=-1, keepdims=True)
    return xc * lax.rsqrt(var + LN_EPS) * g + b


def _dot(a, b):
    return jnp.dot(a, b, preferred_element_type=F32)


def _hdot(a, b):
    return jnp.dot(a, b, preferred_element_type=F32, precision=HIGHEST)


def _dot_nt(a, b):
    return lax.dot_general(a, b, (((1,), (1,)), ((), ())), preferred_element_type=F32)


def _dot_tn(a, b):
    return lax.dot_general(a, b, (((0,), (0,)), ((), ())), preferred_element_type=F32)


def _ln_kernel(x_ref, g_ref, b_ref, o_ref):
    o_ref[...] = _layer_norm(x_ref[...], g_ref[...], b_ref[...])


def _ln_call(x, g, b, tm):
    n, d = x.shape
    return pl.pallas_call(
        _ln_kernel,
        out_shape=jax.ShapeDtypeStruct((n, d), F32),
        grid=(n // tm,),
        in_specs=[pl.BlockSpec((tm, d), lambda i: (i, 0)),
                  pl.BlockSpec((1, d), lambda i: (0, 0)),
                  pl.BlockSpec((1, d), lambda i: (0, 0))],
        out_specs=pl.BlockSpec((tm, d), lambda i: (i, 0)),
        compiler_params=_cparams(("parallel",)),
        name="ln_in",
    )(x, g.reshape(1, d), b.reshape(1, d))


def _ada_kernel(c_ref, w_ref, b_ref, o_ref):
    c = c_ref[...]
    o_ref[...] = _dot(_silu(c).astype(BF16), w_ref[...]) + b_ref[...]


def _ada_call(c_all, w_ada, b_ada):
    depth, d, d6 = w_ada.shape
    nb = c_all.shape[0]
    return pl.pallas_call(
        _ada_kernel,
        out_shape=jax.ShapeDtypeStruct((depth, nb, d6), F32),
        grid=(depth, d6 // d),
        in_specs=[pl.BlockSpec((nb, d), lambda l, j: (0, 0)),
                  pl.BlockSpec((None, d, d), lambda l, j: (l, 0, j)),
                  pl.BlockSpec((None, 1, d), lambda l, j: (l, 0, j))],
        out_specs=pl.BlockSpec((None, nb, d), lambda l, j: (l, 0, j)),
        compiler_params=_cparams(("parallel", "parallel")),
        name="ada",
    )(c_all, w_ada, b_ada.reshape(depth, 1, d6))


def _mod_spec(mod, d, tiles_per_group, chunk):
    rows = mod.shape[1]
    return pl.BlockSpec((None, rows, d), lambda i, *_: (i // tiles_per_group, 0, chunk))


N_MAIN_TILES = 6
N_W_TILES = 9


def _inproj_kernel(x_ref, sc_ref, sh_ref, w_ref, wba_ref,
                   main_ref, q_ref, k_ref, v_ref, ba_ref, h_scr):
    j = pl.program_id(1)

    @pl.when(j == 0)
    def _():
        h_scr[...] = (x_ref[...] * (1.0 + sc_ref[...]) + sh_ref[...]).astype(BF16)

    @pl.when(j < N_MAIN_TILES)
    def _():
        main_ref[...] = _dot(h_scr[...], w_ref[...])

    @pl.when(j == N_MAIN_TILES)
    def _():
        q_ref[...] = _dot(h_scr[...], w_ref[...]).astype(BF16)

    @pl.when(j == N_MAIN_TILES + 1)
    def _():
        k_ref[...] = _dot(h_scr[...], w_ref[...])

    @pl.when(j == N_MAIN_TILES + 2)
    def _():
        v_ref[...] = _dot(h_scr[...], w_ref[...])

    @pl.when(j == N_W_TILES)
    def _():
        ba_ref[...] = _dot(h_scr[...], wba_ref[...])


def _inproj_call(x, mod, w_all, w_ba, kbuf, vbuf, layer, tm, tiles_per_group):
    n, d = x.shape
    depth = w_all.shape[0]
    nba = w_ba.shape[-1]
    grid = (n // tm, N_W_TILES + 1)
    in_specs = [
        pl.BlockSpec((tm, d), lambda i, j: (i, 0)),
        _mod_spec(mod, d, tiles_per_group, 1),
        _mod_spec(mod, d, tiles_per_group, 0),
        pl.BlockSpec((None, d, d), lambda i, j: (layer, 0, jnp.minimum(j, N_W_TILES - 1))),
        pl.BlockSpec((None, d, nba), lambda i, j: (layer, 0, 0)),
    ]
    out_shape = [
        jax.ShapeDtypeStruct((n, N_MAIN_TILES * d), F32),
        jax.ShapeDtypeStruct((n, d), BF16),
        jax.ShapeDtypeStruct((depth, n, d), F32),
        jax.ShapeDtypeStruct((depth, n, d), F32),
        jax.ShapeDtypeStruct((n, nba), F32),
    ]
    out_specs = [
        pl.BlockSpec((tm, d), lambda i, j: (i, jnp.minimum(j, N_MAIN_TILES - 1))),
        pl.BlockSpec((tm, d), lambda i, j: (i, 0)),
        pl.BlockSpec((None, tm, d), lambda i, j: (layer, i, 0)),
        pl.BlockSpec((None, tm, d), lambda i, j: (layer, i, 0)),
        pl.BlockSpec((tm, nba), lambda i, j: (i, 0)),
    ]
    args = [x, mod, mod, w_all, w_ba]
    aliases = {}
    if kbuf is not None:
        in_specs += [pl.BlockSpec(memory_space=pl.ANY), pl.BlockSpec(memory_space=pl.ANY)]
        args += [kbuf, vbuf]
        aliases = {5: 2, 6: 3}

    def body(*refs):
        if kbuf is not None:
            refs = refs[:5] + refs[7:]
        _inproj_kernel(*refs)

    return pl.pallas_call(
        body,
        out_shape=out_shape,
        grid=grid,
        in_specs=in_specs,
        out_specs=out_specs,
        scratch_shapes=[pltpu.VMEM((tm, d), BF16)],
        input_output_aliases=aliases,
        compiler_params=_cparams(("parallel", "arbitrary")),
        name="inproj",
    )(*args)


def _tri_inverse(a, eye, blk):
    s = 8
    d = jnp.where(blk[s], a, 0.0)
    d2 = _hdot(d, d)
    d4 = _hdot(d2, d2)
    t = _hdot(_hdot(eye - d, eye + d2), eye + d4)
    size = a.shape[0]
    while s < size:
        off = jnp.where(jnp.logical_and(blk[2 * s], jnp.logical_not(blk[s])), a, 0.0)
        t = t - _hdot(_hdot(t, off), t)
        s *= 2
    return t


def _delta_kernel(qu_ref, ku_ref, vu_ref, z_ref, ga_ref, ba_ref,
                  cwq_ref, cwk_ref, cwv_ref, par_ref, cbq_ref, cbk_ref, cbv_ref, s0_ref,
                  ma_ref, sn_ref, s_scr, cat_scr, *, chunk, cps, hb):
    c = pl.program_id(2)
    rows_blk = chunk * cps
    tail = CONV_W - 1

    @pl.when(c == 0)
    def _():
        s_scr[...] = s0_ref[...]
        for s, cb in enumerate((cbq_ref, cbk_ref, cbv_ref)):
            cat_scr[s, CONV_PAD - tail:CONV_PAD, :] = cb[...]

    convs = []
    for s, (u_ref, cw_ref) in enumerate(((qu_ref, cwq_ref), (ku_ref, cwk_ref), (vu_ref, cwv_ref))):
        cat_scr[s, CONV_PAD:CONV_PAD + rows_blk, :] = u_ref[...]
        acc = None
        for i in range(CONV_W):
            term = cat_scr[s, pl.ds(CONV_PAD - tail + i, rows_blk), :] * cw_ref[i:i + 1, :]
            acc = term if acc is None else acc + term
        convs.append(_silu(acc))
        cat_scr[s, CONV_PAD - tail:CONV_PAD, :] = u_ref[rows_blk - tail:rows_blk, :]

    ba = ba_ref[...]
    dt_row = par_ref[0:1, :]
    alog_row = par_ref[1:2, :]
    norm_row = par_ref[2:3, 0:HEAD]
    beta_all = _sigmoid(ba)
    xa = ba + dt_row
    softplus = jnp.maximum(xa, 0.0) + jnp.log(1.0 + jnp.exp(-jnp.abs(xa)))
    g_all = -jnp.exp(alog_row) * softplus

    r = lax.broadcasted_iota(jnp.int32, (chunk, chunk), 0)
    cc = lax.broadcasted_iota(jnp.int32, (chunk, chunk), 1)
    causal = r >= cc
    strict = r > cc
    eye = jnp.where(r == cc, 1.0, 0.0).astype(F32)
    tril = jnp.where(causal, 1.0, 0.0).astype(F32)
    blk = {}
    s = 8
    while s <= chunk:
        blk[s] = jnp.bitwise_xor(r, cc) < s
        s *= 2

    for ci in range(cps):
        rs = slice(ci * chunk, (ci + 1) * chunk)
        g_chunk = g_all[rs, :]
        gcum_all = _hdot(tril, g_chunk)
        for hh in range(hb):
            cs = slice(hh * HEAD, (hh + 1) * HEAD)
            q = convs[0][rs, cs]
            k = convs[1][rs, cs]
            v = convs[2][rs, cs]
            q = q * (lax.rsqrt(jnp.sum(q * q, axis=-1, keepdims=True) + 1e-6) * (HEAD ** -0.5))
            k = k * lax.rsqrt(jnp.sum(k * k, axis=-1, keepdims=True) + 1e-6)
            beta = beta_all[rs, hh:hh + 1]
            g_col = g_chunk[:, hb + hh:hb + hh + 1]
            gcum = gcum_all[:, hb + hh:hb + hh + 1]
            g_last = gcum[chunk - 1:chunk, :]
            diff = _hdot(tril, jnp.where(strict, g_col, 0.0))
            decay = jnp.where(causal, jnp.exp(jnp.where(causal, diff, 0.0)), 0.0)
            e_g = jnp.exp(gcum)
            kb = k * beta
            a_mat = jnp.where(strict, _dot_nt(kb, k) * decay, 0.0)
            t_inv = _tri_inverse(a_mat, eye, blk)
            rhs = jnp.concatenate([v * beta, kb * e_g], axis=1)
            uw = _hdot(t_inv, rhs)
            u_v = uw[:, 0:HEAD]
            w_k = uw[:, HEAD:2 * HEAD]
            qk = _dot_nt(q, k) * decay
            q_g = q * e_g
            k_tail = k * jnp.exp(g_last - gcum)
            state = s_scr[hh]
            u = u_v - _dot(w_k, state)
            o = _dot(q_g, state) + _dot(qk, u)
            s_scr[hh] = jnp.exp(g_last) * state + _dot_tn(k_tail, u)
            o = o * lax.rsqrt(jnp.mean(o * o, axis=-1, keepdims=True) + LN_EPS) * norm_row
            o = o * _silu(z_ref[rs, cs])
            ma_ref[rs, cs] = _sigmoid(ga_ref[rs, cs]) * o

    @pl.when(c == pl.num_programs(2) - 1)
    def _():
        sn_ref[...] = s_scr[...]


def _delta_call(main, ba, conv_w, par, conv_buf, s0, layer, nb, t, chunk, cps, hb):
    n = main.shape[0]
    d = main.shape[1] // N_MAIN_TILES
    h = d // HEAD
    ng = h // hb
    wb = hb * HEAD
    rows_blk = chunk * cps
    ncb = t // rows_blk
    grid = (nb, ng, ncb)

    def seg(s):
        return pl.BlockSpec((rows_blk, wb), lambda b, g, c: (b * ncb + c, s * ng + g))

    def cw(s):
        return pl.BlockSpec((None, CONV_W, wb), lambda b, g, c: (layer, 0, s * ng + g))

    def cb(s):
        return pl.BlockSpec((None, CONV_W - 1, wb), lambda b, g, c: (b, 0, s * ng + g))

    in_specs = [seg(0), seg(1), seg(2), seg(3), seg(4),
                pl.BlockSpec((rows_blk, LANES), lambda b, g, c: (b * ncb + c, g)),
                cw(0), cw(1), cw(2),
                pl.BlockSpec((None, 8, LANES), lambda b, g, c: (layer, 0, g)),
                cb(0), cb(1), cb(2),
                pl.BlockSpec((None, hb, HEAD, HEAD), lambda b, g, c: (b, g, 0, 0))]
    out_shape = [jax.ShapeDtypeStruct((n, d), F32),
                 jax.ShapeDtypeStruct((nb, h, HEAD, HEAD), F32)]
    out_specs = [pl.BlockSpec((rows_blk, wb), lambda b, g, c: (b * ncb + c, g)),
                 pl.BlockSpec((None, hb, HEAD, HEAD), lambda b, g, c: (b, g, 0, 0))]
    return pl.pallas_call(
        functools.partial(_delta_kernel, chunk=chunk, cps=cps, hb=hb),
        out_shape=out_shape,
        grid=grid,
        in_specs=in_specs,
        out_specs=out_specs,
        scratch_shapes=[pltpu.VMEM((hb, HEAD, HEAD), F32),
                        pltpu.VMEM((3, CONV_PAD + rows_blk, wb), F32)],
        compiler_params=_cparams(("parallel", "parallel", "arbitrary")),
        name="delta",
    )(main, main, main, main, main, ba, conv_w, conv_w, conv_w, par,
      conv_buf, conv_buf, conv_buf, s0)


def _bucket_thresholds():
    nb = N_BUCKETS // 2
    max_exact = nb // 2
    ratio = MAX_DIST // max_exact
    steps = nb - max_exact
    out = []
    for kk in range(1, steps):
        n = max_exact
        while n ** steps < (max_exact ** steps) * (ratio ** kk):
            n += 1
        out.append(n)
    return out


def _far_distance():
    return _bucket_thresholds()[-1]


def _rel_bias_tile(tab_ref, head, qpos, kpos):
    nb = N_BUCKETS // 2
    max_exact = nb // 2
    rel = kpos - qpos
    n = jnp.abs(rel)
    large = jnp.full(rel.shape, max_exact, jnp.int32)
    for thr in _bucket_thresholds():
        large = large + jnp.where(n >= thr, 1, 0)
    bucket = jnp.where(rel > 0, nb, 0) + jnp.where(n < max_exact, n, large)
    far = tab_ref[nb - 1, head]
    bias = jnp.zeros(rel.shape, F32)
    for b in range(N_BUCKETS):
        bias = jnp.where(bucket == b, tab_ref[b, head] - far, bias)
    shift = CHUNK.bit_length() - 1
    mask = lax.shift_right_logical(kpos, shift) <= lax.shift_right_logical(qpos, shift)
    return jnp.where(mask, bias, NEG)


def _bias_kernel(tab_ref, pt_ref, st_ref, *, tq, past, ts):
    head = pl.program_id(0)
    qi = lax.broadcasted_iota(jnp.int32, (tq, tq), 0)
    ki = lax.broadcasted_iota(jnp.int32, (tq, tq), 1)
    pt_ref[0] = _rel_bias_tile(tab_ref, head, qi + tq, ki + tq)
    pt_ref[1] = _rel_bias_tile(tab_ref, head, qi + tq, ki)
    wk = past + LANES
    qs = lax.broadcasted_iota(jnp.int32, (ts, wk), 0) + past
    ks = lax.broadcasted_iota(jnp.int32, (ts, wk), 1)
    st = _rel_bias_tile(tab_ref, head, qs, ks)
    st_ref[...] = jnp.where(ks < past + ts, st, NEG)


def _bias_call(rel_bias, tq, past, ts):
    h = rel_bias.shape[1]
    wk = past + LANES
    return pl.pallas_call(
        functools.partial(_bias_kernel, tq=tq, past=past, ts=ts),
        out_shape=[jax.ShapeDtypeStruct((h, 2, tq, tq), F32),
                   jax.ShapeDtypeStruct((h, ts, wk), F32)],
        grid=(h,),
        in_specs=[pl.BlockSpec(memory_space=pltpu.SMEM)],
        out_specs=[pl.BlockSpec((None, 2, tq, tq), lambda i: (i, 0, 0, 0)),
                   pl.BlockSpec((None, ts, wk), lambda i: (i, 0, 0))],
        compiler_params=_cparams(("arbitrary",)),
        name="rel_bias",
    )(rel_bias)


def _lam_value(lam_ref, lam_init):
    lp = lam_ref[...]
    s1 = jnp.sum(lp[0:1, :] * lp[1:2, :], axis=-1, keepdims=True)
    s2 = jnp.sum(lp[2:3, :] * lp[3:4, :], axis=-1, keepdims=True)
    return jnp.exp(s1) - jnp.exp(s2) + lam_init


def _stack_maps(q):
    lane = lax.broadcasted_iota(jnp.int32, q.shape, 1)
    zero = jnp.zeros_like(q)
    half = HEAD // 2
    return jnp.concatenate([jnp.where(lane < half, q, zero), jnp.where(lane >= half, q, zero)], axis=0)


def _finish_attn(acc, l, lam_val, lam_init, subln, ma, gb, tq):
    o = acc[0:tq] / l[0:tq] - lam_val * (acc[tq:2 * tq] / l[tq:2 * tq])
    o = o * lax.rsqrt(jnp.mean(o * o, axis=-1, keepdims=True) + LN_EPS) * subln * (1.0 - lam_init)
    return (ma + _sigmoid(gb) * o).astype(BF16)


def _attn_kernel(q_ref, k_ref, v_ref, bias_ref, lam_ref, subln_ref, ma_ref, gb_ref,
                 out_ref, kbf, vbf, m_scr, l_scr, acc_scr, *, tq, lam_init):
    i = pl.program_id(2)

    @pl.when(i == 0)
    def _():
        kbf[...] = k_ref[...].astype(BF16)
        vbf[...] = v_ref[...].astype(BF16)

    q2 = _stack_maps(q_ref[...])
    m_scr[...] = jnp.full(m_scr.shape, NEG, F32)
    l_scr[...] = jnp.zeros(l_scr.shape, F32)
    acc_scr[...] = jnp.zeros(acc_scr.shape, F32)

    def block(j, bias):
        start = pl.multiple_of(j * tq, tq)
        kj = kbf[pl.ds(start, tq), :]
        vj = vbf[pl.ds(start, tq), :]
        s = _dot_nt(q2, kj)
        if bias is not None:
            s = s + jnp.concatenate([bias, bias], axis=0)
        m_old = m_scr[...]
        m_new = jnp.maximum(m_old, jnp.max(s, axis=-1, keepdims=True))
        alpha = jnp.exp(m_old - m_new)
        p = jnp.exp(s - m_new)
        l_scr[...] = alpha * l_scr[...] + jnp.sum(p, axis=-1, keepdims=True)
        acc_scr[...] = alpha * acc_scr[...] + _dot(p.astype(BF16), vj)
        m_scr[...] = m_new

    block(i, bias_ref[0])

    @pl.when(i > 0)
    def _():
        block(i - 1, bias_ref[1])

    def far(j, carry):
        block(j, None)
        return carry

    lax.fori_loop(0, jnp.maximum(i - 1, 0), far, 0)

    lam_val = _lam_value(lam_ref, lam_init)
    out_ref[...] = _finish_attn(acc_scr[...], l_scr[...], lam_val, lam_init, subln_ref[...],
                                ma_ref[...], gb_ref[...], tq)


def _attn_call(q, kbuf, vbuf, bias, lam, subln, ma, main, layer, nb, t, tq, lam_init):
    n, d = q.shape
    h = d // HEAD
    nq = t // tq
    assert tq + 1 >= _far_distance() and tq % CHUNK == 0
    gb_col = (N_MAIN_TILES - 1) * h
    row = lambda b, hh, i: (b * nq + i, hh)
    in_specs = [
        pl.BlockSpec((tq, HEAD), row),
        pl.BlockSpec((None, t, HEAD), lambda b, hh, i: (layer, b, hh)),
        pl.BlockSpec((None, t, HEAD), lambda b, hh, i: (layer, b, hh)),
        pl.BlockSpec((None, 2, tq, tq), lambda b, hh, i: (hh, 0, 0, 0)),
        pl.BlockSpec((None, 4, HEAD // 2), lambda b, hh, i: (layer, 0, 0)),
        pl.BlockSpec((None, 1, HEAD), lambda b, hh, i: (layer, 0, 0)),
        pl.BlockSpec((tq, HEAD), row),
        pl.BlockSpec((tq, HEAD), lambda b, hh, i: (b * nq + i, gb_col + hh)),
    ]
    return pl.pallas_call(
        functools.partial(_attn_kernel, tq=tq, lam_init=lam_init),
        out_shape=jax.ShapeDtypeStruct((n, d), BF16),
        grid=(nb, h, nq),
        in_specs=in_specs,
        out_specs=pl.BlockSpec((tq, HEAD), row),
        scratch_shapes=[pltpu.VMEM((t, HEAD), BF16), pltpu.VMEM((t, HEAD), BF16),
                        pltpu.VMEM((2 * tq, 1), F32), pltpu.VMEM((2 * tq, 1), F32),
                        pltpu.VMEM((2 * tq, HEAD), F32)],
        compiler_params=_cparams(("parallel", "parallel", "arbitrary")),
        name="attn",
    )(q, kbuf, vbuf, bias, lam, subln, ma, main)


def _attn_step_kernel(q_ref, kc_ref, vc_ref, kn_ref, vn_ref, bias_ref, lam_ref, subln_ref,
                      ma_ref, gb_ref, out_ref, *, ts, past, lam_init):
    q2 = _stack_maps(q_ref[...])
    pad = jnp.zeros((LANES - ts, HEAD), F32)
    kc = kc_ref[...].astype(BF16)
    vc = vc_ref[...].astype(BF16)
    kn = jnp.concatenate([kn_ref[...], pad], axis=0).astype(BF16)
    vn = jnp.concatenate([vn_ref[...], pad], axis=0).astype(BF16)
    bias = bias_ref[...]
    bias2 = jnp.concatenate([bias, bias], axis=0)
    s_c = _dot_nt(q2, kc) + bias2[:, 0:past]
    s_n = _dot_nt(q2, kn) + bias2[:, past:past + LANES]
    m = jnp.maximum(jnp.max(s_c, axis=-1, keepdims=True), jnp.max(s_n, axis=-1, keepdims=True))
    p_c = jnp.exp(s_c - m)
    p_n = jnp.exp(s_n - m)
    l = jnp.sum(p_c, axis=-1, keepdims=True) + jnp.sum(p_n, axis=-1, keepdims=True)
    acc = _dot(p_c.astype(BF16), vc) + _dot(p_n.astype(BF16), vn)
    lam_val = _lam_value(lam_ref, lam_init)
    out_ref[...] = _finish_attn(acc, l, lam_val, lam_init, subln_ref[...], ma_ref[...], gb_ref[...], ts)


def _attn_step_call(q, cache_k, cache_v, kbuf, vbuf, bias, lam, subln, ma, main, layer, nb, ts, lam_init):
    n, d = q.shape
    h = d // HEAD
    past = cache_k.shape[1] // nb
    gb_col = (N_MAIN_TILES - 1) * h
    row = lambda b, hh: (b, hh)
    lrow = lambda b, hh: (layer, b, hh)
    in_specs = [
        pl.BlockSpec((ts, HEAD), row),
        pl.BlockSpec((None, past, HEAD), lrow),
        pl.BlockSpec((None, past, HEAD), lrow),
        pl.BlockSpec((None, ts, HEAD), lrow),
        pl.BlockSpec((None, ts, HEAD), lrow),
        pl.BlockSpec((None, ts, past + LANES), lambda b, hh: (hh, 0, 0)),
        pl.BlockSpec((None, 4, HEAD // 2), lambda b, hh: (layer, 0, 0)),
        pl.BlockSpec((None, 1, HEAD), lambda b, hh: (layer, 0, 0)),
        pl.BlockSpec((ts, HEAD), row),
        pl.BlockSpec((ts, HEAD), lambda b, hh: (b, gb_col + hh)),
    ]
    return pl.pallas_call(
        functools.partial(_attn_step_kernel, ts=ts, past=past, lam_init=lam_init),
        out_shape=jax.ShapeDtypeStruct((n, d), BF16),
        grid=(nb, h),
        in_specs=in_specs,
        out_specs=pl.BlockSpec((ts, HEAD), row),
        compiler_params=_cparams(("parallel", "parallel")),
        name="attn_step",
    )(q, cache_k, cache_v, kbuf, vbuf, bias, lam, subln, ma, main)


def _wo_kernel(m_ref, x_ref, gt_ref, w_ref, g_ref, b_ref, o_ref, *, alpha):
    y = _dot(m_ref[...], w_ref[...])
    o_ref[...] = _layer_norm(alpha * x_ref[...] + gt_ref[...] * y, g_ref[...], b_ref[...])


def _wo_call(merged, x, mod, w_o, ln_g, ln_b, layer, tm, tiles_per_group, alpha):
    n, d = x.shape
    vec = pl.BlockSpec((None, 1, d), lambda i: (layer, 0, 0))
    return pl.pallas_call(
        functools.partial(_wo_kernel, alpha=alpha),
        out_shape=jax.ShapeDtypeStruct((n, d), F32),
        grid=(n // tm,),
        in_specs=[pl.BlockSpec((tm, d), lambda i: (i, 0)),
                  pl.BlockSpec((tm, d), lambda i: (i, 0)),
                  _mod_spec(mod, d, tiles_per_group, 2),
                  pl.BlockSpec((None, d, d), lambda i: (layer, 0, 0)),
                  vec, vec],
        out_specs=pl.BlockSpec((tm, d), lambda i: (i, 0)),
        compiler_params=_cparams(("parallel",)),
        name="wo",
    )(merged, x, mod, w_o, ln_g, ln_b)


def _ffn_kernel(x_ref, sc_ref, sh_ref, gt_ref, wu_ref, wv_ref, wo_ref, g_ref, b_ref,
                o_ref, h_scr, acc_scr, *, alpha):
    j = pl.program_id(1)

    @pl.when(j == 0)
    def _():
        h_scr[...] = (x_ref[...] * (1.0 + sc_ref[...]) + sh_ref[...]).astype(BF16)
        acc_scr[...] = jnp.zeros(acc_scr.shape, F32)

    h = h_scr[...]
    u = _dot(h, wu_ref[...])
    v = _dot(h, wv_ref[...])
    acc_scr[...] += _dot((_silu(u) * v).astype(BF16), wo_ref[...])

    @pl.when(j == pl.num_programs(1) - 1)
    def _():
        o_ref[...] = _layer_norm(alpha * x_ref[...] + gt_ref[...] * acc_scr[...], g_ref[...], b_ref[...])


def _ffn_call(x, mod, w_ff_in, w_ff_out, ln_g, ln_b, layer, tm, tiles_per_group, alpha, nf):
    n, d = x.shape
    dff = w_ff_out.shape[1]
    tf = dff // nf
    vec = pl.BlockSpec((None, 1, d), lambda i, j: (layer, 0, 0))
    return pl.pallas_call(
        functools.partial(_ffn_kernel, alpha=alpha),
        out_shape=jax.ShapeDtypeStruct((n, d), F32),
        grid=(n // tm, nf),
        in_specs=[pl.BlockSpec((tm, d), lambda i, j: (i, 0)),
                  _mod_spec(mod, d, tiles_per_group, 4),
                  _mod_spec(mod, d, tiles_per_group, 3),
                  _mod_spec(mod, d, tiles_per_group, 5),
                  pl.BlockSpec((None, d, tf), lambda i, j: (layer, 0, j)),
                  pl.BlockSpec((None, d, tf), lambda i, j: (layer, 0, nf + j)),
                  pl.BlockSpec((None, tf, d), lambda i, j: (layer, j, 0)),
                  vec, vec],
        out_specs=pl.BlockSpec((tm, d), lambda i, j: (i, 0)),
        scratch_shapes=[pltpu.VMEM((tm, d), BF16), pltpu.VMEM((tm, d), F32)],
        compiler_params=_cparams(("parallel", "arbitrary")),
        name="ffn",
    )(x, mod, mod, mod, w_ff_in, w_ff_in, w_ff_out, ln_g, ln_b)


def _pick_tile(n, pref):
    t = min(n, pref)
    while n % t:
        t //= 2
    return t


def kernel(x_prompt, x_sample, cache_k, cache_v, state_conv, state_delta, c_prompt, c_sample,
           ln_in_g, ln_in_b, rel_bias, w_ada, b_ada, w_in, conv_w, a_log, dt_bias, norm_a,
           lam, subln_g, w_o, ln1_g, ln1_b, w_ff_in, w_ff_out, ln2_g, ln2_b):
    bp, tp, d = x_prompt.shape
    bs, ts, _ = x_sample.shape
    depth = w_in.shape[0]
    h = d // HEAD
    past = cache_k.shape[2]
    dff = w_ff_out.shape[1]
    n_p = bp * tp
    n_s = bs * ts
    alpha = (2 * depth) ** 0.25
    hb = h
    ng = h // hb

    o1 = 4 * d + 2 * h
    qkvz = w_in[:, :, 0:4 * d]
    gates = w_in[:, :, o1 + 3 * d:o1 + 5 * d]
    qb = w_in[:, :, o1:o1 + d] * (HEAD // 2) ** -0.5
    kvb = w_in[:, :, o1 + d:o1 + 3 * d]
    w_all = jnp.concatenate([qkvz, gates, qb, kvb], axis=-1).astype(BF16)
    wb = w_in[:, :, 4 * d:4 * d + h].reshape(depth, d, ng, hb)
    wa = w_in[:, :, 4 * d + h:4 * d + 2 * h].reshape(depth, d, ng, hb)
    w_ba = jnp.concatenate([wb, wa, jnp.zeros((depth, d, ng, LANES - 2 * hb), F32)], axis=-1)
    w_ba = w_ba.reshape(depth, d, ng * LANES).astype(BF16)
    w_ada_b = w_ada.astype(BF16)
    w_o_b = w_o.astype(BF16)
    w_ff_in_b = w_ff_in.astype(BF16)
    w_ff_out_b = w_ff_out.astype(BF16)

    def lane_rows(v):
        v = v.reshape(depth, ng, hb)
        z = jnp.zeros((depth, ng, hb), F32)
        zz = jnp.zeros((depth, ng, LANES - 2 * hb), F32)
        return jnp.concatenate([z, v, zz], axis=-1).reshape(depth, ng * LANES)

    par = jnp.zeros((depth, 8, ng * LANES), F32)
    par = par.at[:, 0].set(lane_rows(dt_bias)).at[:, 1].set(lane_rows(a_log))
    par = par.at[:, 2].set(jnp.tile(norm_a, (1, ng)))
    subln = subln_g.reshape(depth, 1, HEAD)
    ln1g, ln1b = ln1_g.reshape(depth, 1, d), ln1_b.reshape(depth, 1, d)
    ln2g, ln2b = ln2_g.reshape(depth, 1, d), ln2_b.reshape(depth, 1, d)

    mod = _ada_call(jnp.concatenate([c_prompt, c_sample], axis=0), w_ada_b, b_ada)
    mod_p = mod[:, :bp].reshape(depth, bp, 1, 6 * d)
    mod_s = jnp.repeat(mod[:, bp:], ts, axis=1).reshape(depth, 1, n_s, 6 * d)

    bias_p, bias_s = _bias_call(rel_bias, TQ, past, ts)

    tm_p = _pick_tile(tp, TM_PROJ)
    tpg_p = tp // tm_p
    xp = _ln_call(x_prompt.reshape(n_p, d), ln_in_g, ln_in_b, tm_p)
    xs = _ln_call(x_sample.reshape(n_s, d), ln_in_g, ln_in_b, n_s)

    conv0 = jnp.zeros((bp, CONV_W - 1, 3 * d), F32)
    s0 = jnp.zeros((bp, h, HEAD, HEAD), F32)
    cache_k2 = cache_k.reshape(depth, bs * past, d)
    cache_v2 = cache_v.reshape(depth, bs * past, d)
    nf = 2 if dff % (2 * LANES) == 0 else 1

    kp = vp = ks = vs = None
    conv_p, conv_s, st_p, st_s = [], [], [], []
    for l in range(depth):
        lam_init = 0.8 - 0.6 * math.exp(-0.3 * l)
        main, qb_p, kp, vp, ba = _inproj_call(xp, mod_p[l], w_all, w_ba, kp, vp, l, tm_p, tpg_p)
        ma, sn = _delta_call(main, ba, conv_w, par, conv0, s0, l, bp, tp, CHUNK, 1, hb)
        merged = _attn_call(qb_p, kp, vp, bias_p, lam, subln, ma, main, l, bp, tp, TQ, lam_init)
        x1 = _wo_call(merged, xp, mod_p[l], w_o_b, ln1g, ln1b, l, tm_p, tpg_p, alpha)
        xp = _ffn_call(x1, mod_p[l], w_ff_in_b, w_ff_out_b, ln2g, ln2b, l, tm_p, tpg_p, alpha, nf)
        conv_p.append(main.reshape(bp, tp, -1)[:, tp - (CONV_W - 1):, 0:3 * d])
        st_p.append(sn)
        main, qb_s, ks, vs, ba = _inproj_call(xs, mod_s[l], w_all, w_ba, ks, vs, l, n_s, 1)
        ma, sn = _delta_call(main, ba, conv_w, par, state_conv[l], state_delta[l], l, bs, ts, ts, 1, hb)
        merged = _attn_step_call(qb_s, cache_k2, cache_v2, ks, vs, bias_s, lam, subln, ma, main,
                                 l, bs, ts, lam_init)
        x1 = _wo_call(merged, xs, mod_s[l], w_o_b, ln1g, ln1b, l, n_s, 1, alpha)
        xs = _ffn_call(x1, mod_s[l], w_ff_in_b, w_ff_out_b, ln2g, ln2b, l, n_s, 1, alpha, nf)
        conv_s.append(main.reshape(bs, ts, -1)[:, ts - (CONV_W - 1):, 0:3 * d])
        st_s.append(sn)

    return (xp.reshape(bp, tp, d), xs.reshape(bs, ts, d),
            kp.reshape(depth, bp, tp, h, HEAD), vp.reshape(depth, bp, tp, h, HEAD),
            jnp.stack(conv_p), jnp.stack(st_p),
            ks.reshape(depth, bs, ts, h, HEAD), vs.reshape(depth, bs, ts, h, HEAD),
            jnp.stack(conv_s), jnp.stack(st_s))
```

```python
import functools
import math

import jax
import jax.numpy as jnp
from jax import lax
from jax.experimental import pallas as pl
from jax.experimental.pallas import tpu as pltpu

F32 = jnp.float32
BF16 = jnp.bfloat16
HIGHEST = lax.Precision.HIGHEST

LN_EPS = 1e-5
CHUNK = 64
HEAD = 128
CONV_W = 4
N_BUCKETS = 32
MAX_DIST = 128
NEG = -1e30
LANES = 128
CONV_PAD = 8
VMEM_LIMIT = 56 * 1024 * 1024

TM_PROJ = 512
TQ = 256


def _cparams(sem):
    return pltpu.CompilerParams(dimension_semantics=sem, vmem_limit_bytes=VMEM_LIMIT)


def _sigmoid(x):
    return 1.0 / (1.0 + jnp.exp(-x))


def _silu(x):
    return x * _sigmoid(x)


def _layer_norm(x, g, b):
    mu = jnp.mean(x, axis=-1, keepdims=True)
    xc = x - mu
    var = jnp.mean(xc * xc, axis=-1, keepdims=True)
    return xc * lax.rsqrt(var + LN_EPS) * g + b


def _dot(a, b):
    return jnp.dot(a, b, preferred_element_type=F32)


def _hdot(a, b):
    return jnp.dot(a, b, preferred_element_type=F32, precision=HIGHEST)


def _dot_nt(a, b):
    return lax.dot_general(a, b, (((1,), (1,)), ((), ())), preferred_element_type=F32)


def _dot_tn(a, b):
    return lax.dot_general(a, b, (((0,), (0,)), ((), ())), preferred_element_type=F32)


def _ln_kernel(x_ref, g_ref, b_ref, o_ref):
    o_ref[...] = _layer_norm(x_ref[...], g_ref[...], b_ref[...])


def _ln_call(x, g, b, tm):
    n, d = x.shape
    return pl.pallas_call(
        _ln_kernel,
        out_shape=jax.ShapeDtypeStruct((n, d), F32),
        grid=(n // tm,),
        in_specs=[pl.BlockSpec((tm, d), lambda i: (i, 0)),
                  pl.BlockSpec((1, d), lambda i: (0, 0)),
                  pl.BlockSpec((1, d), lambda i: (0, 0))],
        out_specs=pl.BlockSpec((tm, d), lambda i: (i, 0)),
        compiler_params=_cparams(("parallel",)),
        name="ln_in",
    )(x, g.reshape(1, d), b.reshape(1, d))


def _ada_kernel(c_ref, w_ref, b_ref, o_ref):
    c = c_ref[...]
    o_ref[...] = _dot(_silu(c).astype(BF16), w_ref[...]) + b_ref[...]


def _ada_call(c_all, w_ada, b_ada):
    depth, d, d6 = w_ada.shape
    nb = c_all.shape[0]
    return pl.pallas_call(
        _ada_kernel,
        out_shape=jax.ShapeDtypeStruct((depth, nb, d6), F32),
        grid=(depth, d6 // d),
        in_specs=[pl.BlockSpec((nb, d), lambda l, j: (0, 0)),
                  pl.BlockSpec((None, d, d), lambda l, j: (l, 0, j)),
                  pl.BlockSpec((None, 1, d), lambda l, j: (l, 0, j))],
        out_specs=pl.BlockSpec((None, nb, d), lambda l, j: (l, 0, j)),
        compiler_params=_cparams(("parallel", "parallel")),
        name="ada",
    )(c_all, w_ada, b_ada.reshape(depth, 1, d6))


def _mod_spec(mod, d, tiles_per_group, chunk):
    rows = mod.shape[1]
    return pl.BlockSpec((None, rows, d), lambda i, *_: (i // tiles_per_group, 0, chunk))


N_MAIN_TILES = 6
N_W_TILES = 9


def _inproj_kernel(x_ref, sc_ref, sh_ref, w_ref, wba_ref,
                   main_ref, q_ref, k_ref, v_ref, ba_ref, h_scr):
    j = pl.program_id(1)

    @pl.when(j == 0)
    def _():
        h_scr[...] = (x_ref[...] * (1.0 + sc_ref[...]) + sh_ref[...]).astype(BF16)

    @pl.when(j < N_MAIN_TILES)
    def _():
        main_ref[...] = _dot(h_scr[...], w_ref[...])

    @pl.when(j == N_MAIN_TILES)
    def _():
        q_ref[...] = _dot(h_scr[...], w_ref[...]).astype(BF16)

    @pl.when(j == N_MAIN_TILES + 1)
    def _():
        k_ref[...] = _dot(h_scr[...], w_ref[...])

    @pl.when(j == N_MAIN_TILES + 2)
    def _():
        v_ref[...] = _dot(h_scr[...], w_ref[...])

    @pl.when(j == N_W_TILES)
    def _():
        ba_ref[...] = _dot(h_scr[...], wba_ref[...])


def _inproj_call(x, mod, w_all, w_ba, kbuf, vbuf, layer, tm, tiles_per_group):
    n, d = x.shape
    depth = w_all.shape[0]
    nba = w_ba.shape[-1]
    grid = (n // tm, N_W_TILES + 1)
    in_specs = [
        pl.BlockSpec((tm, d), lambda i, j: (i, 0)),
        _mod_spec(mod, d, tiles_per_group, 1),
        _mod_spec(mod, d, tiles_per_group, 0),
        pl.BlockSpec((None, d, d), lambda i, j: (layer, 0, jnp.minimum(j, N_W_TILES - 1))),
        pl.BlockSpec((None, d, nba), lambda i, j: (layer, 0, 0)),
    ]
    out_shape = [
        jax.ShapeDtypeStruct((n, N_MAIN_TILES * d), F32),
        jax.ShapeDtypeStruct((n, d), BF16),
        jax.ShapeDtypeStruct((depth, n, d), F32),
        jax.ShapeDtypeStruct((depth, n, d), F32),
        jax.ShapeDtypeStruct((n, nba), F32),
    ]
    out_specs = [
        pl.BlockSpec((tm, d), lambda i, j: (i, jnp.minimum(j, N_MAIN_TILES - 1))),
        pl.BlockSpec((tm, d), lambda i, j: (i, 0)),
        pl.BlockSpec((None, tm, d), lambda i, j: (layer, i, 0)),
        pl.BlockSpec((None, tm, d), lambda i, j: (layer, i, 0)),
        pl.BlockSpec((tm, nba), lambda i, j: (i, 0)),
    ]
    args = [x, mod, mod, w_all, w_ba]
    aliases = {}
    if kbuf is not None:
        in_specs += [pl.BlockSpec(memory_space=pl.ANY), pl.BlockSpec(memory_space=pl.ANY)]
        args += [kbuf, vbuf]
        aliases = {5: 2, 6: 3}

    def body(*refs):
        if kbuf is not None:
            refs = refs[:5] + refs[7:]
        _inproj_kernel(*refs)

    return pl.pallas_call(
        body,
        out_shape=out_shape,
        grid=grid,
        in_specs=in_specs,
        out_specs=out_specs,
        scratch_shapes=[pltpu.VMEM((tm, d), BF16)],
        input_output_aliases=aliases,
        compiler_params=_cparams(("parallel", "arbitrary")),
        name="inproj",
    )(*args)


def _tri_inverse(a_list, eye, blk, size):
    s = 8
    d = [jnp.where(blk[s], a, 0.0) for a in a_list]
    d2 = [_hdot(x, x) for x in d]
    p = [_hdot(eye - x, eye + y) for x, y in zip(d, d2)]
    d4 = [_hdot(x, x) for x in d2]
    t = [_hdot(x, eye + y) for x, y in zip(p, d4)]
    while s < size:
        lower = jnp.logical_and(blk[2 * s], jnp.logical_not(blk[s]))
        off = [jnp.where(lower, a, 0.0) for a in a_list]
        x = [_hdot(ti, oi) for ti, oi in zip(t, off)]
        t = [ti - _hdot(xi, ti) for ti, xi in zip(t, x)]
        s *= 2
    return t


def _delta_kernel(qu_ref, ku_ref, vu_ref, z_ref, ga_ref, ba_ref,
                  cwq_ref, cwk_ref, cwv_ref, par_ref, cbq_ref, cbk_ref, cbv_ref, s0_ref,
                  ma_ref, sn_ref, s_scr, cat_scr, *, chunk, cps, hb):
    c = pl.program_id(2)
    rows_blk = chunk * cps
    tail = CONV_W - 1

    @pl.when(c == 0)
    def _():
        s_scr[...] = s0_ref[...]
        for s, cb in enumerate((cbq_ref, cbk_ref, cbv_ref)):
            cat_scr[s, CONV_PAD - tail:CONV_PAD, :] = cb[...]

    convs = []
    for s, (u_ref, cw_ref) in enumerate(((qu_ref, cwq_ref), (ku_ref, cwk_ref), (vu_ref, cwv_ref))):
        cat_scr[s, CONV_PAD:CONV_PAD + rows_blk, :] = u_ref[...]
        acc = None
        for i in range(CONV_W):
            term = cat_scr[s, pl.ds(CONV_PAD - tail + i, rows_blk), :] * cw_ref[i:i + 1, :]
            acc = term if acc is None else acc + term
        convs.append(_silu(acc))
        cat_scr[s, CONV_PAD - tail:CONV_PAD, :] = u_ref[rows_blk - tail:rows_blk, :]

    ba = ba_ref[...]
    dt_row = par_ref[0:1, :]
    alog_row = par_ref[1:2, :]
    norm_row = par_ref[2:3, 0:HEAD]
    beta_all = _sigmoid(ba)
    xa = ba + dt_row
    softplus = jnp.maximum(xa, 0.0) + jnp.log(1.0 + jnp.exp(-jnp.abs(xa)))
    g_all = -jnp.exp(alog_row) * softplus

    r = lax.broadcasted_iota(jnp.int32, (chunk, chunk), 0)
    cc = lax.broadcasted_iota(jnp.int32, (chunk, chunk), 1)
    causal = r >= cc
    strict = r > cc
    eye = jnp.where(r == cc, 1.0, 0.0).astype(F32)
    tril = jnp.where(causal, 1.0, 0.0).astype(F32)
    blk = {}
    s = 8
    while s <= chunk:
        blk[s] = jnp.bitwise_xor(r, cc) < s
        s *= 2

    heads = range(hb)
    pairs = [(ci, hh) for ci in range(cps) for hh in heads]
    rsl = [slice(ci * chunk, (ci + 1) * chunk) for ci in range(cps)]
    csl = [slice(hh * HEAD, (hh + 1) * HEAD) for hh in heads]
    gcum_all = [_hdot(tril, g_all[rs, :]) for rs in rsl]
    gcum_t = [g.T for g in gcum_all]

    q, k, v, beta, gcum, g_last, decay = [], [], [], [], [], [], []
    for ci, hh in pairs:
        rs, cs = rsl[ci], csl[hh]
        qq = convs[0][rs, cs]
        kk = convs[1][rs, cs]
        q.append(qq * (lax.rsqrt(jnp.sum(qq * qq, axis=-1, keepdims=True) + 1e-6) * (HEAD ** -0.5)))
        k.append(kk * lax.rsqrt(jnp.sum(kk * kk, axis=-1, keepdims=True) + 1e-6))
        v.append(convs[2][rs, cs])
        beta.append(beta_all[rs, hh:hh + 1])
        gc = gcum_all[ci][:, hb + hh:hb + hh + 1]
        gr = gcum_t[ci][hb + hh:hb + hh + 1, :]
        gcum.append(gc)
        g_last.append(gc[chunk - 1:chunk, :])
        decay.append(jnp.where(causal, jnp.exp(jnp.where(causal, gc - gr, 0.0)), 0.0))
    idx = range(len(pairs))
    e_g = [jnp.exp(gcum[i]) for i in idx]
    kb = [k[i] * beta[i] for i in idx]
    kk_mat = [_dot_nt(kb[i], k[i]) for i in idx]
    qk = [_dot_nt(q[i], k[i]) * decay[i] for i in idx]
    a_mat = [jnp.where(strict, kk_mat[i] * decay[i], 0.0) for i in idx]
    t_inv = _tri_inverse(a_mat, eye, blk, chunk)
    uw = [_hdot(t_inv[i], jnp.concatenate([v[i] * beta[i], kb[i] * e_g[i]], axis=1)) for i in idx]
    q_g = [q[i] * e_g[i] for i in idx]
    k_tail = [k[i] * jnp.exp(g_last[i] - gcum[i]) for i in idx]

    for ci in range(cps):
        ids = [ci * hb + hh for hh in heads]
        state = [s_scr[hh] for hh in heads]
        ws = [_dot(uw[i][:, HEAD:2 * HEAD], state[hh]) for hh, i in zip(heads, ids)]
        u = [uw[i][:, 0:HEAD] - ws[hh] for hh, i in zip(heads, ids)]
        qs = [_dot(q_g[i], state[hh]) for hh, i in zip(heads, ids)]
        ku = [_dot_tn(k_tail[i], u[hh]) for hh, i in zip(heads, ids)]
        qu = [_dot(qk[i], u[hh]) for hh, i in zip(heads, ids)]
        for hh, i in zip(heads, ids):
            s_scr[hh] = jnp.exp(g_last[i]) * state[hh] + ku[hh]
        for hh in heads:
            rs, cs = rsl[ci], csl[hh]
            o = qs[hh] + qu[hh]
            o = o * lax.rsqrt(jnp.mean(o * o, axis=-1, keepdims=True) + LN_EPS) * norm_row
            o = o * _silu(z_ref[rs, cs])
            ma_ref[rs, cs] = _sigmoid(ga_ref[rs, cs]) * o

    @pl.when(c == pl.num_programs(2) - 1)
    def _():
        sn_ref[...] = s_scr[...]


def _delta_call(main, ba, conv_w, par, conv_buf, s0, layer, nb, t, chunk, cps, hb):
    n = main.shape[0]
    d = main.shape[1] // N_MAIN_TILES
    h = d // HEAD
    ng = h // hb
    wb = hb * HEAD
    rows_blk = chunk * cps
    ncb = t // rows_blk
    grid = (nb, ng, ncb)

    def seg(s):
        return pl.BlockSpec((rows_blk, wb), lambda b, g, c: (b * ncb + c, s * ng + g))

    def cw(s):
        return pl.BlockSpec((None, CONV_W, wb), lambda b, g, c: (layer, 0, s * ng + g))

    def cb(s):
        return pl.BlockSpec((None, CONV_W - 1, wb), lambda b, g, c: (b, 0, s * ng + g))

    in_specs = [seg(0), seg(1), seg(2), seg(3), seg(4),
                pl.BlockSpec((rows_blk, LANES), lambda b, g, c: (b * ncb + c, g)),
                cw(0), cw(1), cw(2),
                pl.BlockSpec((None, 8, LANES), lambda b, g, c: (layer, 0, g)),
                cb(0), cb(1), cb(2),
                pl.BlockSpec((None, hb, HEAD, HEAD), lambda b, g, c: (b, g, 0, 0))]
    out_shape = [jax.ShapeDtypeStruct((n, d), F32),
                 jax.ShapeDtypeStruct((nb, h, HEAD, HEAD), F32)]
    out_specs = [pl.BlockSpec((rows_blk, wb), lambda b, g, c: (b * ncb + c, g)),
                 pl.BlockSpec((None, hb, HEAD, HEAD), lambda b, g, c: (b, g, 0, 0))]
    return pl.pallas_call(
        functools.partial(_delta_kernel, chunk=chunk, cps=cps, hb=hb),
        out_shape=out_shape,
        grid=grid,
        in_specs=in_specs,
        out_specs=out_specs,
        scratch_shapes=[pltpu.VMEM((hb, HEAD, HEAD), F32),
                        pltpu.VMEM((3, CONV_PAD + rows_blk, wb), F32)],
        compiler_params=_cparams(("parallel", "parallel", "arbitrary")),
        name="delta",
    )(main, main, main, main, main, ba, conv_w, conv_w, conv_w, par,
      conv_buf, conv_buf, conv_buf, s0)


def _bucket_thresholds():
    nb = N_BUCKETS // 2
    max_exact = nb // 2
    ratio = MAX_DIST // max_exact
    steps = nb - max_exact
    out = []
    for kk in range(1, steps):
        n = max_exact
        while n ** steps < (max_exact ** steps) * (ratio ** kk):
            n += 1
        out.append(n)
    return out


def _far_distance():
    return _bucket_thresholds()[-1]


def _rel_bias_tile(tab_ref, head, qpos, kpos):
    nb = N_BUCKETS // 2
    max_exact = nb // 2
    rel = kpos - qpos
    n = jnp.abs(rel)
    large = jnp.full(rel.shape, max_exact, jnp.int32)
    for thr in _bucket_thresholds():
        large = large + jnp.where(n >= thr, 1, 0)
    bucket = jnp.where(rel > 0, nb, 0) + jnp.where(n < max_exact, n, large)
    far = tab_ref[nb - 1, head]
    bias = jnp.zeros(rel.shape, F32)
    for b in range(N_BUCKETS):
        bias = jnp.where(bucket == b, tab_ref[b, head] - far, bias)
    shift = CHUNK.bit_length() - 1
    mask = lax.shift_right_logical(kpos, shift) <= lax.shift_right_logical(qpos, shift)
    return jnp.where(mask, bias, NEG)


def _bias_kernel(tab_ref, pt_ref, st_ref, *, tq, past, ts):
    head = pl.program_id(0)
    qi = lax.broadcasted_iota(jnp.int32, (tq, tq), 0)
    ki = lax.broadcasted_iota(jnp.int32, (tq, tq), 1)
    pt_ref[0] = _rel_bias_tile(tab_ref, head, qi + tq, ki + tq)
    pt_ref[1] = _rel_bias_tile(tab_ref, head, qi + tq, ki)
    wk = past + LANES
    qs = lax.broadcasted_iota(jnp.int32, (ts, wk), 0) + past
    ks = lax.broadcasted_iota(jnp.int32, (ts, wk), 1)
    st = _rel_bias_tile(tab_ref, head, qs, ks)
    st_ref[...] = jnp.where(ks < past + ts, st, NEG)


def _bias_call(rel_bias, tq, past, ts):
    h = rel_bias.shape[1]
    wk = past + LANES
    return pl.pallas_call(
        functools.partial(_bias_kernel, tq=tq, past=past, ts=ts),
        out_shape=[jax.ShapeDtypeStruct((h, 2, tq, tq), F32),
                   jax.ShapeDtypeStruct((h, ts, wk), F32)],
        grid=(h,),
        in_specs=[pl.BlockSpec(memory_space=pltpu.SMEM)],
        out_specs=[pl.BlockSpec((None, 2, tq, tq), lambda i: (i, 0, 0, 0)),
                   pl.BlockSpec((None, ts, wk), lambda i: (i, 0, 0))],
        compiler_params=_cparams(("arbitrary",)),
        name="rel_bias",
    )(rel_bias)


def _lam_value(lam_ref, lam_init):
    lp = lam_ref[...]
    s1 = jnp.sum(lp[0:1, :] * lp[1:2, :], axis=-1, keepdims=True)
    s2 = jnp.sum(lp[2:3, :] * lp[3:4, :], axis=-1, keepdims=True)
    return jnp.exp(s1) - jnp.exp(s2) + lam_init


def _stack_maps(q):
    lane = lax.broadcasted_iota(jnp.int32, q.shape, 1)
    zero = jnp.zeros_like(q)
    half = HEAD // 2
    return jnp.concatenate([jnp.where(lane < half, q, zero), jnp.where(lane >= half, q, zero)], axis=0)


def _finish_attn(acc, l, lam_val, lam_init, subln, ma, gb, tq):
    o = acc[0:tq] / l[0:tq] - lam_val * (acc[tq:2 * tq] / l[tq:2 * tq])
    o = o * lax.rsqrt(jnp.mean(o * o, axis=-1, keepdims=True) + LN_EPS) * subln * (1.0 - lam_init)
    return (ma + _sigmoid(gb) * o).astype(BF16)


def _attn_kernel(q_ref, k_ref, v_ref, bias_ref, lam_ref, subln_ref, ma_ref, gb_ref,
                 out_ref, kbf, vbf, m_scr, l_scr, acc_scr, *, tq, lam_init):
    i = pl.program_id(2)

    @pl.when(i == 0)
    def _():
        kbf[...] = k_ref[...].astype(BF16)
        vbf[...] = v_ref[...].astype(BF16)

    q2 = _stack_maps(q_ref[...])
    m_scr[...] = jnp.full(m_scr.shape, NEG, F32)
    l_scr[...] = jnp.zeros(l_scr.shape, F32)
    acc_scr[...] = jnp.zeros(acc_scr.shape, F32)

    def block(j, bias):
        start = pl.multiple_of(j * tq, tq)
        kj = kbf[pl.ds(start, tq), :]
        vj = vbf[pl.ds(start, tq), :]
        s = _dot_nt(q2, kj)
        if bias is not None:
            s = s + jnp.concatenate([bias, bias], axis=0)
        m_old = m_scr[...]
        m_new = jnp.maximum(m_old, jnp.max(s, axis=-1, keepdims=True))
        alpha = jnp.exp(m_old - m_new)
        p = jnp.exp(s - m_new)
        l_scr[...] = alpha * l_scr[...] + jnp.sum(p, axis=-1, keepdims=True)
        acc_scr[...] = alpha * acc_scr[...] + _dot(p.astype(BF16), vj)
        m_scr[...] = m_new

    block(i, bias_ref[0])

    @pl.when(i > 0)
    def _():
        block(i - 1, bias_ref[1])

    def far(j, carry):
        block(j, None)
        return carry

    lax.fori_loop(0, jnp.maximum(i - 1, 0), far, 0)

    lam_val = _lam_value(lam_ref, lam_init)
    out_ref[...] = _finish_attn(acc_scr[...], l_scr[...], lam_val, lam_init, subln_ref[...],
                                ma_ref[...], gb_ref[...], tq)


def _attn_call(q, kbuf, vbuf, bias, lam, subln, ma, main, layer, nb, t, tq, lam_init):
    n, d = q.shape
    h = d // HEAD
    nq = t // tq
    assert tq + 1 >= _far_distance() and tq % CHUNK == 0
    gb_col = (N_MAIN_TILES - 1) * h
    row = lambda b, hh, i: (b * nq + i, hh)
    in_specs = [
        pl.BlockSpec((tq, HEAD), row),
        pl.BlockSpec((None, t, HEAD), lambda b, hh, i: (layer, b, hh)),
        pl.BlockSpec((None, t, HEAD), lambda b, hh, i: (layer, b, hh)),
        pl.BlockSpec((None, 2, tq, tq), lambda b, hh, i: (hh, 0, 0, 0)),
        pl.BlockSpec((None, 4, HEAD // 2), lambda b, hh, i: (layer, 0, 0)),
        pl.BlockSpec((None, 1, HEAD), lambda b, hh, i: (layer, 0, 0)),
        pl.BlockSpec((tq, HEAD), row),
        pl.BlockSpec((tq, HEAD), lambda b, hh, i: (b * nq + i, gb_col + hh)),
    ]
    return pl.pallas_call(
        functools.partial(_attn_kernel, tq=tq, lam_init=lam_init),
        out_shape=jax.ShapeDtypeStruct((n, d), BF16),
        grid=(nb, h, nq),
        in_specs=in_specs,
        out_specs=pl.BlockSpec((tq, HEAD), row),
        scratch_shapes=[pltpu.VMEM((t, HEAD), BF16), pltpu.VMEM((t, HEAD), BF16),
                        pltpu.VMEM((2 * tq, 1), F32), pltpu.VMEM((2 * tq, 1), F32),
                        pltpu.VMEM((2 * tq, HEAD), F32)],
        compiler_params=_cparams(("parallel", "parallel", "arbitrary")),
        name="attn",
    )(q, kbuf, vbuf, bias, lam, subln, ma, main)


def _attn_step_kernel(q_ref, kc_ref, vc_ref, kn_ref, vn_ref, bias_ref, lam_ref, subln_ref,
                      ma_ref, gb_ref, out_ref, *, ts, past, lam_init):
    q2 = _stack_maps(q_ref[...])
    pad = jnp.zeros((LANES - ts, HEAD), F32)
    kc = kc_ref[...].astype(BF16)
    vc = vc_ref[...].astype(BF16)
    kn = jnp.concatenate([kn_ref[...], pad], axis=0).astype(BF16)
    vn = jnp.concatenate([vn_ref[...], pad], axis=0).astype(BF16)
    bias = bias_ref[...]
    bias2 = jnp.concatenate([bias, bias], axis=0)
    s_c = _dot_nt(q2, kc) + bias2[:, 0:past]
    s_n = _dot_nt(q2, kn) + bias2[:, past:past + LANES]
    m = jnp.maximum(jnp.max(s_c, axis=-1, keepdims=True), jnp.max(s_n, axis=-1, keepdims=True))
    p_c = jnp.exp(s_c - m)
    p_n = jnp.exp(s_n - m)
    l = jnp.sum(p_c, axis=-1, keepdims=True) + jnp.sum(p_n, axis=-1, keepdims=True)
    acc = _dot(p_c.astype(BF16), vc) + _dot(p_n.astype(BF16), vn)
    lam_val = _lam_value(lam_ref, lam_init)
    out_ref[...] = _finish_attn(acc, l, lam_val, lam_init, subln_ref[...], ma_ref[...], gb_ref[...], ts)


def _attn_step_call(q, cache_k, cache_v, kbuf, vbuf, bias, lam, subln, ma, main, layer, nb, ts, lam_init):
    n, d = q.shape
    h = d // HEAD
    past = cache_k.shape[1] // nb
    gb_col = (N_MAIN_TILES - 1) * h
    row = lambda b, hh: (b, hh)
    lrow = lambda b, hh: (layer, b, hh)
    in_specs = [
        pl.BlockSpec((ts, HEAD), row),
        pl.BlockSpec((None, past, HEAD), lrow),
        pl.BlockSpec((None, past, HEAD), lrow),
        pl.BlockSpec((None, ts, HEAD), lrow),
        pl.BlockSpec((None, ts, HEAD), lrow),
        pl.BlockSpec((None, ts, past + LANES), lambda b, hh: (hh, 0, 0)),
        pl.BlockSpec((None, 4, HEAD // 2), lambda b, hh: (layer, 0, 0)),
        pl.BlockSpec((None, 1, HEAD), lambda b, hh: (layer, 0, 0)),
        pl.BlockSpec((ts, HEAD), row),
        pl.BlockSpec((ts, HEAD), lambda b, hh: (b, gb_col + hh)),
    ]
    return pl.pallas_call(
        functools.partial(_attn_step_kernel, ts=ts, past=past, lam_init=lam_init),
        out_shape=jax.ShapeDtypeStruct((n, d), BF16),
        grid=(nb, h),
        in_specs=in_specs,
        out_specs=pl.BlockSpec((ts, HEAD), row),
        compiler_params=_cparams(("parallel", "parallel")),
        name="attn_step",
    )(q, cache_k, cache_v, kbuf, vbuf, bias, lam, subln, ma, main)


def _wo_kernel(m_ref, x_ref, gt_ref, w_ref, g_ref, b_ref, o_ref, *, alpha):
    y = _dot(m_ref[...], w_ref[...])
    o_ref[...] = _layer_norm(alpha * x_ref[...] + gt_ref[...] * y, g_ref[...], b_ref[...])


def _wo_call(merged, x, mod, w_o, ln_g, ln_b, layer, tm, tiles_per_group, alpha):
    n, d = x.shape
    vec = pl.BlockSpec((None, 1, d), lambda i: (layer, 0, 0))
    return pl.pallas_call(
        functools.partial(_wo_kernel, alpha=alpha),
        out_shape=jax.ShapeDtypeStruct((n, d), F32),
        grid=(n // tm,),
        in_specs=[pl.BlockSpec((tm, d), lambda i: (i, 0)),
                  pl.BlockSpec((tm, d), lambda i: (i, 0)),
                  _mod_spec(mod, d, tiles_per_group, 2),
                  pl.BlockSpec((None, d, d), lambda i: (layer, 0, 0)),
                  vec, vec],
        out_specs=pl.BlockSpec((tm, d), lambda i: (i, 0)),
        compiler_params=_cparams(("parallel",)),
        name="wo",
    )(merged, x, mod, w_o, ln_g, ln_b)


def _ffn_kernel(x_ref, sc_ref, sh_ref, gt_ref, wu_ref, wv_ref, wo_ref, g_ref, b_ref,
                o_ref, h_scr, acc_scr, *, alpha):
    j = pl.program_id(1)

    @pl.when(j == 0)
    def _():
        h_scr[...] = (x_ref[...] * (1.0 + sc_ref[...]) + sh_ref[...]).astype(BF16)
        acc_scr[...] = jnp.zeros(acc_scr.shape, F32)

    h = h_scr[...]
    u = _dot(h, wu_ref[...])
    v = _dot(h, wv_ref[...])
    acc_scr[...] += _dot((_silu(u) * v).astype(BF16), wo_ref[...])

    @pl.when(j == pl.num_programs(1) - 1)
    def _():
        o_ref[...] = _layer_norm(alpha * x_ref[...] + gt_ref[...] * acc_scr[...], g_ref[...], b_ref[...])


def _ffn_call(x, mod, w_ff_in, w_ff_out, ln_g, ln_b, layer, tm, tiles_per_group, alpha, nf):
    n, d = x.shape
    dff = w_ff_out.shape[1]
    tf = dff // nf
    vec = pl.BlockSpec((None, 1, d), lambda i, j: (layer, 0, 0))
    return pl.pallas_call(
        functools.partial(_ffn_kernel, alpha=alpha),
        out_shape=jax.ShapeDtypeStruct((n, d), F32),
        grid=(n // tm, nf),
        in_specs=[pl.BlockSpec((tm, d), lambda i, j: (i, 0)),
                  _mod_spec(mod, d, tiles_per_group, 4),
                  _mod_spec(mod, d, tiles_per_group, 3),
                  _mod_spec(mod, d, tiles_per_group, 5),
                  pl.BlockSpec((None, d, tf), lambda i, j: (layer, 0, j)),
                  pl.BlockSpec((None, d, tf), lambda i, j: (layer, 0, nf + j)),
                  pl.BlockSpec((None, tf, d), lambda i, j: (layer, j, 0)),
                  vec, vec],
        out_specs=pl.BlockSpec((tm, d), lambda i, j: (i, 0)),
        scratch_shapes=[pltpu.VMEM((tm, d), BF16), pltpu.VMEM((tm, d), F32)],
        compiler_params=_cparams(("parallel", "arbitrary")),
        name="ffn",
    )(x, mod, mod, mod, w_ff_in, w_ff_in, w_ff_out, ln_g, ln_b)


def _pick_tile(n, pref):
    t = min(n, pref)
    while n % t:
        t //= 2
    return t


def kernel(x_prompt, x_sample, cache_k, cache_v, state_conv, state_delta, c_prompt, c_sample,
           ln_in_g, ln_in_b, rel_bias, w_ada, b_ada, w_in, conv_w, a_log, dt_bias, norm_a,
           lam, subln_g, w_o, ln1_g, ln1_b, w_ff_in, w_ff_out, ln2_g, ln2_b):
    bp, tp, d = x_prompt.shape
    bs, ts, _ = x_sample.shape
    depth = w_in.shape[0]
    h = d // HEAD
    past = cache_k.shape[2]
    dff = w_ff_out.shape[1]
    n_p = bp * tp
    n_s = bs * ts
    alpha = (2 * depth) ** 0.25
    hb = h
    ng = h // hb

    o1 = 4 * d + 2 * h
    qkvz = w_in[:, :, 0:4 * d]
    gates = w_in[:, :, o1 + 3 * d:o1 + 5 * d]
    qb = w_in[:, :, o1:o1 + d] * (HEAD // 2) ** -0.5
    kvb = w_in[:, :, o1 + d:o1 + 3 * d]
    w_all = jnp.concatenate([qkvz, gates, qb, kvb], axis=-1).astype(BF16)
    wb = w_in[:, :, 4 * d:4 * d + h].reshape(depth, d, ng, hb)
    wa = w_in[:, :, 4 * d + h:4 * d + 2 * h].reshape(depth, d, ng, hb)
    w_ba = jnp.concatenate([wb, wa, jnp.zeros((depth, d, ng, LANES - 2 * hb), F32)], axis=-1)
    w_ba = w_ba.reshape(depth, d, ng * LANES).astype(BF16)
    w_ada_b = w_ada.astype(BF16)
    w_o_b = w_o.astype(BF16)
    w_ff_in_b = w_ff_in.astype(BF16)
    w_ff_out_b = w_ff_out.astype(BF16)

    def lane_rows(v):
        v = v.reshape(depth, ng, hb)
        z = jnp.zeros((depth, ng, hb), F32)
        zz = jnp.zeros((depth, ng, LANES - 2 * hb), F32)
        return jnp.concatenate([z, v, zz], axis=-1).reshape(depth, ng * LANES)

    par = jnp.zeros((depth, 8, ng * LANES), F32)
    par = par.at[:, 0].set(lane_rows(dt_bias)).at[:, 1].set(lane_rows(a_log))
    par = par.at[:, 2].set(jnp.tile(norm_a, (1, ng)))
    subln = subln_g.reshape(depth, 1, HEAD)
    ln1g, ln1b = ln1_g.reshape(depth, 1, d), ln1_b.reshape(depth, 1, d)
    ln2g, ln2b = ln2_g.reshape(depth, 1, d), ln2_b.reshape(depth, 1, d)

    mod = _ada_call(jnp.concatenate([c_prompt, c_sample], axis=0), w_ada_b, b_ada)
    mod_p = mod[:, :bp].reshape(depth, bp, 1, 6 * d)
    mod_s = jnp.repeat(mod[:, bp:], ts, axis=1).reshape(depth, 1, n_s, 6 * d)

    bias_p, bias_s = _bias_call(rel_bias, TQ, past, ts)

    tm_p = _pick_tile(tp, TM_PROJ)
    tpg_p = tp // tm_p
    xp = _ln_call(x_prompt.reshape(n_p, d), ln_in_g, ln_in_b, tm_p)
    xs = _ln_call(x_sample.reshape(n_s, d), ln_in_g, ln_in_b, n_s)

    conv0 = jnp.zeros((bp, CONV_W - 1, 3 * d), F32)
    s0 = jnp.zeros((bp, h, HEAD, HEAD), F32)
    cache_k2 = cache_k.reshape(depth, bs * past, d)
    cache_v2 = cache_v.reshape(depth, bs * past, d)
    nf = 2 if dff % (2 * LANES) == 0 else 1

    kp = vp = ks = vs = None
    conv_p, conv_s, st_p, st_s = [], [], [], []
    for l in range(depth):
        lam_init = 0.8 - 0.6 * math.exp(-0.3 * l)
        main, qb_p, kp, vp, ba = _inproj_call(xp, mod_p[l], w_all, w_ba, kp, vp, l, tm_p, tpg_p)
        ma, sn = _delta_call(main, ba, conv_w, par, conv0, s0, l, bp, tp, CHUNK, 1, hb)
        merged = _attn_call(qb_p, kp, vp, bias_p, lam, subln, ma, main, l, bp, tp, TQ, lam_init)
        x1 = _wo_call(merged, xp, mod_p[l], w_o_b, ln1g, ln1b, l, tm_p, tpg_p, alpha)
        xp = _ffn_call(x1, mod_p[l], w_ff_in_b, w_ff_out_b, ln2g, ln2b, l, tm_p, tpg_p, alpha, nf)
        conv_p.append(main.reshape(bp, tp, -1)[:, tp - (CONV_W - 1):, 0:3 * d])
        st_p.append(sn)
        main, qb_s, ks, vs, ba = _inproj_call(xs, mod_s[l], w_all, w_ba, ks, vs, l, n_s, 1)
        ma, sn = _delta_call(main, ba, conv_w, par, state_conv[l], state_delta[l], l, bs, ts, ts, 1, hb)
        merged = _attn_step_call(qb_s, cache_k2, cache_v2, ks, vs, bias_s, lam, subln, ma, main,
                                 l, bs, ts, lam_init)
        x1 = _wo_call(merged, xs, mod_s[l], w_o_b, ln1g, ln1b, l, n_s, 1, alpha)
        xs = _ffn_call(x1, mod_s[l], w_ff_in_b, w_ff_out_b, ln2g, ln2b, l, n_s, 1, alpha, nf)
        conv_s.append(main.reshape(bs, ts, -1)[:, ts - (CONV_W - 1):, 0:3 * d])
        st_s.append(sn)

    return (xp.reshape(bp, tp, d), xs.reshape(bs, ts, d),
            kp.reshape(depth, bp, tp, h, HEAD), vp.reshape(depth, bp, tp, h, HEAD),
            jnp.stack(conv_p), jnp.stack(st_p),
            ks.reshape(depth, bs, ts, h, HEAD), vs.reshape(depth, bs, ts, h, HEAD),
            jnp.stack(conv_s), jnp.stack(st_s))
```

```python
import functools
import math

import jax
import jax.numpy as jnp
from jax import lax
from jax.experimental import pallas as pl
from jax.experimental.pallas import tpu as pltpu

F32 = jnp.float32
BF16 = jnp.bfloat16
HIGHEST = lax.Precision.HIGHEST

LN_EPS = 1e-5
CHUNK = 64
HEAD = 128
CONV_W = 4
N_BUCKETS = 32
MAX_DIST = 128
NEG = -1e30
LOG2E = math.log2(math.e)
LANES = 128
CONV_PAD = 8
VMEM_LIMIT = 56 * 1024 * 1024

TM_PROJ = 512
TQ = 256


def _cparams(sem):
    return pltpu.CompilerParams(dimension_semantics=sem, vmem_limit_bytes=VMEM_LIMIT)


def _sigmoid(x):
    return 1.0 / (1.0 + jnp.exp(-x))


def _silu(x):
    return x * _sigmoid(x)


def _layer_norm(x, g, b):
    mu = jnp.mean(x, axis=-1, keepdims=True)
    xc = x - mu
    var = jnp.mean(xc * xc, axis=-1, keepdims=True)
    return xc * lax.rsqrt(var + LN_EPS) * g + b


def _dot(a, b):
    return jnp.dot(a, b, preferred_element_type=F32)


def _hdot(a, b):
    return jnp.dot(a, b, preferred_element_type=F32, precision=HIGHEST)


def _dot_nt(a, b):
    return lax.dot_general(a, b, (((1,), (1,)), ((), ())), preferred_element_type=F32)


def _dot_tn(a, b):
    return lax.dot_general(a, b, (((0,), (0,)), ((), ())), preferred_element_type=F32)


def _ln_kernel(x_ref, g_ref, b_ref, o_ref):
    o_ref[...] = _layer_norm(x_ref[...], g_ref[...], b_ref[...])


def _ln_call(x, g, b, tm):
    n, d = x.shape
    return pl.pallas_call(
        _ln_kernel,
        out_shape=jax.ShapeDtypeStruct((n, d), F32),
        grid=(n // tm,),
        in_specs=[pl.BlockSpec((tm, d), lambda i: (i, 0)),
                  pl.BlockSpec((1, d), lambda i: (0, 0)),
                  pl.BlockSpec((1, d), lambda i: (0, 0))],
        out_specs=pl.BlockSpec((tm, d), lambda i: (i, 0)),
        compiler_params=_cparams(("parallel",)),
        name="ln_in",
    )(x, g.reshape(1, d), b.reshape(1, d))


def _ada_kernel(c_ref, w_ref, b_ref, o_ref):
    c = c_ref[...]
    o_ref[...] = _dot(_silu(c).astype(BF16), w_ref[...]) + b_ref[...]


def _ada_call(c_all, w_ada, b_ada):
    depth, d, d6 = w_ada.shape
    nb = c_all.shape[0]
    return pl.pallas_call(
        _ada_kernel,
        out_shape=jax.ShapeDtypeStruct((depth, nb, d6), F32),
        grid=(depth, d6 // d),
        in_specs=[pl.BlockSpec((nb, d), lambda l, j: (0, 0)),
                  pl.BlockSpec((None, d, d), lambda l, j: (l, 0, j)),
                  pl.BlockSpec((None, 1, d), lambda l, j: (l, 0, j))],
        out_specs=pl.BlockSpec((None, nb, d), lambda l, j: (l, 0, j)),
        compiler_params=_cparams(("parallel", "parallel")),
        name="ada",
    )(c_all, w_ada, b_ada.reshape(depth, 1, d6))


def _mod_spec(mod, d, tiles_per_group, chunk):
    rows = mod.shape[1]
    return pl.BlockSpec((None, rows, d), lambda i, *_: (i // tiles_per_group, 0, chunk))


N_MAIN_TILES = 6
N_W_TILES = 9


def _inproj_kernel(x_ref, sc_ref, sh_ref, w_ref, wba_ref,
                   main_ref, q_ref, k_ref, v_ref, ba_ref, h_scr):
    j = pl.program_id(1)

    @pl.when(j == 0)
    def _():
        h_scr[...] = (x_ref[...] * (1.0 + sc_ref[...]) + sh_ref[...]).astype(BF16)

    @pl.when(j < N_MAIN_TILES)
    def _():
        main_ref[...] = _dot(h_scr[...], w_ref[...])

    @pl.when(j == N_MAIN_TILES)
    def _():
        q_ref[...] = _dot(h_scr[...], w_ref[...]).astype(BF16)

    @pl.when(j == N_MAIN_TILES + 1)
    def _():
        k_ref[...] = _dot(h_scr[...], w_ref[...])

    @pl.when(j == N_MAIN_TILES + 2)
    def _():
        v_ref[...] = _dot(h_scr[...], w_ref[...])

    @pl.when(j == N_W_TILES)
    def _():
        ba_ref[...] = _dot(h_scr[...], wba_ref[...])


def _inproj_call(x, mod, w_all, w_ba, layer, tm, tiles_per_group):
    n, d = x.shape
    nba = w_ba.shape[-1]
    grid = (n // tm, N_W_TILES + 1)
    in_specs = [
        pl.BlockSpec((tm, d), lambda i, j: (i, 0)),
        _mod_spec(mod, d, tiles_per_group, 1),
        _mod_spec(mod, d, tiles_per_group, 0),
        pl.BlockSpec((None, d, d), lambda i, j: (layer, 0, jnp.minimum(j, N_W_TILES - 1))),
        pl.BlockSpec((None, d, nba), lambda i, j: (layer, 0, 0)),
    ]
    out_shape = [
        jax.ShapeDtypeStruct((n, N_MAIN_TILES * d), F32),
        jax.ShapeDtypeStruct((n, d), BF16),
        jax.ShapeDtypeStruct((n, d), F32),
        jax.ShapeDtypeStruct((n, d), F32),
        jax.ShapeDtypeStruct((n, nba), F32),
    ]
    out_specs = [
        pl.BlockSpec((tm, d), lambda i, j: (i, jnp.minimum(j, N_MAIN_TILES - 1))),
        pl.BlockSpec((tm, d), lambda i, j: (i, 0)),
        pl.BlockSpec((tm, d), lambda i, j: (i, 0)),
        pl.BlockSpec((tm, d), lambda i, j: (i, 0)),
        pl.BlockSpec((tm, nba), lambda i, j: (i, 0)),
    ]
    return pl.pallas_call(
        _inproj_kernel,
        out_shape=out_shape,
        grid=grid,
        in_specs=in_specs,
        out_specs=out_specs,
        scratch_shapes=[pltpu.VMEM((tm, d), BF16)],
        compiler_params=_cparams(("parallel", "arbitrary")),
        name="inproj",
    )(x, mod, mod, w_all, w_ba)


def _tri_inverse(a_list, eye, blk, size):
    s = 8
    d = [jnp.where(blk[s], a, 0.0) for a in a_list]
    d2 = [_hdot(x, x) for x in d]
    p = [_hdot(eye - x, eye + y) for x, y in zip(d, d2)]
    d4 = [_hdot(x, x) for x in d2]
    t = [_hdot(x, eye + y) for x, y in zip(p, d4)]
    while s < size:
        lower = jnp.logical_and(blk[2 * s], jnp.logical_not(blk[s]))
        off = [jnp.where(lower, a, 0.0) for a in a_list]
        x = [_hdot(ti, oi) for ti, oi in zip(t, off)]
        t = [ti - _hdot(xi, ti) for ti, xi in zip(t, x)]
        s *= 2
    return t


def _delta_kernel(qu_ref, ku_ref, vu_ref, z_ref, ga_ref, ba_ref,
                  cwq_ref, cwk_ref, cwv_ref, par_ref, cbq_ref, cbk_ref, cbv_ref, s0_ref,
                  ma_ref, sn_ref, s_scr, cat_scr, *, chunk, cps, hb):
    c = pl.program_id(2)
    rows_blk = chunk * cps
    tail = CONV_W - 1

    @pl.when(c == 0)
    def _():
        s_scr[...] = s0_ref[...]
        for s, cb in enumerate((cbq_ref, cbk_ref, cbv_ref)):
            cat_scr[s, CONV_PAD - tail:CONV_PAD, :] = cb[...]

    convs = []
    for s, (u_ref, cw_ref) in enumerate(((qu_ref, cwq_ref), (ku_ref, cwk_ref), (vu_ref, cwv_ref))):
        cat_scr[s, CONV_PAD:CONV_PAD + rows_blk, :] = u_ref[...]
        acc = None
        for i in range(CONV_W):
            term = cat_scr[s, pl.ds(CONV_PAD - tail + i, rows_blk), :] * cw_ref[i:i + 1, :]
            acc = term if acc is None else acc + term
        convs.append(_silu(acc))
        cat_scr[s, CONV_PAD - tail:CONV_PAD, :] = u_ref[rows_blk - tail:rows_blk, :]

    ba = ba_ref[...]
    dt_row = par_ref[0:1, :]
    alog_row = par_ref[1:2, :]
    norm_row = par_ref[2:3, 0:HEAD]
    beta_all = _sigmoid(ba)
    xa = ba + dt_row
    softplus = jnp.maximum(xa, 0.0) + jnp.log(1.0 + jnp.exp(-jnp.abs(xa)))
    g_all = -jnp.exp(alog_row) * softplus

    r = lax.broadcasted_iota(jnp.int32, (chunk, chunk), 0)
    cc = lax.broadcasted_iota(jnp.int32, (chunk, chunk), 1)
    causal = r >= cc
    strict = r > cc
    eye = jnp.where(r == cc, 1.0, 0.0).astype(F32)
    tril = jnp.where(causal, 1.0, 0.0).astype(F32)
    blk = {}
    s = 8
    while s <= chunk:
        blk[s] = jnp.bitwise_xor(r, cc) < s
        s *= 2

    heads = range(hb)
    pairs = [(ci, hh) for ci in range(cps) for hh in heads]
    rsl = [slice(ci * chunk, (ci + 1) * chunk) for ci in range(cps)]
    csl = [slice(hh * HEAD, (hh + 1) * HEAD) for hh in heads]
    gcum_all = [_hdot(tril, g_all[rs, :]) for rs in rsl]
    gcum_t = [g.T for g in gcum_all]

    q, k, v, beta, gcum, g_last, decay = [], [], [], [], [], [], []
    for ci, hh in pairs:
        rs, cs = rsl[ci], csl[hh]
        qq = convs[0][rs, cs]
        kk = convs[1][rs, cs]
        q.append(qq * (lax.rsqrt(jnp.sum(qq * qq, axis=-1, keepdims=True) + 1e-6) * (HEAD ** -0.5)))
        k.append(kk * lax.rsqrt(jnp.sum(kk * kk, axis=-1, keepdims=True) + 1e-6))
        v.append(convs[2][rs, cs])
        beta.append(beta_all[rs, hh:hh + 1])
        gc = gcum_all[ci][:, hb + hh:hb + hh + 1]
        gr = gcum_t[ci][hb + hh:hb + hh + 1, :]
        gcum.append(gc)
        g_last.append(gc[chunk - 1:chunk, :])
        decay.append(jnp.where(causal, jnp.exp(jnp.where(causal, gc - gr, 0.0)), 0.0))
    idx = range(len(pairs))
    e_g = [jnp.exp(gcum[i]) for i in idx]
    kb = [k[i] * beta[i] for i in idx]
    kk_mat = [_dot_nt(kb[i], k[i]) for i in idx]
    qk = [_dot_nt(q[i], k[i]) * decay[i] for i in idx]
    a_mat = [jnp.where(strict, kk_mat[i] * decay[i], 0.0) for i in idx]
    t_inv = _tri_inverse(a_mat, eye, blk, chunk)
    uw = [_hdot(t_inv[i], jnp.concatenate([v[i] * beta[i], kb[i] * e_g[i]], axis=1)) for i in idx]
    q_g = [q[i] * e_g[i] for i in idx]
    k_tail = [k[i] * jnp.exp(g_last[i] - gcum[i]) for i in idx]

    for ci in range(cps):
        ids = [ci * hb + hh for hh in heads]
        state = [s_scr[hh] for hh in heads]
        ws = [_dot(uw[i][:, HEAD:2 * HEAD], state[hh]) for hh, i in zip(heads, ids)]
        u = [uw[i][:, 0:HEAD] - ws[hh] for hh, i in zip(heads, ids)]
        qs = [_dot(q_g[i], state[hh]) for hh, i in zip(heads, ids)]
        ku = [_dot_tn(k_tail[i], u[hh]) for hh, i in zip(heads, ids)]
        qu = [_dot(qk[i], u[hh]) for hh, i in zip(heads, ids)]
        for hh, i in zip(heads, ids):
            s_scr[hh] = jnp.exp(g_last[i]) * state[hh] + ku[hh]
        for hh in heads:
            rs, cs = rsl[ci], csl[hh]
            o = qs[hh] + qu[hh]
            o = o * lax.rsqrt(jnp.mean(o * o, axis=-1, keepdims=True) + LN_EPS) * norm_row
            o = o * _silu(z_ref[rs, cs])
            ma_ref[rs, cs] = _sigmoid(ga_ref[rs, cs]) * o

    @pl.when(c == pl.num_programs(2) - 1)
    def _():
        sn_ref[...] = s_scr[...]


def _delta_call(main, ba, conv_w, par, conv_buf, s0, layer, nb, t, chunk, cps, hb):
    n = main.shape[0]
    d = main.shape[1] // N_MAIN_TILES
    h = d // HEAD
    ng = h // hb
    wb = hb * HEAD
    rows_blk = chunk * cps
    ncb = t // rows_blk
    grid = (nb, ng, ncb)

    def seg(s):
        return pl.BlockSpec((rows_blk, wb), lambda b, g, c: (b * ncb + c, s * ng + g))

    def cw(s):
        return pl.BlockSpec((None, CONV_W, wb), lambda b, g, c: (layer, 0, s * ng + g))

    def cb(s):
        return pl.BlockSpec((None, CONV_W - 1, wb), lambda b, g, c: (b, 0, s * ng + g))

    in_specs = [seg(0), seg(1), seg(2), seg(3), seg(4),
                pl.BlockSpec((rows_blk, LANES), lambda b, g, c: (b * ncb + c, g)),
                cw(0), cw(1), cw(2),
                pl.BlockSpec((None, 8, LANES), lambda b, g, c: (layer, 0, g)),
                cb(0), cb(1), cb(2),
                pl.BlockSpec((None, hb, HEAD, HEAD), lambda b, g, c: (b, g, 0, 0))]
    out_shape = [jax.ShapeDtypeStruct((n, d), F32),
                 jax.ShapeDtypeStruct((nb, h, HEAD, HEAD), F32)]
    out_specs = [pl.BlockSpec((rows_blk, wb), lambda b, g, c: (b * ncb + c, g)),
                 pl.BlockSpec((None, hb, HEAD, HEAD), lambda b, g, c: (b, g, 0, 0))]
    return pl.pallas_call(
        functools.partial(_delta_kernel, chunk=chunk, cps=cps, hb=hb),
        out_shape=out_shape,
        grid=grid,
        in_specs=in_specs,
        out_specs=out_specs,
        scratch_shapes=[pltpu.VMEM((hb, HEAD, HEAD), F32),
                        pltpu.VMEM((3, CONV_PAD + rows_blk, wb), F32)],
        compiler_params=_cparams(("parallel", "parallel", "arbitrary")),
        name="delta",
    )(main, main, main, main, main, ba, conv_w, conv_w, conv_w, par,
      conv_buf, conv_buf, conv_buf, s0)


def _bucket_thresholds():
    nb = N_BUCKETS // 2
    max_exact = nb // 2
    ratio = MAX_DIST // max_exact
    steps = nb - max_exact
    out = []
    for kk in range(1, steps):
        n = max_exact
        while n ** steps < (max_exact ** steps) * (ratio ** kk):
            n += 1
        out.append(n)
    return out


def _far_distance():
    return _bucket_thresholds()[-1]


def _rel_bias_tile(tab_ref, head, qpos, kpos):
    nb = N_BUCKETS // 2
    max_exact = nb // 2
    rel = kpos - qpos
    n = jnp.abs(rel)
    large = jnp.full(rel.shape, max_exact, jnp.int32)
    for thr in _bucket_thresholds():
        large = large + jnp.where(n >= thr, 1, 0)
    bucket = jnp.where(rel > 0, nb, 0) + jnp.where(n < max_exact, n, large)
    far = tab_ref[nb - 1, head]
    bias = jnp.zeros(rel.shape, F32)
    for b in range(N_BUCKETS):
        bias = jnp.where(bucket == b, (tab_ref[b, head] - far) * LOG2E, bias)
    shift = CHUNK.bit_length() - 1
    mask = lax.shift_right_logical(kpos, shift) <= lax.shift_right_logical(qpos, shift)
    return jnp.where(mask, bias, NEG)


def _bias_kernel(tab_ref, pt_ref, st_ref, *, tq, past, ts):
    head = pl.program_id(0)
    ki = lax.broadcasted_iota(jnp.int32, (tq, 2 * tq), 0)
    qi = lax.broadcasted_iota(jnp.int32, (tq, 2 * tq), 1)
    qi = jnp.where(qi >= tq, qi - tq, qi)
    pt_ref[0] = _rel_bias_tile(tab_ref, head, qi + tq, ki + tq)
    pt_ref[1] = _rel_bias_tile(tab_ref, head, qi + tq, ki)
    wk = past + LANES
    qs = lax.broadcasted_iota(jnp.int32, (ts, wk), 0) + past
    ks = lax.broadcasted_iota(jnp.int32, (ts, wk), 1)
    st = _rel_bias_tile(tab_ref, head, qs, ks)
    st_ref[...] = jnp.where(ks < past + ts, st, NEG)


def _bias_call(rel_bias, tq, past, ts):
    h = rel_bias.shape[1]
    wk = past + LANES
    return pl.pallas_call(
        functools.partial(_bias_kernel, tq=tq, past=past, ts=ts),
        out_shape=[jax.ShapeDtypeStruct((h, 2, tq, 2 * tq), F32),
                   jax.ShapeDtypeStruct((h, ts, wk), F32)],
        grid=(h,),
        in_specs=[pl.BlockSpec(memory_space=pltpu.SMEM)],
        out_specs=[pl.BlockSpec((None, 2, tq, 2 * tq), lambda i: (i, 0, 0, 0)),
                   pl.BlockSpec((None, ts, wk), lambda i: (i, 0, 0))],
        compiler_params=_cparams(("arbitrary",)),
        name="rel_bias",
    )(rel_bias)


def _lam_value(lam_ref, lam_init):
    lp = lam_ref[...]
    s1 = jnp.sum(lp[0:1, :] * lp[1:2, :], axis=-1, keepdims=True)
    s2 = jnp.sum(lp[2:3, :] * lp[3:4, :], axis=-1, keepdims=True)
    return jnp.exp(s1) - jnp.exp(s2) + lam_init


def _stack_maps(q):
    lane = lax.broadcasted_iota(jnp.int32, q.shape, 1)
    zero = jnp.zeros_like(q)
    half = HEAD // 2
    return jnp.concatenate([jnp.where(lane < half, q, zero), jnp.where(lane >= half, q, zero)], axis=0)


def _merge_out(o, lam_init, subln, ma, gb):
    o = o * lax.rsqrt(jnp.mean(o * o, axis=-1, keepdims=True) + LN_EPS) * subln * (1.0 - lam_init)
    return (ma + _sigmoid(gb) * o).astype(BF16)


def _attn_block(n_blocks, q2, kbf, vt, bias_ref, tq):
    kv = n_blocks * tq
    pieces = []
    if n_blocks > 2:
        pieces.append((0, kv - 2 * tq, None))
    if n_blocks >= 2:
        pieces.append((kv - 2 * tq, tq, 1))
    pieces.append((kv - tq, tq, 0))
    scores = []
    for start, size, tile in pieces:
        s = _dot_nt(kbf[start:start + size, :], q2)
        if tile is not None:
            s = s + bias_ref[tile]
        scores.append(s)
    m = None
    for s in scores:
        mx = jnp.max(s, axis=0, keepdims=True)
        m = mx if m is None else jnp.maximum(m, mx)
    l = None
    acc = None
    for (start, size, _), s in zip(pieces, scores):
        p = jnp.exp2(s - m)
        ls = jnp.sum(p, axis=0, keepdims=True)
        pv = _dot(vt[:, start:start + size], p.astype(BF16))
        l = ls if l is None else l + ls
        acc = pv if acc is None else acc + pv
    return acc, l


def _attn_kernel(q_ref, k_ref, v_ref, bias_ref, lam_ref, subln_ref, ma_ref, gb_ref,
                 out_ref, kbf, vt, *, tq, nq, lam_init):
    i = pl.program_id(2)

    @pl.when(i == 0)
    def _():
        kbf[...] = k_ref[...].astype(BF16)
        vt[...] = v_ref[...].T.astype(BF16)

    q2 = _stack_maps(q_ref[...])
    lam_val = _lam_value(lam_ref, lam_init)

    for n in range(1, nq + 1):
        @pl.when(i == n - 1)
        def _(n=n):
            acc, l = _attn_block(n, q2, kbf, vt, bias_ref, tq)
            o_t = acc[:, 0:tq] / l[:, 0:tq] - lam_val * (acc[:, tq:2 * tq] / l[:, tq:2 * tq])
            out_ref[...] = _merge_out(o_t.T, lam_init, subln_ref[...], ma_ref[...], gb_ref[...])


def _attn_call(q, k, v, bias, lam, subln, ma, main, layer, nb, t, tq, lam_init):
    n, d = q.shape
    h = d // HEAD
    nq = t // tq
    assert tq + 1 >= _far_distance() and tq % CHUNK == 0
    gb_col = (N_MAIN_TILES - 1) * h
    row = lambda b, hh, i: (b * nq + i, hh)
    in_specs = [
        pl.BlockSpec((tq, HEAD), row),
        pl.BlockSpec((t, HEAD), lambda b, hh, i: (b, hh)),
        pl.BlockSpec((t, HEAD), lambda b, hh, i: (b, hh)),
        pl.BlockSpec((None, 2, tq, 2 * tq), lambda b, hh, i: (hh, 0, 0, 0)),
        pl.BlockSpec((None, 4, HEAD // 2), lambda b, hh, i: (layer, 0, 0)),
        pl.BlockSpec((None, 1, HEAD), lambda b, hh, i: (layer, 0, 0)),
        pl.BlockSpec((tq, HEAD), row),
        pl.BlockSpec((tq, HEAD), lambda b, hh, i: (b * nq + i, gb_col + hh)),
    ]
    return pl.pallas_call(
        functools.partial(_attn_kernel, tq=tq, nq=nq, lam_init=lam_init),
        out_shape=jax.ShapeDtypeStruct((n, d), BF16),
        grid=(nb, h, nq),
        in_specs=in_specs,
        out_specs=pl.BlockSpec((tq, HEAD), row),
        scratch_shapes=[pltpu.VMEM((t, HEAD), BF16), pltpu.VMEM((HEAD, t), BF16)],
        compiler_params=_cparams(("parallel", "parallel", "arbitrary")),
        name="attn",
    )(q, k, v, bias, lam, subln, ma, main)


def _attn_step_kernel(q_ref, kc_ref, vc_ref, kn_ref, vn_ref, bias_ref, lam_ref, subln_ref,
                      ma_ref, gb_ref, out_ref, *, ts, past, lam_init):
    q2 = _stack_maps(q_ref[...])
    pad = jnp.zeros((LANES - ts, HEAD), F32)
    kc = kc_ref[...].astype(BF16)
    vc = vc_ref[...].astype(BF16)
    kn = jnp.concatenate([kn_ref[...], pad], axis=0).astype(BF16)
    vn = jnp.concatenate([vn_ref[...], pad], axis=0).astype(BF16)
    bias = bias_ref[...]
    bias2 = jnp.concatenate([bias, bias], axis=0)
    s_c = _dot_nt(q2, kc) + bias2[:, 0:past]
    s_n = _dot_nt(q2, kn) + bias2[:, past:past + LANES]
    m = jnp.maximum(jnp.max(s_c, axis=-1, keepdims=True), jnp.max(s_n, axis=-1, keepdims=True))
    p_c = jnp.exp2(s_c - m)
    p_n = jnp.exp2(s_n - m)
    l = jnp.sum(p_c, axis=-1, keepdims=True) + jnp.sum(p_n, axis=-1, keepdims=True)
    acc = _dot(p_c.astype(BF16), vc) + _dot(p_n.astype(BF16), vn)
    lam_val = _lam_value(lam_ref, lam_init)
    o = acc[0:ts] / l[0:ts] - lam_val * (acc[ts:2 * ts] / l[ts:2 * ts])
    out_ref[...] = _merge_out(o, lam_init, subln_ref[...], ma_ref[...], gb_ref[...])


def _attn_step_call(q, cache_k, cache_v, k_new, v_new, bias, lam, subln, ma, main, layer, nb, ts, lam_init):
    n, d = q.shape
    h = d // HEAD
    past = cache_k.shape[1] // nb
    gb_col = (N_MAIN_TILES - 1) * h
    row = lambda b, hh: (b, hh)
    lrow = lambda b, hh: (layer, b, hh)
    in_specs = [
        pl.BlockSpec((ts, HEAD), row),
        pl.BlockSpec((None, past, HEAD), lrow),
        pl.BlockSpec((None, past, HEAD), lrow),
        pl.BlockSpec((ts, HEAD), row),
        pl.BlockSpec((ts, HEAD), row),
        pl.BlockSpec((None, ts, past + LANES), lambda b, hh: (hh, 0, 0)),
        pl.BlockSpec((None, 4, HEAD // 2), lambda b, hh: (layer, 0, 0)),
        pl.BlockSpec((None, 1, HEAD), lambda b, hh: (layer, 0, 0)),
        pl.BlockSpec((ts, HEAD), row),
        pl.BlockSpec((ts, HEAD), lambda b, hh: (b, gb_col + hh)),
    ]
    return pl.pallas_call(
        functools.partial(_attn_step_kernel, ts=ts, past=past, lam_init=lam_init),
        out_shape=jax.ShapeDtypeStruct((n, d), BF16),
        grid=(nb, h),
        in_specs=in_specs,
        out_specs=pl.BlockSpec((ts, HEAD), row),
        compiler_params=_cparams(("parallel", "parallel")),
        name="attn_step",
    )(q, cache_k, cache_v, k_new, v_new, bias, lam, subln, ma, main)


def _wo_kernel(m_ref, x_ref, gt_ref, w_ref, g_ref, b_ref, o_ref, *, alpha):
    y = _dot(m_ref[...], w_ref[...])
    o_ref[...] = _layer_norm(alpha * x_ref[...] + gt_ref[...] * y, g_ref[...], b_ref[...])


def _wo_call(merged, x, mod, w_o, ln_g, ln_b, layer, tm, tiles_per_group, alpha):
    n, d = x.shape
    vec = pl.BlockSpec((None, 1, d), lambda i: (layer, 0, 0))
    return pl.pallas_call(
        functools.partial(_wo_kernel, alpha=alpha),
        out_shape=jax.ShapeDtypeStruct((n, d), F32),
        grid=(n // tm,),
        in_specs=[pl.BlockSpec((tm, d), lambda i: (i, 0)),
                  pl.BlockSpec((tm, d), lambda i: (i, 0)),
                  _mod_spec(mod, d, tiles_per_group, 2),
                  pl.BlockSpec((None, d, d), lambda i: (layer, 0, 0)),
                  vec, vec],
        out_specs=pl.BlockSpec((tm, d), lambda i: (i, 0)),
        compiler_params=_cparams(("parallel",)),
        name="wo",
    )(merged, x, mod, w_o, ln_g, ln_b)


def _ffn_kernel(x_ref, sc_ref, sh_ref, gt_ref, wu_ref, wv_ref, wo_ref, g_ref, b_ref,
                o_ref, h_scr, acc_scr, *, alpha):
    j = pl.program_id(1)

    @pl.when(j == 0)
    def _():
        h_scr[...] = (x_ref[...] * (1.0 + sc_ref[...]) + sh_ref[...]).astype(BF16)
        acc_scr[...] = jnp.zeros(acc_scr.shape, F32)

    h = h_scr[...]
    u = _dot(h, wu_ref[...])
    v = _dot(h, wv_ref[...])
    acc_scr[...] += _dot((_silu(u) * v).astype(BF16), wo_ref[...])

    @pl.when(j == pl.num_programs(1) - 1)
    def _():
        o_ref[...] = _layer_norm(alpha * x_ref[...] + gt_ref[...] * acc_scr[...], g_ref[...], b_ref[...])


def _ffn_call(x, mod, w_ff_in, w_ff_out, ln_g, ln_b, layer, tm, tiles_per_group, alpha, nf):
    n, d = x.shape
    dff = w_ff_out.shape[1]
    tf = dff // nf
    vec = pl.BlockSpec((None, 1, d), lambda i, j: (layer, 0, 0))
    return pl.pallas_call(
        functools.partial(_ffn_kernel, alpha=alpha),
        out_shape=jax.ShapeDtypeStruct((n, d), F32),
        grid=(n // tm, nf),
        in_specs=[pl.BlockSpec((tm, d), lambda i, j: (i, 0)),
                  _mod_spec(mod, d, tiles_per_group, 4),
                  _mod_spec(mod, d, tiles_per_group, 3),
                  _mod_spec(mod, d, tiles_per_group, 5),
                  pl.BlockSpec((None, d, tf), lambda i, j: (layer, 0, j)),
                  pl.BlockSpec((None, d, tf), lambda i, j: (layer, 0, nf + j)),
                  pl.BlockSpec((None, tf, d), lambda i, j: (layer, j, 0)),
                  vec, vec],
        out_specs=pl.BlockSpec((tm, d), lambda i, j: (i, 0)),
        scratch_shapes=[pltpu.VMEM((tm, d), BF16), pltpu.VMEM((tm, d), F32)],
        compiler_params=_cparams(("parallel", "arbitrary")),
        name="ffn",
    )(x, mod, mod, mod, w_ff_in, w_ff_in, w_ff_out, ln_g, ln_b)


def _pick_tile(n, pref):
    t = min(n, pref)
    while n % t:
        t //= 2
    return t


def kernel(x_prompt, x_sample, cache_k, cache_v, state_conv, state_delta, c_prompt, c_sample,
           ln_in_g, ln_in_b, rel_bias, w_ada, b_ada, w_in, conv_w, a_log, dt_bias, norm_a,
           lam, subln_g, w_o, ln1_g, ln1_b, w_ff_in, w_ff_out, ln2_g, ln2_b):
    bp, tp, d = x_prompt.shape
    bs, ts, _ = x_sample.shape
    depth = w_in.shape[0]
    h = d // HEAD
    past = cache_k.shape[2]
    dff = w_ff_out.shape[1]
    n_p = bp * tp
    n_s = bs * ts
    alpha = (2 * depth) ** 0.25
    hb = h
    ng = h // hb

    o1 = 4 * d + 2 * h
    qkvz = w_in[:, :, 0:4 * d]
    gates = w_in[:, :, o1 + 3 * d:o1 + 5 * d]
    qb = w_in[:, :, o1:o1 + d] * ((HEAD // 2) ** -0.5 * LOG2E)
    kvb = w_in[:, :, o1 + d:o1 + 3 * d]
    w_all = jnp.concatenate([qkvz, gates, qb, kvb], axis=-1).astype(BF16)
    wb = w_in[:, :, 4 * d:4 * d + h].reshape(depth, d, ng, hb)
    wa = w_in[:, :, 4 * d + h:4 * d + 2 * h].reshape(depth, d, ng, hb)
    w_ba = jnp.concatenate([wb, wa, jnp.zeros((depth, d, ng, LANES - 2 * hb), F32)], axis=-1)
    w_ba = w_ba.reshape(depth, d, ng * LANES).astype(BF16)
    w_ada_b = w_ada.astype(BF16)
    w_o_b = w_o.astype(BF16)
    w_ff_in_b = w_ff_in.astype(BF16)
    w_ff_out_b = w_ff_out.astype(BF16)

    def lane_rows(v):
        v = v.reshape(depth, ng, hb)
        z = jnp.zeros((depth, ng, hb), F32)
        zz = jnp.zeros((depth, ng, LANES - 2 * hb), F32)
        return jnp.concatenate([z, v, zz], axis=-1).reshape(depth, ng * LANES)

    par = jnp.zeros((depth, 8, ng * LANES), F32)
    par = par.at[:, 0].set(lane_rows(dt_bias)).at[:, 1].set(lane_rows(a_log))
    par = par.at[:, 2].set(jnp.tile(norm_a, (1, ng)))
    subln = subln_g.reshape(depth, 1, HEAD)
    ln1g, ln1b = ln1_g.reshape(depth, 1, d), ln1_b.reshape(depth, 1, d)
    ln2g, ln2b = ln2_g.reshape(depth, 1, d), ln2_b.reshape(depth, 1, d)

    mod = _ada_call(jnp.concatenate([c_prompt, c_sample], axis=0), w_ada_b, b_ada)
    mod_p = mod[:, :bp].reshape(depth, bp, 1, 6 * d)
    mod_s = jnp.repeat(mod[:, bp:], ts, axis=1).reshape(depth, 1, n_s, 6 * d)

    bias_p, bias_s = _bias_call(rel_bias, TQ, past, ts)

    tm_p = _pick_tile(tp, TM_PROJ)
    tpg_p = tp // tm_p
    xp = _ln_call(x_prompt.reshape(n_p, d), ln_in_g, ln_in_b, tm_p)
    xs = _ln_call(x_sample.reshape(n_s, d), ln_in_g, ln_in_b, n_s)

    conv0 = jnp.zeros((bp, CONV_W - 1, 3 * d), F32)
    s0 = jnp.zeros((bp, h, HEAD, HEAD), F32)
    cache_k2 = cache_k.reshape(depth, bs * past, d)
    cache_v2 = cache_v.reshape(depth, bs * past, d)
    nf = 2 if dff % (2 * LANES) == 0 else 1

    kp, vp, ks, vs = [], [], [], []
    conv_p, conv_s, st_p, st_s = [], [], [], []
    for l in range(depth):
        lam_init = 0.8 - 0.6 * math.exp(-0.3 * l)
        main, qb_p, k_l, v_l, ba = _inproj_call(xp, mod_p[l], w_all, w_ba, l, tm_p, tpg_p)
        ma, sn = _delta_call(main, ba, conv_w, par, conv0, s0, l, bp, tp, CHUNK, 1, hb)
        merged = _attn_call(qb_p, k_l, v_l, bias_p, lam, subln, ma, main, l, bp, tp, TQ, lam_init)
        x1 = _wo_call(merged, xp, mod_p[l], w_o_b, ln1g, ln1b, l, tm_p, tpg_p, alpha)
        xp = _ffn_call(x1, mod_p[l], w_ff_in_b, w_ff_out_b, ln2g, ln2b, l, tm_p, tpg_p, alpha, nf)
        conv_p.append(main.reshape(bp, tp, -1)[:, tp - (CONV_W - 1):, 0:3 * d])
        st_p.append(sn)
        kp.append(k_l)
        vp.append(v_l)
        main, qb_s, k_l, v_l, ba = _inproj_call(xs, mod_s[l], w_all, w_ba, l, n_s, 1)
        ma, sn = _delta_call(main, ba, conv_w, par, state_conv[l], state_delta[l], l, bs, ts, ts, 1, hb)
        merged = _attn_step_call(qb_s, cache_k2, cache_v2, k_l, v_l, bias_s, lam, subln, ma, main,
                                 l, bs, ts, lam_init)
        x1 = _wo_call(merged, xs, mod_s[l], w_o_b, ln1g, ln1b, l, n_s, 1, alpha)
        xs = _ffn_call(x1, mod_s[l], w_ff_in_b, w_ff_out_b, ln2g, ln2b, l, n_s, 1, alpha, nf)
        conv_s.append(main.reshape(bs, ts, -1)[:, ts - (CONV_W - 1):, 0:3 * d])
        st_s.append(sn)
        ks.append(k_l)
        vs.append(v_l)

    def heads_out(xs_list, nb, t):
        return jnp.stack(xs_list).reshape(depth, nb, t, h, HEAD)

    return (xp.reshape(bp, tp, d), xs.reshape(bs, ts, d),
            heads_out(kp, bp, tp), heads_out(vp, bp, tp), jnp.stack(conv_p), jnp.stack(st_p),
            heads_out(ks, bs, ts), heads_out(vs, bs, ts), jnp.stack(conv_s), jnp.stack(st_s))
```

```python
import functools
import math

import jax
import jax.numpy as jnp
from jax import lax
from jax.experimental import pallas as pl
from jax.experimental.pallas import tpu as pltpu

F32 = jnp.float32
BF16 = jnp.bfloat16
HIGHEST = lax.Precision.HIGHEST

LN_EPS = 1e-5
CHUNK = 64
HEAD = 128
CONV_W = 4
N_BUCKETS = 32
MAX_DIST = 128
NEG = -1e30
LOG2E = math.log2(math.e)
LANES = 128
CONV_PAD = 8
VMEM_LIMIT = 56 * 1024 * 1024

TM_PROJ = 512
TQ = 256


def _cparams(sem):
    return pltpu.CompilerParams(dimension_semantics=sem, vmem_limit_bytes=VMEM_LIMIT)


def _sigmoid(x):
    return 1.0 / (1.0 + jnp.exp(-x))


def _silu(x):
    return x * _sigmoid(x)


def _layer_norm(x, g, b):
    mu = jnp.mean(x, axis=-1, keepdims=True)
    xc = x - mu
    var = jnp.mean(xc * xc, axis=-1, keepdims=True)
    return xc * lax.rsqrt(var + LN_EPS) * g + b


def _dot(a, b):
    return jnp.dot(a, b, preferred_element_type=F32)


def _hdot(a, b):
    return jnp.dot(a, b, preferred_element_type=F32, precision=HIGHEST)


def _bdot(a, b):
    return _dot(a.astype(BF16), b.astype(BF16))


def _dot_nt(a, b):
    return lax.dot_general(a, b, (((1,), (1,)), ((), ())), preferred_element_type=F32)


def _dot_tn(a, b):
    return lax.dot_general(a, b, (((0,), (0,)), ((), ())), preferred_element_type=F32)


def _ln_kernel(x_ref, g_ref, b_ref, o_ref):
    o_ref[...] = _layer_norm(x_ref[...], g_ref[...], b_ref[...])


def _ln_call(x, g, b, tm):
    n, d = x.shape
    return pl.pallas_call(
        _ln_kernel,
        out_shape=jax.ShapeDtypeStruct((n, d), F32),
        grid=(n // tm,),
        in_specs=[pl.BlockSpec((tm, d), lambda i: (i, 0)),
                  pl.BlockSpec((1, d), lambda i: (0, 0)),
                  pl.BlockSpec((1, d), lambda i: (0, 0))],
        out_specs=pl.BlockSpec((tm, d), lambda i: (i, 0)),
        compiler_params=_cparams(("parallel",)),
        name="ln_in",
    )(x, g.reshape(1, d), b.reshape(1, d))


def _ada_kernel(c_ref, w_ref, b_ref, o_ref):
    c = c_ref[...]
    o_ref[...] = _dot(_silu(c).astype(BF16), w_ref[...]) + b_ref[...]


def _ada_call(c_all, w_ada, b_ada):
    depth, d, d6 = w_ada.shape
    nb = c_all.shape[0]
    return pl.pallas_call(
        _ada_kernel,
        out_shape=jax.ShapeDtypeStruct((depth, nb, d6), F32),
        grid=(depth, d6 // d),
        in_specs=[pl.BlockSpec((nb, d), lambda l, j: (0, 0)),
                  pl.BlockSpec((None, d, d), lambda l, j: (l, 0, j)),
                  pl.BlockSpec((None, 1, d), lambda l, j: (l, 0, j))],
        out_specs=pl.BlockSpec((None, nb, d), lambda l, j: (l, 0, j)),
        compiler_params=_cparams(("parallel", "parallel")),
        name="ada",
    )(c_all, w_ada, b_ada.reshape(depth, 1, d6))


def _mod_spec(mod, d, tiles_per_group, chunk):
    rows = mod.shape[1]
    return pl.BlockSpec((None, rows, d), lambda i, *_: (i // tiles_per_group, 0, chunk))


N_MAIN_TILES = 6
N_W_TILES = 9
N_CONV_TILES = 3


def _inproj_kernel(x_ref, sc_ref, sh_ref, w_ref, wba_ref, main_ref, q_ref, k_ref, v_ref, ba_ref, tail_ref):
    tm, d = x_ref.shape
    h = (x_ref[...] * (1.0 + sc_ref[...]) + sh_ref[...]).astype(BF16)

    def cols(j):
        return _dot(h, w_ref[:, j * d:(j + 1) * d])

    for j in range(N_MAIN_TILES):
        res = cols(j)
        main_ref[:, j * d:(j + 1) * d] = res.astype(main_ref.dtype)
        if j < N_CONV_TILES:
            tail_ref[:, j * d:(j + 1) * d] = res[tm - CONV_PAD:tm, :]
    q_ref[...] = cols(N_MAIN_TILES).astype(BF16)
    k_ref[...] = cols(N_MAIN_TILES + 1)
    v_ref[...] = cols(N_MAIN_TILES + 2)
    ba_ref[...] = _dot(h, wba_ref[...])


def _resident(block_shape, index_map):
    return pl.BlockSpec(block_shape, index_map, pipeline_mode=pl.Buffered(1))


def _inproj_call(x, mod, w_all, w_ba, layer, tm, tiles_per_group, main_dtype):
    n, d = x.shape
    nba = w_ba.shape[-1]
    wide = N_W_TILES * d
    in_specs = [
        pl.BlockSpec((tm, d), lambda i: (i, 0)),
        _mod_spec(mod, d, tiles_per_group, 1),
        _mod_spec(mod, d, tiles_per_group, 0),
        _resident((None, d, wide), lambda i: (layer, 0, 0)),
        _resident((None, d, nba), lambda i: (layer, 0, 0)),
    ]
    out_shape = [
        jax.ShapeDtypeStruct((n, N_MAIN_TILES * d), main_dtype),
        jax.ShapeDtypeStruct((n, d), BF16),
        jax.ShapeDtypeStruct((n, d), F32),
        jax.ShapeDtypeStruct((n, d), F32),
        jax.ShapeDtypeStruct((n, nba), F32),
        jax.ShapeDtypeStruct((n // tm, CONV_PAD, N_CONV_TILES * d), F32),
    ]
    out_specs = [
        pl.BlockSpec((tm, N_MAIN_TILES * d), lambda i: (i, 0)),
        pl.BlockSpec((tm, d), lambda i: (i, 0)),
        pl.BlockSpec((tm, d), lambda i: (i, 0)),
        pl.BlockSpec((tm, d), lambda i: (i, 0)),
        pl.BlockSpec((tm, nba), lambda i: (i, 0)),
        pl.BlockSpec((None, CONV_PAD, N_CONV_TILES * d), lambda i: (i, 0, 0)),
    ]
    return pl.pallas_call(
        _inproj_kernel,
        out_shape=out_shape,
        grid=(n // tm,),
        in_specs=in_specs,
        out_specs=out_specs,
        compiler_params=_cparams(("parallel",)),
        name="inproj",
    )(x, mod, mod, w_all, w_ba)


def _tri_inverse(a_list, eye, blk, size):
    s = 8
    d = [jnp.where(blk[s], a, 0.0) for a in a_list]
    d2 = [_bdot(x, x) for x in d]
    p = [_bdot(eye - x, eye + y) for x, y in zip(d, d2)]
    d4 = [_bdot(x, x) for x in d2]
    t = [_bdot(x, eye + y) for x, y in zip(p, d4)]
    while s < size:
        lower = jnp.logical_and(blk[2 * s], jnp.logical_not(blk[s]))
        off = [jnp.where(lower, a, 0.0) for a in a_list]
        x = [_bdot(ti, oi) for ti, oi in zip(t, off)]
        t = [ti - _bdot(xi, ti) for ti, xi in zip(t, x)]
        s *= 2
    return t


def _delta_kernel(qu_ref, ku_ref, vu_ref, z_ref, ga_ref, ba_ref,
                  cwq_ref, cwk_ref, cwv_ref, par_ref, cbq_ref, cbk_ref, cbv_ref, s0_ref,
                  ma_ref, sn_ref, s_scr, cat_scr, *, chunk, cps, hb):
    c = pl.program_id(2)
    rows_blk = chunk * cps
    tail = CONV_W - 1

    @pl.when(c == 0)
    def _():
        s_scr[...] = s0_ref[...]
        for s, cb in enumerate((cbq_ref, cbk_ref, cbv_ref)):
            cat_scr[s, CONV_PAD - tail:CONV_PAD, :] = cb[...]

    convs = []
    for s, (u_ref, cw_ref) in enumerate(((qu_ref, cwq_ref), (ku_ref, cwk_ref), (vu_ref, cwv_ref))):
        cat_scr[s, CONV_PAD:CONV_PAD + rows_blk, :] = u_ref[...].astype(F32)
        acc = None
        for i in range(CONV_W):
            term = cat_scr[s, pl.ds(CONV_PAD - tail + i, rows_blk), :] * cw_ref[i:i + 1, :]
            acc = term if acc is None else acc + term
        convs.append(_silu(acc))
        cat_scr[s, CONV_PAD - tail:CONV_PAD, :] = cat_scr[s, CONV_PAD + rows_blk - tail:CONV_PAD + rows_blk, :]

    ba = ba_ref[...]
    dt_row = par_ref[0:1, :]
    alog_row = par_ref[1:2, :]
    norm_row = par_ref[2:3, 0:HEAD]
    beta_all = _sigmoid(ba)
    xa = ba + dt_row
    softplus = jnp.maximum(xa, 0.0) + jnp.log(1.0 + jnp.exp(-jnp.abs(xa)))
    g_all = -jnp.exp(alog_row) * softplus

    r = lax.broadcasted_iota(jnp.int32, (chunk, chunk), 0)
    cc = lax.broadcasted_iota(jnp.int32, (chunk, chunk), 1)
    causal = r >= cc
    strict = r > cc
    eye = jnp.where(r == cc, 1.0, 0.0).astype(F32)
    tril = jnp.where(causal, 1.0, 0.0).astype(F32)
    blk = {}
    s = 8
    while s <= chunk:
        blk[s] = jnp.bitwise_xor(r, cc) < s
        s *= 2

    heads = range(hb)
    pairs = [(ci, hh) for ci in range(cps) for hh in heads]
    rsl = [slice(ci * chunk, (ci + 1) * chunk) for ci in range(cps)]
    csl = [slice(hh * HEAD, (hh + 1) * HEAD) for hh in heads]
    gcum_all = [_hdot(tril, g_all[rs, :]) for rs in rsl]
    gcum_t = [g.T for g in gcum_all]

    q, k, v, beta, gcum, g_last, decay = [], [], [], [], [], [], []
    for ci, hh in pairs:
        rs, cs = rsl[ci], csl[hh]
        qq = convs[0][rs, cs]
        kk = convs[1][rs, cs]
        q.append(qq * (lax.rsqrt(jnp.sum(qq * qq, axis=-1, keepdims=True) + 1e-6) * (HEAD ** -0.5)))
        k.append(kk * lax.rsqrt(jnp.sum(kk * kk, axis=-1, keepdims=True) + 1e-6))
        v.append(convs[2][rs, cs])
        beta.append(beta_all[rs, hh:hh + 1])
        gc = gcum_all[ci][:, hb + hh:hb + hh + 1]
        gr = gcum_t[ci][hb + hh:hb + hh + 1, :]
        gcum.append(gc)
        g_last.append(gc[chunk - 1:chunk, :])
        decay.append(jnp.where(causal, jnp.exp(jnp.where(causal, gc - gr, 0.0)), 0.0))
    idx = range(len(pairs))
    e_g = [jnp.exp(gcum[i]) for i in idx]
    kb = [k[i] * beta[i] for i in idx]
    k_b16 = [k[i].astype(BF16) for i in idx]
    kk_mat = [_dot_nt(kb[i].astype(BF16), k_b16[i]) for i in idx]
    qk = [_dot_nt(q[i].astype(BF16), k_b16[i]) * decay[i] for i in idx]
    a_mat = [jnp.where(strict, kk_mat[i] * decay[i], 0.0) for i in idx]
    t_inv = _tri_inverse(a_mat, eye, blk, chunk)
    uw = [_bdot(t_inv[i], jnp.concatenate([v[i] * beta[i], kb[i] * e_g[i]], axis=1)) for i in idx]
    q_g = [(q[i] * e_g[i]).astype(BF16) for i in idx]
    k_tail = [(k[i] * jnp.exp(g_last[i] - gcum[i])).astype(BF16) for i in idx]
    qk = [x.astype(BF16) for x in qk]

    for ci in range(cps):
        ids = [ci * hb + hh for hh in heads]
        state = [s_scr[hh] for hh in heads]
        state_b = [x.astype(BF16) for x in state]
        ws = [_dot(uw[i][:, HEAD:2 * HEAD].astype(BF16), state_b[hh]) for hh, i in zip(heads, ids)]
        u = [uw[i][:, 0:HEAD] - ws[hh] for hh, i in zip(heads, ids)]
        u_b = [x.astype(BF16) for x in u]
        qs = [_dot(q_g[i], state_b[hh]) for hh, i in zip(heads, ids)]
        ku = [_dot_tn(k_tail[i], u_b[hh]) for hh, i in zip(heads, ids)]
        qu = [_dot(qk[i], u_b[hh]) for hh, i in zip(heads, ids)]
        for hh, i in zip(heads, ids):
            s_scr[hh] = jnp.exp(g_last[i]) * state[hh] + ku[hh]
        for hh in heads:
            rs, cs = rsl[ci], csl[hh]
            o = qs[hh] + qu[hh]
            o = o * lax.rsqrt(jnp.mean(o * o, axis=-1, keepdims=True) + LN_EPS) * norm_row
            o = o * _silu(z_ref[rs, cs].astype(F32))
            ma_ref[rs, cs] = _sigmoid(ga_ref[rs, cs].astype(F32)) * o

    @pl.when(c == pl.num_programs(2) - 1)
    def _():
        sn_ref[...] = s_scr[...]


def _delta_call(main, ba, conv_w, par, conv_buf, s0, layer, nb, t, chunk, cps, hb):
    n = main.shape[0]
    d = main.shape[1] // N_MAIN_TILES
    h = d // HEAD
    ng = h // hb
    wb = hb * HEAD
    rows_blk = chunk * cps
    ncb = t // rows_blk
    grid = (nb, ng, ncb)

    def seg(s):
        return pl.BlockSpec((rows_blk, wb), lambda b, g, c: (b * ncb + c, s * ng + g))

    def cw(s):
        return pl.BlockSpec((None, CONV_W, wb), lambda b, g, c: (layer, 0, s * ng + g))

    def cb(s):
        return pl.BlockSpec((None, CONV_W - 1, wb), lambda b, g, c: (b, 0, s * ng + g))

    in_specs = [seg(0), seg(1), seg(2), seg(3), seg(4),
                pl.BlockSpec((rows_blk, LANES), lambda b, g, c: (b * ncb + c, g)),
                cw(0), cw(1), cw(2),
                pl.BlockSpec((None, 8, LANES), lambda b, g, c: (layer, 0, g)),
                cb(0), cb(1), cb(2),
                pl.BlockSpec((None, hb, HEAD, HEAD), lambda b, g, c: (b, g, 0, 0))]
    out_shape = [jax.ShapeDtypeStruct((n, d), F32),
                 jax.ShapeDtypeStruct((nb, h, HEAD, HEAD), F32)]
    out_specs = [pl.BlockSpec((rows_blk, wb), lambda b, g, c: (b * ncb + c, g)),
                 pl.BlockSpec((None, hb, HEAD, HEAD), lambda b, g, c: (b, g, 0, 0))]
    return pl.pallas_call(
        functools.partial(_delta_kernel, chunk=chunk, cps=cps, hb=hb),
        out_shape=out_shape,
        grid=grid,
        in_specs=in_specs,
        out_specs=out_specs,
        scratch_shapes=[pltpu.VMEM((hb, HEAD, HEAD), F32),
                        pltpu.VMEM((3, CONV_PAD + rows_blk, wb), F32)],
        compiler_params=_cparams(("parallel", "parallel", "arbitrary")),
        name="delta",
    )(main, main, main, main, main, ba, conv_w, conv_w, conv_w, par,
      conv_buf, conv_buf, conv_buf, s0)


def _bucket_thresholds():
    nb = N_BUCKETS // 2
    max_exact = nb // 2
    ratio = MAX_DIST // max_exact
    steps = nb - max_exact
    out = []
    for kk in range(1, steps):
        n = max_exact
        while n ** steps < (max_exact ** steps) * (ratio ** kk):
            n += 1
        out.append(n)
    return out


def _far_distance():
    return _bucket_thresholds()[-1]


def _rel_bias_tile(tab_ref, head, qpos, kpos):
    nb = N_BUCKETS // 2
    max_exact = nb // 2
    rel = kpos - qpos
    n = jnp.abs(rel)
    large = jnp.full(rel.shape, max_exact, jnp.int32)
    for thr in _bucket_thresholds():
        large = large + jnp.where(n >= thr, 1, 0)
    bucket = jnp.where(rel > 0, nb, 0) + jnp.where(n < max_exact, n, large)
    far = tab_ref[nb - 1, head]
    bias = jnp.zeros(rel.shape, F32)
    for b in range(N_BUCKETS):
        bias = jnp.where(bucket == b, (tab_ref[b, head] - far) * LOG2E, bias)
    shift = CHUNK.bit_length() - 1
    mask = lax.shift_right_logical(kpos, shift) <= lax.shift_right_logical(qpos, shift)
    return jnp.where(mask, bias, NEG)


def _bias_kernel(tab_ref, pt_ref, st_ref, *, tq, past, ts):
    head = pl.program_id(0)
    ki = lax.broadcasted_iota(jnp.int32, (tq, 2 * tq), 0)
    qi = lax.broadcasted_iota(jnp.int32, (tq, 2 * tq), 1)
    qi = jnp.where(qi >= tq, qi - tq, qi)
    pt_ref[0] = _rel_bias_tile(tab_ref, head, qi + tq, ki + tq)
    pt_ref[1] = _rel_bias_tile(tab_ref, head, qi + tq, ki)
    wk = past + LANES
    qs = lax.broadcasted_iota(jnp.int32, (ts, wk), 0) + past
    ks = lax.broadcasted_iota(jnp.int32, (ts, wk), 1)
    st = _rel_bias_tile(tab_ref, head, qs, ks)
    st_ref[...] = jnp.where(ks < past + ts, st, NEG)


def _bias_call(rel_bias, tq, past, ts):
    h = rel_bias.shape[1]
    wk = past + LANES
    return pl.pallas_call(
        functools.partial(_bias_kernel, tq=tq, past=past, ts=ts),
        out_shape=[jax.ShapeDtypeStruct((h, 2, tq, 2 * tq), F32),
                   jax.ShapeDtypeStruct((h, ts, wk), F32)],
        grid=(h,),
        in_specs=[pl.BlockSpec(memory_space=pltpu.SMEM)],
        out_specs=[pl.BlockSpec((None, 2, tq, 2 * tq), lambda i: (i, 0, 0, 0)),
                   pl.BlockSpec((None, ts, wk), lambda i: (i, 0, 0))],
        compiler_params=_cparams(("arbitrary",)),
        name="rel_bias",
    )(rel_bias)


def _lam_value(lam_ref, lam_init):
    lp = lam_ref[...]
    s1 = jnp.sum(lp[0:1, :] * lp[1:2, :], axis=-1, keepdims=True)
    s2 = jnp.sum(lp[2:3, :] * lp[3:4, :], axis=-1, keepdims=True)
    return jnp.exp(s1) - jnp.exp(s2) + lam_init


def _stack_maps(q):
    lane = lax.broadcasted_iota(jnp.int32, q.shape, 1)
    zero = jnp.zeros_like(q)
    half = HEAD // 2
    return jnp.concatenate([jnp.where(lane < half, q, zero), jnp.where(lane >= half, q, zero)], axis=0)


def _merge_out(o, lam_init, subln, ma, gb):
    o = o * lax.rsqrt(jnp.mean(o * o, axis=-1, keepdims=True) + LN_EPS) * subln * (1.0 - lam_init)
    return (ma + _sigmoid(gb.astype(F32)) * o).astype(BF16)


def _attn_block(n_blocks, q2, kbf, vt, bias_ref, tq):
    kv = n_blocks * tq
    pieces = []
    if n_blocks > 2:
        pieces.append((0, kv - 2 * tq, None))
    if n_blocks >= 2:
        pieces.append((kv - 2 * tq, tq, 1))
    pieces.append((kv - tq, tq, 0))
    scores = []
    for start, size, tile in pieces:
        s = _dot_nt(kbf[start:start + size, :], q2)
        if tile is not None:
            s = s + bias_ref[tile]
        scores.append(s)
    m = None
    for s in scores:
        mx = jnp.max(s, axis=0, keepdims=True)
        m = mx if m is None else jnp.maximum(m, mx)
    l = None
    acc = None
    for (start, size, _), s in zip(pieces, scores):
        p = jnp.exp2(s - m)
        ls = jnp.sum(p, axis=0, keepdims=True)
        pv = _dot(vt[:, start:start + size], p.astype(BF16))
        l = ls if l is None else l + ls
        acc = pv if acc is None else acc + pv
    return acc, l


def _attn_kernel(q_ref, k_ref, v_ref, bias_ref, lam_ref, subln_ref, ma_ref, gb_ref,
                 out_ref, kbf, vt, *, tq, nq, lam_init):
    kbf[...] = k_ref[...].astype(BF16)
    vt[...] = v_ref[...].T.astype(BF16)
    lam_val = _lam_value(lam_ref, lam_init)
    subln = subln_ref[...]

    for n in range(1, nq + 1):
        rows = slice((n - 1) * tq, n * tq)
        q2 = _stack_maps(q_ref[rows, :])
        acc, l = _attn_block(n, q2, kbf, vt, bias_ref, tq)
        o_t = acc[:, 0:tq] / l[:, 0:tq] - lam_val * (acc[:, tq:2 * tq] / l[:, tq:2 * tq])
        out_ref[rows, :] = _merge_out(o_t.T, lam_init, subln, ma_ref[rows, :], gb_ref[rows, :])


def _attn_call(q, k, v, bias, lam, subln, ma, main, layer, nb, t, tq, lam_init):
    n, d = q.shape
    h = d // HEAD
    nq = t // tq
    assert tq + 1 >= _far_distance() and tq % CHUNK == 0
    gb_col = (N_MAIN_TILES - 1) * h
    col = pl.BlockSpec((t, HEAD), lambda b, hh: (b, hh))
    in_specs = [
        col, col, col,
        pl.BlockSpec((None, 2, tq, 2 * tq), lambda b, hh: (hh, 0, 0, 0)),
        pl.BlockSpec((None, 4, HEAD // 2), lambda b, hh: (layer, 0, 0)),
        pl.BlockSpec((None, 1, HEAD), lambda b, hh: (layer, 0, 0)),
        col,
        pl.BlockSpec((t, HEAD), lambda b, hh: (b, gb_col + hh)),
    ]
    return pl.pallas_call(
        functools.partial(_attn_kernel, tq=tq, nq=nq, lam_init=lam_init),
        out_shape=jax.ShapeDtypeStruct((n, d), BF16),
        grid=(nb, h),
        in_specs=in_specs,
        out_specs=col,
        scratch_shapes=[pltpu.VMEM((t, HEAD), BF16), pltpu.VMEM((HEAD, t), BF16)],
        compiler_params=_cparams(("parallel", "parallel")),
        name="attn",
    )(q, k, v, bias, lam, subln, ma, main)


def _attn_step_kernel(q_ref, kc_ref, vc_ref, kn_ref, vn_ref, bias_ref, lam_ref, subln_ref,
                      ma_ref, gb_ref, out_ref, *, ts, past, lam_init):
    q2 = _stack_maps(q_ref[...])
    pad = jnp.zeros((LANES - ts, HEAD), F32)
    kc = kc_ref[...].astype(BF16)
    vc = vc_ref[...].astype(BF16)
    kn = jnp.concatenate([kn_ref[...], pad], axis=0).astype(BF16)
    vn = jnp.concatenate([vn_ref[...], pad], axis=0).astype(BF16)
    bias = bias_ref[...]
    bias2 = jnp.concatenate([bias, bias], axis=0)
    s_c = _dot_nt(q2, kc) + bias2[:, 0:past]
    s_n = _dot_nt(q2, kn) + bias2[:, past:past + LANES]
    m = jnp.maximum(jnp.max(s_c, axis=-1, keepdims=True), jnp.max(s_n, axis=-1, keepdims=True))
    p_c = jnp.exp2(s_c - m)
    p_n = jnp.exp2(s_n - m)
    l = jnp.sum(p_c, axis=-1, keepdims=True) + jnp.sum(p_n, axis=-1, keepdims=True)
    acc = _dot(p_c.astype(BF16), vc) + _dot(p_n.astype(BF16), vn)
    lam_val = _lam_value(lam_ref, lam_init)
    o = acc[0:ts] / l[0:ts] - lam_val * (acc[ts:2 * ts] / l[ts:2 * ts])
    out_ref[...] = _merge_out(o, lam_init, subln_ref[...], ma_ref[...], gb_ref[...])


def _attn_step_call(q, cache_k, cache_v, k_new, v_new, bias, lam, subln, ma, main, layer, nb, ts, lam_init):
    n, d = q.shape
    h = d // HEAD
    past = cache_k.shape[1] // nb
    gb_col = (N_MAIN_TILES - 1) * h
    row = lambda b, hh: (b, hh)
    lrow = lambda b, hh: (layer, b, hh)
    in_specs = [
        pl.BlockSpec((ts, HEAD), row),
        pl.BlockSpec((None, past, HEAD), lrow),
        pl.BlockSpec((None, past, HEAD), lrow),
        pl.BlockSpec((ts, HEAD), row),
        pl.BlockSpec((ts, HEAD), row),
        pl.BlockSpec((None, ts, past + LANES), lambda b, hh: (hh, 0, 0)),
        pl.BlockSpec((None, 4, HEAD // 2), lambda b, hh: (layer, 0, 0)),
        pl.BlockSpec((None, 1, HEAD), lambda b, hh: (layer, 0, 0)),
        pl.BlockSpec((ts, HEAD), row),
        pl.BlockSpec((ts, HEAD), lambda b, hh: (b, gb_col + hh)),
    ]
    return pl.pallas_call(
        functools.partial(_attn_step_kernel, ts=ts, past=past, lam_init=lam_init),
        out_shape=jax.ShapeDtypeStruct((n, d), BF16),
        grid=(nb, h),
        in_specs=in_specs,
        out_specs=pl.BlockSpec((ts, HEAD), row),
        compiler_params=_cparams(("parallel", "parallel")),
        name="attn_step",
    )(q, cache_k, cache_v, k_new, v_new, bias, lam, subln, ma, main)


def _wo_kernel(m_ref, x_ref, gt_ref, w_ref, g_ref, b_ref, o_ref, *, alpha):
    y = _dot(m_ref[...], w_ref[...])
    o_ref[...] = _layer_norm(alpha * x_ref[...] + gt_ref[...] * y, g_ref[...], b_ref[...])


def _wo_call(merged, x, mod, w_o, ln_g, ln_b, layer, tm, tiles_per_group, alpha):
    n, d = x.shape
    vec = pl.BlockSpec((None, 1, d), lambda i: (layer, 0, 0))
    return pl.pallas_call(
        functools.partial(_wo_kernel, alpha=alpha),
        out_shape=jax.ShapeDtypeStruct((n, d), F32),
        grid=(n // tm,),
        in_specs=[pl.BlockSpec((tm, d), lambda i: (i, 0)),
                  pl.BlockSpec((tm, d), lambda i: (i, 0)),
                  _mod_spec(mod, d, tiles_per_group, 2),
                  pl.BlockSpec((None, d, d), lambda i: (layer, 0, 0)),
                  vec, vec],
        out_specs=pl.BlockSpec((tm, d), lambda i: (i, 0)),
        compiler_params=_cparams(("parallel",)),
        name="wo",
    )(merged, x, mod, w_o, ln_g, ln_b)


def _ffn_kernel(x_ref, sc_ref, sh_ref, gt_ref, wi_ref, wo_ref, g_ref, b_ref, o_ref, *, alpha, nf):
    x = x_ref[...]
    h = (x * (1.0 + sc_ref[...]) + sh_ref[...]).astype(BF16)
    dff = wo_ref.shape[0]
    tf = dff // nf
    y = None
    for j in range(nf):
        u = _dot(h, wi_ref[:, j * tf:(j + 1) * tf])
        v = _dot(h, wi_ref[:, dff + j * tf:dff + (j + 1) * tf])
        part = _dot((_silu(u) * v).astype(BF16), wo_ref[j * tf:(j + 1) * tf, :])
        y = part if y is None else y + part
    o_ref[...] = _layer_norm(alpha * x + gt_ref[...] * y, g_ref[...], b_ref[...])


def _ffn_call(x, mod, w_ff_in, w_ff_out, ln_g, ln_b, layer, tm, tiles_per_group, alpha, nf):
    n, d = x.shape
    dff = w_ff_out.shape[1]
    vec = pl.BlockSpec((None, 1, d), lambda i: (layer, 0, 0))
    return pl.pallas_call(
        functools.partial(_ffn_kernel, alpha=alpha, nf=nf),
        out_shape=jax.ShapeDtypeStruct((n, d), F32),
        grid=(n // tm,),
        in_specs=[pl.BlockSpec((tm, d), lambda i: (i, 0)),
                  _mod_spec(mod, d, tiles_per_group, 4),
                  _mod_spec(mod, d, tiles_per_group, 3),
                  _mod_spec(mod, d, tiles_per_group, 5),
                  _resident((None, d, 2 * dff), lambda i: (layer, 0, 0)),
                  _resident((None, dff, d), lambda i: (layer, 0, 0)),
                  vec, vec],
        out_specs=pl.BlockSpec((tm, d), lambda i: (i, 0)),
        compiler_params=_cparams(("parallel",)),
        name="ffn",
    )(x, mod, mod, mod, w_ff_in, w_ff_out, ln_g, ln_b)


def _pick_tile(n, pref):
    t = min(n, pref)
    while n % t:
        t //= 2
    return t


def kernel(x_prompt, x_sample, cache_k, cache_v, state_conv, state_delta, c_prompt, c_sample,
           ln_in_g, ln_in_b, rel_bias, w_ada, b_ada, w_in, conv_w, a_log, dt_bias, norm_a,
           lam, subln_g, w_o, ln1_g, ln1_b, w_ff_in, w_ff_out, ln2_g, ln2_b):
    bp, tp, d = x_prompt.shape
    bs, ts, _ = x_sample.shape
    depth = w_in.shape[0]
    h = d // HEAD
    past = cache_k.shape[2]
    dff = w_ff_out.shape[1]
    n_p = bp * tp
    n_s = bs * ts
    alpha = (2 * depth) ** 0.25
    hb = h
    ng = h // hb

    o1 = 4 * d + 2 * h
    qkvz = w_in[:, :, 0:4 * d]
    gates = w_in[:, :, o1 + 3 * d:o1 + 5 * d]
    qb = w_in[:, :, o1:o1 + d] * ((HEAD // 2) ** -0.5 * LOG2E)
    kvb = w_in[:, :, o1 + d:o1 + 3 * d]
    w_all = jnp.concatenate([qkvz, gates, qb, kvb], axis=-1).astype(BF16)
    wb = w_in[:, :, 4 * d:4 * d + h].reshape(depth, d, ng, hb)
    wa = w_in[:, :, 4 * d + h:4 * d + 2 * h].reshape(depth, d, ng, hb)
    w_ba = jnp.concatenate([wb, wa, jnp.zeros((depth, d, ng, LANES - 2 * hb), F32)], axis=-1)
    w_ba = w_ba.reshape(depth, d, ng * LANES).astype(BF16)
    w_ada_b = w_ada.astype(BF16)
    w_o_b = w_o.astype(BF16)
    w_ff_in_b = w_ff_in.astype(BF16)
    w_ff_out_b = w_ff_out.astype(BF16)

    def lane_rows(v):
        v = v.reshape(depth, ng, hb)
        z = jnp.zeros((depth, ng, hb), F32)
        zz = jnp.zeros((depth, ng, LANES - 2 * hb), F32)
        return jnp.concatenate([z, v, zz], axis=-1).reshape(depth, ng * LANES)

    par = jnp.zeros((depth, 8, ng * LANES), F32)
    par = par.at[:, 0].set(lane_rows(dt_bias)).at[:, 1].set(lane_rows(a_log))
    par = par.at[:, 2].set(jnp.tile(norm_a, (1, ng)))
    subln = subln_g.reshape(depth, 1, HEAD)
    ln1g, ln1b = ln1_g.reshape(depth, 1, d), ln1_b.reshape(depth, 1, d)
    ln2g, ln2b = ln2_g.reshape(depth, 1, d), ln2_b.reshape(depth, 1, d)

    mod = _ada_call(jnp.concatenate([c_prompt, c_sample], axis=0), w_ada_b, b_ada)
    mod_p = mod[:, :bp].reshape(depth, bp, 1, 6 * d)
    mod_s = jnp.repeat(mod[:, bp:], ts, axis=1).reshape(depth, 1, n_s, 6 * d)

    bias_p, bias_s = _bias_call(rel_bias, TQ, past, ts)

    tm_p = _pick_tile(tp, TM_PROJ)
    tpg_p = tp // tm_p
    xp = _ln_call(x_prompt.reshape(n_p, d), ln_in_g, ln_in_b, tm_p)
    xs = _ln_call(x_sample.reshape(n_s, d), ln_in_g, ln_in_b, n_s)

    conv0 = jnp.zeros((bp, CONV_W - 1, 3 * d), F32)
    s0 = jnp.zeros((bp, h, HEAD, HEAD), F32)
    cache_k2 = cache_k.reshape(depth, bs * past, d)
    cache_v2 = cache_v.reshape(depth, bs * past, d)
    nf = 2 if dff % (2 * LANES) == 0 else 1

    kp, vp, ks, vs = [], [], [], []
    conv_p, conv_s, st_p, st_s = [], [], [], []
    for l in range(depth):
        lam_init = 0.8 - 0.6 * math.exp(-0.3 * l)
        main, qb_p, k_l, v_l, ba, tail = _inproj_call(xp, mod_p[l], w_all, w_ba, l, tm_p, tpg_p, BF16)
        ma, sn = _delta_call(main, ba, conv_w, par, conv0, s0, l, bp, tp, CHUNK, 1, hb)
        merged = _attn_call(qb_p, k_l, v_l, bias_p, lam, subln, ma, main, l, bp, tp, TQ, lam_init)
        x1 = _wo_call(merged, xp, mod_p[l], w_o_b, ln1g, ln1b, l, tm_p, tpg_p, alpha)
        xp = _ffn_call(x1, mod_p[l], w_ff_in_b, w_ff_out_b, ln2g, ln2b, l, tm_p, tpg_p, alpha, nf)
        conv_p.append(tail.reshape(bp, tpg_p, CONV_PAD, 3 * d)[:, tpg_p - 1, CONV_PAD - (CONV_W - 1):, :])
        st_p.append(sn)
        kp.append(k_l)
        vp.append(v_l)
        main, qb_s, k_l, v_l, ba, _ = _inproj_call(xs, mod_s[l], w_all, w_ba, l, n_s, 1, F32)
        ma, sn = _delta_call(main, ba, conv_w, par, state_conv[l], state_delta[l], l, bs, ts, ts, 1, hb)
        merged = _attn_step_call(qb_s, cache_k2, cache_v2, k_l, v_l, bias_s, lam, subln, ma, main,
                                 l, bs, ts, lam_init)
        x1 = _wo_call(merged, xs, mod_s[l], w_o_b, ln1g, ln1b, l, n_s, 1, alpha)
        xs = _ffn_call(x1, mod_s[l], w_ff_in_b, w_ff_out_b, ln2g, ln2b, l, n_s, 1, alpha, nf)
        conv_s.append(main.reshape(bs, ts, -1)[:, ts - (CONV_W - 1):, 0:3 * d])
        st_s.append(sn)
        ks.append(k_l)
        vs.append(v_l)

    def heads_out(xs_list, nb, t):
        return jnp.stack(xs_list).reshape(depth, nb, t, h, HEAD)

    return (xp.reshape(bp, tp, d), xs.reshape(bs, ts, d),
            heads_out(kp, bp, tp), heads_out(vp, bp, tp), jnp.stack(conv_p), jnp.stack(st_p),
            heads_out(ks, bs, ts), heads_out(vs, bs, ts), jnp.stack(conv_s), jnp.stack(st_s))
```

```python
import functools
import math

import jax
import jax.numpy as jnp
from jax import lax
from jax.experimental import pallas as pl
from jax.experimental.pallas import tpu as pltpu

F32 = jnp.float32
BF16 = jnp.bfloat16
HIGHEST = lax.Precision.HIGHEST

LN_EPS = 1e-5
CHUNK = 64
HEAD = 128
CONV_W = 4
N_BUCKETS = 32
MAX_DIST = 128
NEG = -1e30
LOG2E = math.log2(math.e)
LANES = 128
CONV_PAD = 8
VMEM_LIMIT = 56 * 1024 * 1024

TM_PROJ = 512
TQ = 256
DELTA_CPS = 2


def _cparams(sem):
    return pltpu.CompilerParams(dimension_semantics=sem, vmem_limit_bytes=VMEM_LIMIT)


def _sigmoid(x):
    return 1.0 / (1.0 + jnp.exp(-x))


def _silu(x):
    return x * _sigmoid(x)


def _layer_norm(x, g, b):
    mu = jnp.mean(x, axis=-1, keepdims=True)
    xc = x - mu
    var = jnp.mean(xc * xc, axis=-1, keepdims=True)
    return xc * lax.rsqrt(var + LN_EPS) * g + b


def _dot(a, b):
    return jnp.dot(a, b, preferred_element_type=F32)


def _hdot(a, b):
    return jnp.dot(a, b, preferred_element_type=F32, precision=HIGHEST)


def _bdot(a, b):
    return _dot(a.astype(BF16), b.astype(BF16))


def _dot_nt(a, b):
    return lax.dot_general(a, b, (((1,), (1,)), ((), ())), preferred_element_type=F32)


def _dot_tn(a, b):
    return lax.dot_general(a, b, (((0,), (0,)), ((), ())), preferred_element_type=F32)


def _ln_kernel(x_ref, g_ref, b_ref, o_ref):
    o_ref[...] = _layer_norm(x_ref[...], g_ref[...], b_ref[...])


def _ln_call(x, g, b, tm):
    n, d = x.shape
    return pl.pallas_call(
        _ln_kernel,
        out_shape=jax.ShapeDtypeStruct((n, d), F32),
        grid=(n // tm,),
        in_specs=[pl.BlockSpec((tm, d), lambda i: (i, 0)),
                  pl.BlockSpec((1, d), lambda i: (0, 0)),
                  pl.BlockSpec((1, d), lambda i: (0, 0))],
        out_specs=pl.BlockSpec((tm, d), lambda i: (i, 0)),
        compiler_params=_cparams(("parallel",)),
        name="ln_in",
    )(x, g.reshape(1, d), b.reshape(1, d))


def _ada_kernel(c_ref, w_ref, b_ref, o_ref):
    c = c_ref[...]
    o_ref[...] = _dot(_silu(c).astype(BF16), w_ref[...]) + b_ref[...]


def _ada_call(c_all, w_ada, b_ada):
    depth, d, d6 = w_ada.shape
    nb = c_all.shape[0]
    return pl.pallas_call(
        _ada_kernel,
        out_shape=jax.ShapeDtypeStruct((depth, nb, d6), F32),
        grid=(depth, d6 // d),
        in_specs=[pl.BlockSpec((nb, d), lambda l, j: (0, 0)),
                  pl.BlockSpec((None, d, d), lambda l, j: (l, 0, j)),
                  pl.BlockSpec((None, 1, d), lambda l, j: (l, 0, j))],
        out_specs=pl.BlockSpec((None, nb, d), lambda l, j: (l, 0, j)),
        compiler_params=_cparams(("parallel", "parallel")),
        name="ada",
    )(c_all, w_ada, b_ada.reshape(depth, 1, d6))


def _mod_spec(mod, d, tiles_per_group, chunk):
    rows = mod.shape[1]
    return pl.BlockSpec((None, rows, d), lambda i, *_: (i // tiles_per_group, 0, chunk))


N_MAIN_TILES = 6
N_W_TILES = 9
N_CONV_TILES = 3


def _conv_taps(cat_ref, cw, rows):
    acc = None
    for i in range(CONV_W):
        term = cat_ref[pl.ds(CONV_PAD - (CONV_W - 1) + i, rows), :] * cw[i:i + 1, :]
        acc = term if acc is None else acc + term
    return acc


def _l2norm_heads(y, scale):
    outs = []
    for hh in range(y.shape[1] // HEAD):
        seg = y[:, hh * HEAD:(hh + 1) * HEAD]
        outs.append(seg * (lax.rsqrt(jnp.sum(seg * seg, axis=-1, keepdims=True) + 1e-6) * scale))
    return jnp.concatenate(outs, axis=1)


def _inproj_kernel(x_ref, sc_ref, sh_ref, w_ref, wba_ref, cw_ref, cb_ref,
                   main_ref, q_ref, kb_ref, vb_ref, k_ref, v_ref, ba_ref, tail_ref,
                   cat_scr, carry_scr, *, fuse_conv, tiles_per_group):
    tm, d = x_ref.shape
    h = (x_ref[...] * (1.0 + sc_ref[...]) + sh_ref[...]).astype(BF16)

    def cols(j):
        return _dot(h, w_ref[:, j * d:(j + 1) * d])

    if fuse_conv:
        @pl.when(pl.program_id(0) % tiles_per_group == 0)
        def _():
            for j in range(N_CONV_TILES):
                carry_scr[j, CONV_PAD - (CONV_W - 1):CONV_PAD, :] = cb_ref[:, j * d:(j + 1) * d]

    for j in range(N_MAIN_TILES):
        res = cols(j)
        if j < N_CONV_TILES:
            tail_ref[:, j * d:(j + 1) * d] = res[tm - CONV_PAD:tm, :]
            if fuse_conv:
                cat_scr[0:CONV_PAD, :] = carry_scr[j]
                cat_scr[CONV_PAD:CONV_PAD + tm, :] = res
                carry_scr[j] = res[tm - CONV_PAD:tm, :]
                res = _silu(_conv_taps(cat_scr, cw_ref[:, j * d:(j + 1) * d], tm))
                if j == 0:
                    res = _l2norm_heads(res, HEAD ** -0.5)
                elif j == 1:
                    res = _l2norm_heads(res, 1.0)
        main_ref[:, j * d:(j + 1) * d] = res.astype(main_ref.dtype)
    q_ref[...] = cols(N_MAIN_TILES).astype(BF16)
    for j, (b_ref, o_ref) in enumerate(((kb_ref, k_ref), (vb_ref, v_ref))):
        res = cols(N_MAIN_TILES + 1 + j)
        b_ref[...] = res.astype(BF16)
        o_ref[...] = res.reshape(o_ref.shape)
    ba_ref[...] = _dot(h, wba_ref[...])


def _resident(block_shape, index_map):
    return pl.BlockSpec(block_shape, index_map, pipeline_mode=pl.Buffered(1))


def _inproj_call(x, mod, w_all, w_ba, conv_w, conv_buf, layer, tm, tiles_per_group, main_dtype, fuse_conv):
    n, d = x.shape
    nba = w_ba.shape[-1]
    wide = N_W_TILES * d
    cwide = N_CONV_TILES * d
    in_specs = [
        pl.BlockSpec((tm, d), lambda i: (i, 0)),
        _mod_spec(mod, d, tiles_per_group, 1),
        _mod_spec(mod, d, tiles_per_group, 0),
        _resident((None, d, wide), lambda i: (layer, 0, 0)),
        _resident((None, d, nba), lambda i: (layer, 0, 0)),
        _resident((None, CONV_W, cwide), lambda i: (layer, 0, 0)),
        pl.BlockSpec((None, CONV_W - 1, cwide), lambda i: (i // tiles_per_group if fuse_conv else 0, 0, 0)),
    ]
    out_shape = [
        jax.ShapeDtypeStruct((n, N_MAIN_TILES * d), main_dtype),
        jax.ShapeDtypeStruct((n, d), BF16),
        jax.ShapeDtypeStruct((n, d), BF16),
        jax.ShapeDtypeStruct((n, d), BF16),
        jax.ShapeDtypeStruct((n, d // HEAD, HEAD), F32),
        jax.ShapeDtypeStruct((n, d // HEAD, HEAD), F32),
        jax.ShapeDtypeStruct((n, nba), F32),
        jax.ShapeDtypeStruct((n // tm, CONV_PAD, N_CONV_TILES * d), F32),
    ]
    out_specs = [
        pl.BlockSpec((tm, N_MAIN_TILES * d), lambda i: (i, 0)),
        pl.BlockSpec((tm, d), lambda i: (i, 0)),
        pl.BlockSpec((tm, d), lambda i: (i, 0)),
        pl.BlockSpec((tm, d), lambda i: (i, 0)),
        pl.BlockSpec((tm, d // HEAD, HEAD), lambda i: (i, 0, 0)),
        pl.BlockSpec((tm, d // HEAD, HEAD), lambda i: (i, 0, 0)),
        pl.BlockSpec((tm, nba), lambda i: (i, 0)),
        pl.BlockSpec((None, CONV_PAD, N_CONV_TILES * d), lambda i: (i, 0, 0)),
    ]
    return pl.pallas_call(
        functools.partial(_inproj_kernel, fuse_conv=fuse_conv, tiles_per_group=tiles_per_group),
        out_shape=out_shape,
        grid=(n // tm,),
        in_specs=in_specs,
        out_specs=out_specs,
        scratch_shapes=[pltpu.VMEM((CONV_PAD + tm, d), F32),
                        pltpu.VMEM((N_CONV_TILES, CONV_PAD, d), F32)],
        compiler_params=_cparams(("arbitrary",)),
        name="inproj",
    )(x, mod, mod, w_all, w_ba, conv_w, conv_buf)


def _tri_inverse(a_list, eye, blk, size):
    s = 8
    d = [jnp.where(blk[s], a, 0.0) for a in a_list]
    d2 = [_bdot(x, x) for x in d]
    p = [_bdot(eye - x, eye + y) for x, y in zip(d, d2)]
    d4 = [_bdot(x, x) for x in d2]
    t = [_bdot(x, eye + y) for x, y in zip(p, d4)]
    while s < size:
        lower = jnp.logical_and(blk[2 * s], jnp.logical_not(blk[s]))
        off = [jnp.where(lower, a, 0.0) for a in a_list]
        x = [_bdot(ti, oi) for ti, oi in zip(t, off)]
        t = [ti - _bdot(xi, ti) for ti, xi in zip(t, x)]
        s *= 2
    return t


def _delta_kernel(qu_ref, ku_ref, vu_ref, z_ref, ga_ref, ba_ref,
                  cwq_ref, cwk_ref, cwv_ref, par_ref, cbq_ref, cbk_ref, cbv_ref, s0_ref,
                  ma_ref, sn_ref, s_scr, cat_scr, *, chunk, cps, hb, pre_conv):
    c = pl.program_id(2)
    rows_blk = chunk * cps
    tail = CONV_W - 1

    @pl.when(c == 0)
    def _():
        s_scr[...] = s0_ref[...]
        if not pre_conv:
            for s, cb in enumerate((cbq_ref, cbk_ref, cbv_ref)):
                cat_scr[s, CONV_PAD - tail:CONV_PAD, :] = cb[...]

    if pre_conv:
        convs = [u_ref[...].astype(F32) for u_ref in (qu_ref, ku_ref, vu_ref)]
    else:
        convs = []
        for s, (u_ref, cw_ref) in enumerate(((qu_ref, cwq_ref), (ku_ref, cwk_ref), (vu_ref, cwv_ref))):
            cat_scr[s, CONV_PAD:CONV_PAD + rows_blk, :] = u_ref[...].astype(F32)
            y = _silu(_conv_taps(cat_scr.at[s], cw_ref[...], rows_blk))
            convs.append(y if s == 2 else _l2norm_heads(y, HEAD ** -0.5 if s == 0 else 1.0))
            cat_scr[s, CONV_PAD - tail:CONV_PAD, :] = cat_scr[s, CONV_PAD + rows_blk - tail:CONV_PAD + rows_blk, :]

    ba = ba_ref[...]
    dt_row = par_ref[0:1, :]
    alog_row = par_ref[1:2, :]
    norm_row = par_ref[2:3, 0:HEAD]
    beta_all = _sigmoid(ba)
    xa = ba + dt_row
    softplus = jnp.maximum(xa, 0.0) + jnp.log(1.0 + jnp.exp(-jnp.abs(xa)))
    g_all = -jnp.exp(alog_row) * softplus

    r = lax.broadcasted_iota(jnp.int32, (chunk, chunk), 0)
    cc = lax.broadcasted_iota(jnp.int32, (chunk, chunk), 1)
    causal = r >= cc
    strict = r > cc
    eye = jnp.where(r == cc, 1.0, 0.0).astype(F32)
    tril = jnp.where(causal, 1.0, 0.0).astype(F32)
    blk = {}
    s = 8
    while s <= chunk:
        blk[s] = jnp.bitwise_xor(r, cc) < s
        s *= 2

    heads = range(hb)
    pairs = [(ci, hh) for ci in range(cps) for hh in heads]
    rsl = [slice(ci * chunk, (ci + 1) * chunk) for ci in range(cps)]
    csl = [slice(hh * HEAD, (hh + 1) * HEAD) for hh in heads]
    gcum_all = [_hdot(tril, g_all[rs, :]) for rs in rsl]
    gcum_t = [g.T for g in gcum_all]

    q, k, v, beta, gcum, g_last, decay = [], [], [], [], [], [], []
    for ci, hh in pairs:
        rs, cs = rsl[ci], csl[hh]
        q.append(convs[0][rs, cs])
        k.append(convs[1][rs, cs])
        v.append(convs[2][rs, cs])
        beta.append(beta_all[rs, hh:hh + 1])
        gc = gcum_all[ci][:, hb + hh:hb + hh + 1]
        gr = gcum_t[ci][hb + hh:hb + hh + 1, :]
        gcum.append(gc)
        g_last.append(gc[chunk - 1:chunk, :])
        decay.append(jnp.where(causal, jnp.exp(jnp.where(causal, gc - gr, 0.0)), 0.0))
    idx = range(len(pairs))
    e_g = [jnp.exp(gcum[i]) for i in idx]
    kb = [k[i] * beta[i] for i in idx]
    k_b16 = [k[i].astype(BF16) for i in idx]
    kk_mat = [_dot_nt(kb[i].astype(BF16), k_b16[i]) for i in idx]
    qk = [_dot_nt(q[i].astype(BF16), k_b16[i]) * decay[i] for i in idx]
    a_mat = [jnp.where(strict, kk_mat[i] * decay[i], 0.0) for i in idx]
    t_inv = _tri_inverse(a_mat, eye, blk, chunk)
    uw = [_bdot(t_inv[i], jnp.concatenate([v[i] * beta[i], kb[i] * e_g[i]], axis=1)) for i in idx]
    q_g = [(q[i] * e_g[i]).astype(BF16) for i in idx]
    k_tail = [(k[i] * jnp.exp(g_last[i] - gcum[i])).astype(BF16) for i in idx]
    qk = [x.astype(BF16) for x in qk]

    for ci in range(cps):
        ids = [ci * hb + hh for hh in heads]
        state = [s_scr[hh] for hh in heads]
        state_b = [x.astype(BF16) for x in state]
        ws = [_dot(uw[i][:, HEAD:2 * HEAD].astype(BF16), state_b[hh]) for hh, i in zip(heads, ids)]
        u = [uw[i][:, 0:HEAD] - ws[hh] for hh, i in zip(heads, ids)]
        u_b = [x.astype(BF16) for x in u]
        qs = [_dot(q_g[i], state_b[hh]) for hh, i in zip(heads, ids)]
        ku = [_dot_tn(k_tail[i], u_b[hh]) for hh, i in zip(heads, ids)]
        qu = [_dot(qk[i], u_b[hh]) for hh, i in zip(heads, ids)]
        for hh, i in zip(heads, ids):
            s_scr[hh] = jnp.exp(g_last[i]) * state[hh] + ku[hh]
        for hh in heads:
            rs, cs = rsl[ci], csl[hh]
            o = qs[hh] + qu[hh]
            o = o * lax.rsqrt(jnp.mean(o * o, axis=-1, keepdims=True) + LN_EPS) * norm_row
            o = o * _silu(z_ref[rs, cs].astype(F32))
            ma_ref[rs, cs] = _sigmoid(ga_ref[rs, cs].astype(F32)) * o

    @pl.when(c == pl.num_programs(2) - 1)
    def _():
        sn_ref[...] = s_scr[...]


def _delta_call(main, ba, conv_w, par, conv_buf, s0, layer, nb, t, chunk, cps, hb, pre_conv):
    n = main.shape[0]
    d = main.shape[1] // N_MAIN_TILES
    h = d // HEAD
    ng = h // hb
    wb = hb * HEAD
    rows_blk = chunk * cps
    ncb = t // rows_blk
    grid = (nb, ng, ncb)

    def seg(s):
        return pl.BlockSpec((rows_blk, wb), lambda b, g, c: (b * ncb + c, s * ng + g))

    def cw(s):
        return pl.BlockSpec((None, CONV_W, wb), lambda b, g, c: (layer, 0, s * ng + g))

    def cb(s):
        return pl.BlockSpec((None, CONV_W - 1, wb), lambda b, g, c: (b, 0, s * ng + g))

    in_specs = [seg(0), seg(1), seg(2), seg(3), seg(4),
                pl.BlockSpec((rows_blk, LANES), lambda b, g, c: (b * ncb + c, g)),
                cw(0), cw(1), cw(2),
                pl.BlockSpec((None, 8, LANES), lambda b, g, c: (layer, 0, g)),
                cb(0), cb(1), cb(2),
                pl.BlockSpec((None, hb, HEAD, HEAD), lambda b, g, c: (b, g, 0, 0))]
    out_shape = [jax.ShapeDtypeStruct((n, d), F32),
                 jax.ShapeDtypeStruct((nb, h, HEAD, HEAD), F32)]
    out_specs = [pl.BlockSpec((rows_blk, wb), lambda b, g, c: (b * ncb + c, g)),
                 pl.BlockSpec((None, hb, HEAD, HEAD), lambda b, g, c: (b, g, 0, 0))]
    return pl.pallas_call(
        functools.partial(_delta_kernel, chunk=chunk, cps=cps, hb=hb, pre_conv=pre_conv),
        out_shape=out_shape,
        grid=grid,
        in_specs=in_specs,
        out_specs=out_specs,
        scratch_shapes=[pltpu.VMEM((hb, HEAD, HEAD), F32),
                        pltpu.VMEM((3, CONV_PAD + rows_blk, wb), F32)],
        compiler_params=_cparams(("parallel", "parallel", "arbitrary")),
        name="delta",
    )(main, main, main, main, main, ba, conv_w, conv_w, conv_w, par,
      conv_buf, conv_buf, conv_buf, s0)


def _bucket_thresholds():
    nb = N_BUCKETS // 2
    max_exact = nb // 2
    ratio = MAX_DIST // max_exact
    steps = nb - max_exact
    out = []
    for kk in range(1, steps):
        n = max_exact
        while n ** steps < (max_exact ** steps) * (ratio ** kk):
            n += 1
        out.append(n)
    return out


def _far_distance():
    return _bucket_thresholds()[-1]


def _rel_bias_tile(tab_ref, head, qpos, kpos):
    nb = N_BUCKETS // 2
    max_exact = nb // 2
    rel = kpos - qpos
    n = jnp.abs(rel)
    large = jnp.full(rel.shape, max_exact, jnp.int32)
    for thr in _bucket_thresholds():
        large = large + jnp.where(n >= thr, 1, 0)
    bucket = jnp.where(rel > 0, nb, 0) + jnp.where(n < max_exact, n, large)
    far = tab_ref[nb - 1, head]
    bias = jnp.zeros(rel.shape, F32)
    for b in range(N_BUCKETS):
        bias = jnp.where(bucket == b, (tab_ref[b, head] - far) * LOG2E, bias)
    shift = CHUNK.bit_length() - 1
    mask = lax.shift_right_logical(kpos, shift) <= lax.shift_right_logical(qpos, shift)
    return jnp.where(mask, bias, NEG)


def _bias_kernel(tab_ref, pt_ref, st_ref, *, tq, past, ts):
    head = pl.program_id(0)
    ki = lax.broadcasted_iota(jnp.int32, (tq, 2 * tq), 0)
    qi = lax.broadcasted_iota(jnp.int32, (tq, 2 * tq), 1)
    qi = jnp.where(qi >= tq, qi - tq, qi)
    pt_ref[0] = _rel_bias_tile(tab_ref, head, qi + tq, ki + tq)
    pt_ref[1] = _rel_bias_tile(tab_ref, head, qi + tq, ki)
    wk = past + LANES
    qs = lax.broadcasted_iota(jnp.int32, (ts, wk), 0) + past
    ks = lax.broadcasted_iota(jnp.int32, (ts, wk), 1)
    st = _rel_bias_tile(tab_ref, head, qs, ks)
    st_ref[...] = jnp.where(ks < past + ts, st, NEG)


def _bias_call(rel_bias, tq, past, ts):
    h = rel_bias.shape[1]
    wk = past + LANES
    return pl.pallas_call(
        functools.partial(_bias_kernel, tq=tq, past=past, ts=ts),
        out_shape=[jax.ShapeDtypeStruct((h, 2, tq, 2 * tq), F32),
                   jax.ShapeDtypeStruct((h, ts, wk), F32)],
        grid=(h,),
        in_specs=[pl.BlockSpec(memory_space=pltpu.SMEM)],
        out_specs=[pl.BlockSpec((None, 2, tq, 2 * tq), lambda i: (i, 0, 0, 0)),
                   pl.BlockSpec((None, ts, wk), lambda i: (i, 0, 0))],
        compiler_params=_cparams(("arbitrary",)),
        name="rel_bias",
    )(rel_bias)


def _lam_value(lam_ref, lam_init):
    lp = lam_ref[...]
    s1 = jnp.sum(lp[0:1, :] * lp[1:2, :], axis=-1, keepdims=True)
    s2 = jnp.sum(lp[2:3, :] * lp[3:4, :], axis=-1, keepdims=True)
    return jnp.exp(s1) - jnp.exp(s2) + lam_init


def _stack_maps(q):
    lane = lax.broadcasted_iota(jnp.int32, q.shape, 1)
    zero = jnp.zeros_like(q)
    half = HEAD // 2
    return jnp.concatenate([jnp.where(lane < half, q, zero), jnp.where(lane >= half, q, zero)], axis=0)


def _merge_out(o, lam_init, subln, ma, gb):
    o = o * lax.rsqrt(jnp.mean(o * o, axis=-1, keepdims=True) + LN_EPS) * subln * (1.0 - lam_init)
    return (ma + _sigmoid(gb.astype(F32)) * o).astype(BF16)


def _attn_block(n_blocks, q2, kbf, vt, bias_ref, tq):
    kv = n_blocks * tq
    pieces = []
    if n_blocks > 2:
        pieces.append((0, kv - 2 * tq, None))
    if n_blocks >= 2:
        pieces.append((kv - 2 * tq, tq, 1))
    pieces.append((kv - tq, tq, 0))
    scores = []
    for start, size, tile in pieces:
        s = _dot_nt(kbf[start:start + size, :], q2)
        if tile is not None:
            s = s + bias_ref[tile]
        scores.append(s)
    m = None
    for s in scores:
        mx = jnp.max(s, axis=0, keepdims=True)
        m = mx if m is None else jnp.maximum(m, mx)
    l = None
    acc = None
    for (start, size, _), s in zip(pieces, scores):
        p = jnp.exp2(s - m)
        ls = jnp.sum(p, axis=0, keepdims=True)
        pv = _dot(vt[:, start:start + size], p.astype(BF16))
        l = ls if l is None else l + ls
        acc = pv if acc is None else acc + pv
    return acc, l


def _attn_kernel(q_ref, k_ref, v_ref, bias_ref, lam_ref, subln_ref, ma_ref, gb_ref,
                 out_ref, vt, *, tq, nq, lam_init):
    vt[...] = v_ref[...].T
    lam_val = _lam_value(lam_ref, lam_init)
    subln = subln_ref[...]

    for n in range(1, nq + 1):
        rows = slice((n - 1) * tq, n * tq)
        q2 = _stack_maps(q_ref[rows, :])
        acc, l = _attn_block(n, q2, k_ref, vt, bias_ref, tq)
        o_t = acc[:, 0:tq] / l[:, 0:tq] - lam_val * (acc[:, tq:2 * tq] / l[:, tq:2 * tq])
        out_ref[rows, :] = _merge_out(o_t.T, lam_init, subln, ma_ref[rows, :], gb_ref[rows, :])


def _attn_call(q, k, v, bias, lam, subln, ma, main, layer, nb, t, tq, lam_init):
    n, d = q.shape
    h = d // HEAD
    nq = t // tq
    assert tq + 1 >= _far_distance() and tq % CHUNK == 0
    gb_col = (N_MAIN_TILES - 1) * h
    col = pl.BlockSpec((t, HEAD), lambda b, hh: (b, hh))
    in_specs = [
        col, col, col,
        pl.BlockSpec((None, 2, tq, 2 * tq), lambda b, hh: (hh, 0, 0, 0)),
        pl.BlockSpec((None, 4, HEAD // 2), lambda b, hh: (layer, 0, 0)),
        pl.BlockSpec((None, 1, HEAD), lambda b, hh: (layer, 0, 0)),
        col,
        pl.BlockSpec((t, HEAD), lambda b, hh: (b, gb_col + hh)),
    ]
    return pl.pallas_call(
        functools.partial(_attn_kernel, tq=tq, nq=nq, lam_init=lam_init),
        out_shape=jax.ShapeDtypeStruct((n, d), BF16),
        grid=(nb, h),
        in_specs=in_specs,
        out_specs=col,
        scratch_shapes=[pltpu.VMEM((HEAD, t), BF16)],
        compiler_params=_cparams(("parallel", "parallel")),
        name="attn",
    )(q, k, v, bias, lam, subln, ma, main)


def _attn_step_kernel(q_ref, kc_ref, vc_ref, kn_ref, vn_ref, bias_ref, lam_ref, subln_ref,
                      ma_ref, gb_ref, out_ref, *, ts, past, lam_init):
    q2 = _stack_maps(q_ref[...])
    pad = jnp.zeros((LANES - ts, HEAD), BF16)
    kc = kc_ref[...].astype(BF16)
    vc = vc_ref[...].astype(BF16)
    kn = jnp.concatenate([kn_ref[...], pad], axis=0)
    vn = jnp.concatenate([vn_ref[...], pad], axis=0)
    bias = bias_ref[...]
    bias2 = jnp.concatenate([bias, bias], axis=0)
    s_c = _dot_nt(q2, kc) + bias2[:, 0:past]
    s_n = _dot_nt(q2, kn) + bias2[:, past:past + LANES]
    m = jnp.maximum(jnp.max(s_c, axis=-1, keepdims=True), jnp.max(s_n, axis=-1, keepdims=True))
    p_c = jnp.exp2(s_c - m)
    p_n = jnp.exp2(s_n - m)
    l = jnp.sum(p_c, axis=-1, keepdims=True) + jnp.sum(p_n, axis=-1, keepdims=True)
    acc = _dot(p_c.astype(BF16), vc) + _dot(p_n.astype(BF16), vn)
    lam_val = _lam_value(lam_ref, lam_init)
    o = acc[0:ts] / l[0:ts] - lam_val * (acc[ts:2 * ts] / l[ts:2 * ts])
    out_ref[...] = _merge_out(o, lam_init, subln_ref[...], ma_ref[...], gb_ref[...])


def _attn_step_call(q, cache_k, cache_v, k_new, v_new, bias, lam, subln, ma, main, layer, nb, ts, lam_init):
    n, d = q.shape
    h = d // HEAD
    past = cache_k.shape[1] // nb
    gb_col = (N_MAIN_TILES - 1) * h
    row = lambda b, hh: (b, hh)
    lrow = lambda b, hh: (layer, b, hh)
    in_specs = [
        pl.BlockSpec((ts, HEAD), row),
        pl.BlockSpec((None, past, HEAD), lrow),
        pl.BlockSpec((None, past, HEAD), lrow),
        pl.BlockSpec((ts, HEAD), row),
        pl.BlockSpec((ts, HEAD), row),
        pl.BlockSpec((None, ts, past + LANES), lambda b, hh: (hh, 0, 0)),
        pl.BlockSpec((None, 4, HEAD // 2), lambda b, hh: (layer, 0, 0)),
        pl.BlockSpec((None, 1, HEAD), lambda b, hh: (layer, 0, 0)),
        pl.BlockSpec((ts, HEAD), row),
        pl.BlockSpec((ts, HEAD), lambda b, hh: (b, gb_col + hh)),
    ]
    return pl.pallas_call(
        functools.partial(_attn_step_kernel, ts=ts, past=past, lam_init=lam_init),
        out_shape=jax.ShapeDtypeStruct((n, d), BF16),
        grid=(nb, h),
        in_specs=in_specs,
        out_specs=pl.BlockSpec((ts, HEAD), row),
        compiler_params=_cparams(("parallel", "parallel")),
        name="attn_step",
    )(q, cache_k, cache_v, k_new, v_new, bias, lam, subln, ma, main)


def _wo_kernel(m_ref, x_ref, gt_ref, w_ref, g_ref, b_ref, o_ref, *, alpha):
    y = _dot(m_ref[...], w_ref[...])
    o_ref[...] = _layer_norm(alpha * x_ref[...] + gt_ref[...] * y, g_ref[...], b_ref[...])


def _wo_call(merged, x, mod, w_o, ln_g, ln_b, layer, tm, tiles_per_group, alpha):
    n, d = x.shape
    vec = pl.BlockSpec((None, 1, d), lambda i: (layer, 0, 0))
    return pl.pallas_call(
        functools.partial(_wo_kernel, alpha=alpha),
        out_shape=jax.ShapeDtypeStruct((n, d), F32),
        grid=(n // tm,),
        in_specs=[pl.BlockSpec((tm, d), lambda i: (i, 0)),
                  pl.BlockSpec((tm, d), lambda i: (i, 0)),
                  _mod_spec(mod, d, tiles_per_group, 2),
                  pl.BlockSpec((None, d, d), lambda i: (layer, 0, 0)),
                  vec, vec],
        out_specs=pl.BlockSpec((tm, d), lambda i: (i, 0)),
        compiler_params=_cparams(("parallel",)),
        name="wo",
    )(merged, x, mod, w_o, ln_g, ln_b)


def _ffn_kernel(x_ref, sc_ref, sh_ref, gt_ref, wi_ref, wo_ref, g_ref, b_ref, o_ref, *, alpha, nf):
    x = x_ref[...]
    h = (x * (1.0 + sc_ref[...]) + sh_ref[...]).astype(BF16)
    dff = wo_ref.shape[0]
    tf = dff // nf
    y = None
    for j in range(nf):
        u = _dot(h, wi_ref[:, j * tf:(j + 1) * tf])
        v = _dot(h, wi_ref[:, dff + j * tf:dff + (j + 1) * tf])
        part = _dot((_silu(u) * v).astype(BF16), wo_ref[j * tf:(j + 1) * tf, :])
        y = part if y is None else y + part
    o_ref[...] = _layer_norm(alpha * x + gt_ref[...] * y, g_ref[...], b_ref[...])


def _ffn_call(x, mod, w_ff_in, w_ff_out, ln_g, ln_b, layer, tm, tiles_per_group, alpha, nf):
    n, d = x.shape
    dff = w_ff_out.shape[1]
    vec = pl.BlockSpec((None, 1, d), lambda i: (layer, 0, 0))
    return pl.pallas_call(
        functools.partial(_ffn_kernel, alpha=alpha, nf=nf),
        out_shape=jax.ShapeDtypeStruct((n, d), F32),
        grid=(n // tm,),
        in_specs=[pl.BlockSpec((tm, d), lambda i: (i, 0)),
                  _mod_spec(mod, d, tiles_per_group, 4),
                  _mod_spec(mod, d, tiles_per_group, 3),
                  _mod_spec(mod, d, tiles_per_group, 5),
                  _resident((None, d, 2 * dff), lambda i: (layer, 0, 0)),
                  _resident((None, dff, d), lambda i: (layer, 0, 0)),
                  vec, vec],
        out_specs=pl.BlockSpec((tm, d), lambda i: (i, 0)),
        compiler_params=_cparams(("parallel",)),
        name="ffn",
    )(x, mod, mod, mod, w_ff_in, w_ff_out, ln_g, ln_b)


def _pick_tile(n, pref):
    t = min(n, pref)
    while n % t:
        t //= 2
    return t


def kernel(x_prompt, x_sample, cache_k, cache_v, state_conv, state_delta, c_prompt, c_sample,
           ln_in_g, ln_in_b, rel_bias, w_ada, b_ada, w_in, conv_w, a_log, dt_bias, norm_a,
           lam, subln_g, w_o, ln1_g, ln1_b, w_ff_in, w_ff_out, ln2_g, ln2_b):
    bp, tp, d = x_prompt.shape
    bs, ts, _ = x_sample.shape
    depth = w_in.shape[0]
    h = d // HEAD
    past = cache_k.shape[2]
    dff = w_ff_out.shape[1]
    n_p = bp * tp
    n_s = bs * ts
    alpha = (2 * depth) ** 0.25
    hb = h
    ng = h // hb

    o1 = 4 * d + 2 * h
    qkvz = w_in[:, :, 0:4 * d]
    gates = w_in[:, :, o1 + 3 * d:o1 + 5 * d]
    qb = w_in[:, :, o1:o1 + d] * ((HEAD // 2) ** -0.5 * LOG2E)
    kvb = w_in[:, :, o1 + d:o1 + 3 * d]
    w_all = jnp.concatenate([qkvz, gates, qb, kvb], axis=-1).astype(BF16)
    wb = w_in[:, :, 4 * d:4 * d + h].reshape(depth, d, ng, hb)
    wa = w_in[:, :, 4 * d + h:4 * d + 2 * h].reshape(depth, d, ng, hb)
    w_ba = jnp.concatenate([wb, wa, jnp.zeros((depth, d, ng, LANES - 2 * hb), F32)], axis=-1)
    w_ba = w_ba.reshape(depth, d, ng * LANES).astype(BF16)
    w_ada_b = w_ada.astype(BF16)
    w_o_b = w_o.astype(BF16)
    w_ff_in_b = w_ff_in.astype(BF16)
    w_ff_out_b = w_ff_out.astype(BF16)

    def lane_rows(v):
        v = v.reshape(depth, ng, hb)
        z = jnp.zeros((depth, ng, hb), F32)
        zz = jnp.zeros((depth, ng, LANES - 2 * hb), F32)
        return jnp.concatenate([z, v, zz], axis=-1).reshape(depth, ng * LANES)

    par = jnp.zeros((depth, 8, ng * LANES), F32)
    par = par.at[:, 0].set(lane_rows(dt_bias)).at[:, 1].set(lane_rows(a_log))
    par = par.at[:, 2].set(jnp.tile(norm_a, (1, ng)))
    subln = subln_g.reshape(depth, 1, HEAD)
    ln1g, ln1b = ln1_g.reshape(depth, 1, d), ln1_b.reshape(depth, 1, d)
    ln2g, ln2b = ln2_g.reshape(depth, 1, d), ln2_b.reshape(depth, 1, d)

    mod = _ada_call(jnp.concatenate([c_prompt, c_sample], axis=0), w_ada_b, b_ada)
    mod_p = mod[:, :bp].reshape(depth, bp, 1, 6 * d)
    mod_s = jnp.repeat(mod[:, bp:], ts, axis=1).reshape(depth, 1, n_s, 6 * d)

    bias_p, bias_s = _bias_call(rel_bias, TQ, past, ts)

    tm_p = _pick_tile(tp, TM_PROJ)
    tpg_p = tp // tm_p
    xp = _ln_call(x_prompt.reshape(n_p, d), ln_in_g, ln_in_b, tm_p)
    xs = _ln_call(x_sample.reshape(n_s, d), ln_in_g, ln_in_b, n_s)

    conv0 = jnp.zeros((bp, CONV_W - 1, 3 * d), F32)
    s0 = jnp.zeros((bp, h, HEAD, HEAD), F32)
    cache_k2 = cache_k.reshape(depth, bs * past, d)
    cache_v2 = cache_v.reshape(depth, bs * past, d)
    nf = 2 if dff % (2 * LANES) == 0 else 1

    kp, vp, ks, vs = [], [], [], []
    conv_p, conv_s, st_p, st_s = [], [], [], []
    for l in range(depth):
        lam_init = 0.8 - 0.6 * math.exp(-0.3 * l)
        main, qb_p, kb_l, vb_l, k_l, v_l, ba, tail = _inproj_call(
            xp, mod_p[l], w_all, w_ba, conv_w, conv0, l, tm_p, tpg_p, BF16, True)
        ma, sn = _delta_call(main, ba, conv_w, par, conv0, s0, l, bp, tp, CHUNK, DELTA_CPS, hb, True)
        merged = _attn_call(qb_p, kb_l, vb_l, bias_p, lam, subln, ma, main, l, bp, tp, TQ, lam_init)
        x1 = _wo_call(merged, xp, mod_p[l], w_o_b, ln1g, ln1b, l, tm_p, tpg_p, alpha)
        xp = _ffn_call(x1, mod_p[l], w_ff_in_b, w_ff_out_b, ln2g, ln2b, l, tm_p, tpg_p, alpha, nf)
        conv_p.append(tail.reshape(bp, tpg_p, CONV_PAD, 3 * d)[:, tpg_p - 1, CONV_PAD - (CONV_W - 1):, :])
        st_p.append(sn)
        kp.append(k_l)
        vp.append(v_l)
        main, qb_s, kb_l, vb_l, k_l, v_l, ba, _ = _inproj_call(
            xs, mod_s[l], w_all, w_ba, conv_w, state_conv[l], l, n_s, 1, F32, False)
        ma, sn = _delta_call(main, ba, conv_w, par, state_conv[l], state_delta[l], l, bs, ts, ts, 1, hb, False)
        merged = _attn_step_call(qb_s, cache_k2, cache_v2, kb_l, vb_l, bias_s, lam, subln, ma, main,
                                 l, bs, ts, lam_init)
        x1 = _wo_call(merged, xs, mod_s[l], w_o_b, ln1g, ln1b, l, n_s, 1, alpha)
        xs = _ffn_call(x1, mod_s[l], w_ff_in_b, w_ff_out_b, ln2g, ln2b, l, n_s, 1, alpha, nf)
        conv_s.append(main.reshape(bs, ts, -1)[:, ts - (CONV_W - 1):, 0:3 * d])
        st_s.append(sn)
        ks.append(k_l)
        vs.append(v_l)

    def heads_out(xs_list, nb, t):
        return jnp.stack(xs_list).reshape(depth, nb, t, h, HEAD)

    return (xp.reshape(bp, tp, d), xs.reshape(bs, ts, d),
            heads_out(kp, bp, tp), heads_out(vp, bp, tp), jnp.stack(conv_p), jnp.stack(st_p),
            heads_out(ks, bs, ts), heads_out(vs, bs, ts), jnp.stack(conv_s), jnp.stack(st_s))
```

```python
import functools
import math

import jax
import jax.numpy as jnp
from jax import lax
from jax.experimental import pallas as pl
from jax.experimental.pallas import tpu as pltpu

F32 = jnp.float32
BF16 = jnp.bfloat16
HIGHEST = lax.Precision.HIGHEST

LN_EPS = 1e-5
CHUNK = 64
HEAD = 128
CONV_W = 4
N_BUCKETS = 32
MAX_DIST = 128
NEG = -1e30
LOG2E = math.log2(math.e)
LANES = 128
CONV_PAD = 8
VMEM_LIMIT = 56 * 1024 * 1024

TM_PROJ = 512
TQ = 256
DELTA_CPS = 2


def _cparams(sem):
    return pltpu.CompilerParams(dimension_semantics=sem, vmem_limit_bytes=VMEM_LIMIT)


def _sigmoid(x):
    return 1.0 / (1.0 + jnp.exp(-x))


def _silu(x):
    return x * _sigmoid(x)


def _layer_norm(x, g, b):
    mu = jnp.mean(x, axis=-1, keepdims=True)
    xc = x - mu
    var = jnp.mean(xc * xc, axis=-1, keepdims=True)
    return xc * lax.rsqrt(var + LN_EPS) * g + b


def _dot(a, b):
    return jnp.dot(a, b, preferred_element_type=F32)


def _hdot(a, b):
    return jnp.dot(a, b, preferred_element_type=F32, precision=HIGHEST)


def _bdot(a, b):
    return _dot(a.astype(BF16), b.astype(BF16))


def _dot_nt(a, b):
    return lax.dot_general(a, b, (((1,), (1,)), ((), ())), preferred_element_type=F32)


def _dot_tn(a, b):
    return lax.dot_general(a, b, (((0,), (0,)), ((), ())), preferred_element_type=F32)


def _ln_kernel(x_ref, g_ref, b_ref, o_ref):
    o_ref[...] = _layer_norm(x_ref[...], g_ref[...], b_ref[...])


def _ln_call(x, g, b, tm):
    n, d = x.shape
    return pl.pallas_call(
        _ln_kernel,
        out_shape=jax.ShapeDtypeStruct((n, d), F32),
        grid=(n // tm,),
        in_specs=[pl.BlockSpec((tm, d), lambda i: (i, 0)),
                  pl.BlockSpec((1, d), lambda i: (0, 0)),
                  pl.BlockSpec((1, d), lambda i: (0, 0))],
        out_specs=pl.BlockSpec((tm, d), lambda i: (i, 0)),
        compiler_params=_cparams(("parallel",)),
        name="ln_in",
    )(x, g.reshape(1, d), b.reshape(1, d))


def _ada_kernel(c_ref, w_ref, b_ref, o_ref):
    c = c_ref[...]
    o_ref[...] = _dot(_silu(c).astype(BF16), w_ref[...]) + b_ref[...]


def _ada_call(c_all, w_ada, b_ada):
    depth, d, d6 = w_ada.shape
    nb = c_all.shape[0]
    return pl.pallas_call(
        _ada_kernel,
        out_shape=jax.ShapeDtypeStruct((depth, nb, d6), F32),
        grid=(depth, d6 // d),
        in_specs=[pl.BlockSpec((nb, d), lambda l, j: (0, 0)),
                  pl.BlockSpec((None, d, d), lambda l, j: (l, 0, j)),
                  pl.BlockSpec((None, 1, d), lambda l, j: (l, 0, j))],
        out_specs=pl.BlockSpec((None, nb, d), lambda l, j: (l, 0, j)),
        compiler_params=_cparams(("parallel", "parallel")),
        name="ada",
    )(c_all, w_ada, b_ada.reshape(depth, 1, d6))


def _mod_spec(mod, d, tiles_per_group, chunk):
    rows = mod.shape[1]
    return pl.BlockSpec((None, rows, d), lambda i, *_: (i // tiles_per_group, 0, chunk))


N_MAIN_TILES = 6
N_W_TILES = 9
N_CONV_TILES = 3


def _conv_taps(cat_ref, cw, rows):
    acc = None
    for i in range(CONV_W):
        term = cat_ref[pl.ds(CONV_PAD - (CONV_W - 1) + i, rows), :] * cw[i:i + 1, :]
        acc = term if acc is None else acc + term
    return acc


def _conv_taps_rolled(prev, cur, cw):
    rows = cur.shape[0]
    cat = jnp.concatenate([prev, cur], axis=0)
    acc = cur * cw[CONV_W - 1:CONV_W, :]
    for back in range(1, CONV_W):
        shifted = pltpu.roll(cat, back, axis=0)[CONV_PAD:CONV_PAD + rows, :]
        acc = acc + shifted * cw[CONV_W - 1 - back:CONV_W - back, :]
    return acc


def _l2norm_heads(y, scale):
    outs = []
    for hh in range(y.shape[1] // HEAD):
        seg = y[:, hh * HEAD:(hh + 1) * HEAD]
        outs.append(seg * (lax.rsqrt(jnp.sum(seg * seg, axis=-1, keepdims=True) + 1e-6) * scale))
    return jnp.concatenate(outs, axis=1)


def _inproj_kernel(x_ref, sc_ref, sh_ref, w_ref, wba_ref, cw_ref, cb_ref, kin_ref, vin_ref,
                   main_ref, q_ref, kb_ref, vb_ref, k_ref, v_ref, ba_ref, tail_ref,
                   carry_scr, *, fuse_conv, tiles_per_group):
    tm, d = x_ref.shape
    h = (x_ref[...] * (1.0 + sc_ref[...]) + sh_ref[...]).astype(BF16)

    def cols(j):
        return _dot(h, w_ref[:, j * d:(j + 1) * d])

    if fuse_conv:
        @pl.when(pl.program_id(0) % tiles_per_group == 0)
        def _():
            for j in range(N_CONV_TILES):
                carry_scr[j, CONV_PAD - (CONV_W - 1):CONV_PAD, :] = cb_ref[:, j * d:(j + 1) * d]

    order = []
    for j in range(N_CONV_TILES):
        order += [N_CONV_TILES + j, j]
    for j in order + list(range(2 * N_CONV_TILES, N_MAIN_TILES)):
        res = cols(j)
        if j < N_CONV_TILES:
            last = res[tm - CONV_PAD:tm, :]
            tail_ref[:, j * d:(j + 1) * d] = last
            if fuse_conv:
                prev = carry_scr[j]
                carry_scr[j] = last
                res = _silu(_conv_taps_rolled(prev, res, cw_ref[:, j * d:(j + 1) * d]))
                if j == 0:
                    res = _l2norm_heads(res, HEAD ** -0.5)
                elif j == 1:
                    res = _l2norm_heads(res, 1.0)
        main_ref[:, j * d:(j + 1) * d] = res.astype(main_ref.dtype)
    q_ref[...] = cols(N_MAIN_TILES).astype(BF16)
    for j, (b_ref, o_ref) in enumerate(((kb_ref, k_ref), (vb_ref, v_ref))):
        res = cols(N_MAIN_TILES + 1 + j)
        b_ref[...] = res.astype(BF16)
        o_ref[...] = res.reshape(o_ref.shape)
    ba_ref[...] = _dot(h, wba_ref[...])


def _resident(block_shape, index_map):
    return pl.BlockSpec(block_shape, index_map, pipeline_mode=pl.Buffered(1))


def _inproj_call(x, mod, w_all, w_ba, conv_w, conv_buf, kbuf, vbuf, layer, tm, tiles_per_group,
                 main_dtype, fuse_conv):
    n, d = x.shape
    nba = w_ba.shape[-1]
    wide = N_W_TILES * d
    cwide = N_CONV_TILES * d
    in_specs = [
        pl.BlockSpec((tm, d), lambda i: (i, 0)),
        _mod_spec(mod, d, tiles_per_group, 1),
        _mod_spec(mod, d, tiles_per_group, 0),
        _resident((None, d, wide), lambda i: (layer, 0, 0)),
        _resident((None, d, nba), lambda i: (layer, 0, 0)),
        _resident((None, CONV_W, cwide), lambda i: (layer, 0, 0)),
        pl.BlockSpec((None, CONV_W - 1, cwide), lambda i: (i // tiles_per_group if fuse_conv else 0, 0, 0)),
        pl.BlockSpec(memory_space=pl.ANY),
        pl.BlockSpec(memory_space=pl.ANY),
    ]
    out_shape = [
        jax.ShapeDtypeStruct((n, N_MAIN_TILES * d), main_dtype),
        jax.ShapeDtypeStruct((n, d), BF16),
        jax.ShapeDtypeStruct((n, d), BF16),
        jax.ShapeDtypeStruct((n, d), BF16),
        jax.ShapeDtypeStruct(kbuf.shape, F32),
        jax.ShapeDtypeStruct(vbuf.shape, F32),
        jax.ShapeDtypeStruct((n, nba), F32),
        jax.ShapeDtypeStruct((n // tm, CONV_PAD, N_CONV_TILES * d), F32),
    ]
    out_specs = [
        pl.BlockSpec((tm, N_MAIN_TILES * d), lambda i: (i, 0)),
        pl.BlockSpec((tm, d), lambda i: (i, 0)),
        pl.BlockSpec((tm, d), lambda i: (i, 0)),
        pl.BlockSpec((tm, d), lambda i: (i, 0)),
        pl.BlockSpec((None, tm, d // HEAD, HEAD), lambda i: (layer, i, 0, 0)),
        pl.BlockSpec((None, tm, d // HEAD, HEAD), lambda i: (layer, i, 0, 0)),
        pl.BlockSpec((tm, nba), lambda i: (i, 0)),
        pl.BlockSpec((None, CONV_PAD, N_CONV_TILES * d), lambda i: (i, 0, 0)),
    ]
    return pl.pallas_call(
        functools.partial(_inproj_kernel, fuse_conv=fuse_conv, tiles_per_group=tiles_per_group),
        out_shape=out_shape,
        grid=(n // tm,),
        in_specs=in_specs,
        out_specs=out_specs,
        scratch_shapes=[pltpu.VMEM((N_CONV_TILES, CONV_PAD, d), F32)],
        input_output_aliases={7: 4, 8: 5},
        compiler_params=_cparams(("arbitrary",)),
        name="inproj",
    )(x, mod, mod, w_all, w_ba, conv_w, conv_buf, kbuf, vbuf)


def _tri_inverse(a_list, eye, blk, size):
    s = 8
    d = [jnp.where(blk[s], a, 0.0) for a in a_list]
    d2 = [_bdot(x, x) for x in d]
    p = [_bdot(eye - x, eye + y) for x, y in zip(d, d2)]
    d4 = [_bdot(x, x) for x in d2]
    t = [_bdot(x, eye + y) for x, y in zip(p, d4)]
    while s < size:
        lower = jnp.logical_and(blk[2 * s], jnp.logical_not(blk[s]))
        off = [jnp.where(lower, a, 0.0) for a in a_list]
        x = [_bdot(ti, oi) for ti, oi in zip(t, off)]
        t = [ti - _bdot(xi, ti) for ti, xi in zip(t, x)]
        s *= 2
    return t


def _delta_kernel(qu_ref, ku_ref, vu_ref, z_ref, ga_ref, ba_ref,
                  cwq_ref, cwk_ref, cwv_ref, par_ref, cbq_ref, cbk_ref, cbv_ref, s0_ref,
                  ma_ref, sn_ref, s_scr, cat_scr, *, chunk, cps, hb, pre_conv):
    c = pl.program_id(2)
    rows_blk = chunk * cps
    tail = CONV_W - 1

    @pl.when(c == 0)
    def _():
        s_scr[...] = s0_ref[...]
        if not pre_conv:
            for s, cb in enumerate((cbq_ref, cbk_ref, cbv_ref)):
                cat_scr[s, CONV_PAD - tail:CONV_PAD, :] = cb[...]

    if pre_conv:
        convs = [u_ref[...].astype(F32) for u_ref in (qu_ref, ku_ref, vu_ref)]
    else:
        convs = []
        for s, (u_ref, cw_ref) in enumerate(((qu_ref, cwq_ref), (ku_ref, cwk_ref), (vu_ref, cwv_ref))):
            cat_scr[s, CONV_PAD:CONV_PAD + rows_blk, :] = u_ref[...].astype(F32)
            y = _silu(_conv_taps(cat_scr.at[s], cw_ref[...], rows_blk))
            convs.append(y if s == 2 else _l2norm_heads(y, HEAD ** -0.5 if s == 0 else 1.0))
            cat_scr[s, CONV_PAD - tail:CONV_PAD, :] = cat_scr[s, CONV_PAD + rows_blk - tail:CONV_PAD + rows_blk, :]

    ba = ba_ref[...]
    dt_row = par_ref[0:1, :]
    alog_row = par_ref[1:2, :]
    norm_row = par_ref[2:3, 0:HEAD]
    beta_all = _sigmoid(ba)
    xa = ba + dt_row
    softplus = jnp.maximum(xa, 0.0) + jnp.log(1.0 + jnp.exp(-jnp.abs(xa)))
    g_all = -jnp.exp(alog_row) * softplus

    r = lax.broadcasted_iota(jnp.int32, (chunk, chunk), 0)
    cc = lax.broadcasted_iota(jnp.int32, (chunk, chunk), 1)
    causal = r >= cc
    strict = r > cc
    eye = jnp.where(r == cc, 1.0, 0.0).astype(F32)
    tril = jnp.where(causal, 1.0, 0.0).astype(F32)
    blk = {}
    s = 8
    while s <= chunk:
        blk[s] = jnp.bitwise_xor(r, cc) < s
        s *= 2

    heads = range(hb)
    pairs = [(ci, hh) for ci in range(cps) for hh in heads]
    rsl = [slice(ci * chunk, (ci + 1) * chunk) for ci in range(cps)]
    csl = [slice(hh * HEAD, (hh + 1) * HEAD) for hh in heads]
    gcum_all = [_hdot(tril, g_all[rs, :]) for rs in rsl]
    gcum_t = [g.T for g in gcum_all]

    q, k, v, beta, gcum, g_last, decay = [], [], [], [], [], [], []
    for ci, hh in pairs:
        rs, cs = rsl[ci], csl[hh]
        q.append(convs[0][rs, cs])
        k.append(convs[1][rs, cs])
        v.append(convs[2][rs, cs])
        beta.append(beta_all[rs, hh:hh + 1])
        gc = gcum_all[ci][:, hb + hh:hb + hh + 1]
        gr = gcum_t[ci][hb + hh:hb + hh + 1, :]
        gcum.append(gc)
        g_last.append(gc[chunk - 1:chunk, :])
        decay.append(jnp.where(causal, jnp.exp(jnp.where(causal, gc - gr, 0.0)), 0.0))
    idx = range(len(pairs))
    e_g = [jnp.exp(gcum[i]) for i in idx]
    kb = [k[i] * beta[i] for i in idx]
    k_b16 = [k[i].astype(BF16) for i in idx]
    kk_mat = [_dot_nt(kb[i].astype(BF16), k_b16[i]) for i in idx]
    qk = [_dot_nt(q[i].astype(BF16), k_b16[i]) * decay[i] for i in idx]
    a_mat = [jnp.where(strict, kk_mat[i] * decay[i], 0.0) for i in idx]
    t_inv = _tri_inverse(a_mat, eye, blk, chunk)
    uw = [_bdot(t_inv[i], jnp.concatenate([v[i] * beta[i], kb[i] * e_g[i]], axis=1)) for i in idx]
    q_g = [(q[i] * e_g[i]).astype(BF16) for i in idx]
    k_tail = [(k[i] * jnp.exp(g_last[i] - gcum[i])).astype(BF16) for i in idx]
    qk = [x.astype(BF16) for x in qk]

    for ci in range(cps):
        ids = [ci * hb + hh for hh in heads]
        state = [s_scr[hh] for hh in heads]
        state_b = [x.astype(BF16) for x in state]
        ws = [_dot(uw[i][:, HEAD:2 * HEAD].astype(BF16), state_b[hh]) for hh, i in zip(heads, ids)]
        u = [uw[i][:, 0:HEAD] - ws[hh] for hh, i in zip(heads, ids)]
        u_b = [x.astype(BF16) for x in u]
        qs = [_dot(q_g[i], state_b[hh]) for hh, i in zip(heads, ids)]
        ku = [_dot_tn(k_tail[i], u_b[hh]) for hh, i in zip(heads, ids)]
        qu = [_dot(qk[i], u_b[hh]) for hh, i in zip(heads, ids)]
        for hh, i in zip(heads, ids):
            s_scr[hh] = jnp.exp(g_last[i]) * state[hh] + ku[hh]
        for hh in heads:
            rs, cs = rsl[ci], csl[hh]
            o = qs[hh] + qu[hh]
            o = o * lax.rsqrt(jnp.mean(o * o, axis=-1, keepdims=True) + LN_EPS) * norm_row
            o = o * _silu(z_ref[rs, cs].astype(F32))
            ma_ref[rs, cs] = _sigmoid(ga_ref[rs, cs].astype(F32)) * o

    @pl.when(c == pl.num_programs(2) - 1)
    def _():
        sn_ref[...] = s_scr[...]


def _delta_call(main, ba, conv_w, par, conv_buf, s0, layer, nb, t, chunk, cps, hb, pre_conv):
    n = main.shape[0]
    d = main.shape[1] // N_MAIN_TILES
    h = d // HEAD
    ng = h // hb
    wb = hb * HEAD
    rows_blk = chunk * cps
    ncb = t // rows_blk
    grid = (nb, ng, ncb)

    def seg(s):
        return pl.BlockSpec((rows_blk, wb), lambda b, g, c: (b * ncb + c, s * ng + g))

    def cw(s):
        return pl.BlockSpec((None, CONV_W, wb), lambda b, g, c: (layer, 0, s * ng + g))

    def cb(s):
        return pl.BlockSpec((None, CONV_W - 1, wb), lambda b, g, c: (b, 0, s * ng + g))

    in_specs = [seg(0), seg(1), seg(2), seg(3), seg(4),
                pl.BlockSpec((rows_blk, LANES), lambda b, g, c: (b * ncb + c, g)),
                cw(0), cw(1), cw(2),
                pl.BlockSpec((None, 8, LANES), lambda b, g, c: (layer, 0, g)),
                cb(0), cb(1), cb(2),
                pl.BlockSpec((None, hb, HEAD, HEAD), lambda b, g, c: (b, g, 0, 0))]
    out_shape = [jax.ShapeDtypeStruct((n, d), F32),
                 jax.ShapeDtypeStruct((nb, h, HEAD, HEAD), F32)]
    out_specs = [pl.BlockSpec((rows_blk, wb), lambda b, g, c: (b * ncb + c, g)),
                 pl.BlockSpec((None, hb, HEAD, HEAD), lambda b, g, c: (b, g, 0, 0))]
    return pl.pallas_call(
        functools.partial(_delta_kernel, chunk=chunk, cps=cps, hb=hb, pre_conv=pre_conv),
        out_shape=out_shape,
        grid=grid,
        in_specs=in_specs,
        out_specs=out_specs,
        scratch_shapes=[pltpu.VMEM((hb, HEAD, HEAD), F32),
                        pltpu.VMEM((3, CONV_PAD + rows_blk, wb), F32)],
        compiler_params=_cparams(("parallel", "parallel", "arbitrary")),
        name="delta",
    )(main, main, main, main, main, ba, conv_w, conv_w, conv_w, par,
      conv_buf, conv_buf, conv_buf, s0)


def _bucket_thresholds():
    nb = N_BUCKETS // 2
    max_exact = nb // 2
    ratio = MAX_DIST // max_exact
    steps = nb - max_exact
    out = []
    for kk in range(1, steps):
        n = max_exact
        while n ** steps < (max_exact ** steps) * (ratio ** kk):
            n += 1
        out.append(n)
    return out


def _far_distance():
    return _bucket_thresholds()[-1]


def _rel_bias_tile(tab_ref, head, qpos, kpos):
    nb = N_BUCKETS // 2
    max_exact = nb // 2
    rel = kpos - qpos
    n = jnp.abs(rel)
    large = jnp.full(rel.shape, max_exact, jnp.int32)
    for thr in _bucket_thresholds():
        large = large + jnp.where(n >= thr, 1, 0)
    bucket = jnp.where(rel > 0, nb, 0) + jnp.where(n < max_exact, n, large)
    far = tab_ref[nb - 1, head]
    bias = jnp.zeros(rel.shape, F32)
    for b in range(N_BUCKETS):
        bias = jnp.where(bucket == b, (tab_ref[b, head] - far) * LOG2E, bias)
    shift = CHUNK.bit_length() - 1
    mask = lax.shift_right_logical(kpos, shift) <= lax.shift_right_logical(qpos, shift)
    return jnp.where(mask, bias, NEG)


def _bias_kernel(tab_ref, pt_ref, st_ref, *, tq, past, ts):
    head = pl.program_id(0)
    ki = lax.broadcasted_iota(jnp.int32, (tq, 2 * tq), 0)
    qi = lax.broadcasted_iota(jnp.int32, (tq, 2 * tq), 1)
    qi = jnp.where(qi >= tq, qi - tq, qi)
    pt_ref[0] = _rel_bias_tile(tab_ref, head, qi + tq, ki + tq)
    pt_ref[1] = _rel_bias_tile(tab_ref, head, qi + tq, ki)
    wk = past + LANES
    qs = lax.broadcasted_iota(jnp.int32, (ts, wk), 0) + past
    ks = lax.broadcasted_iota(jnp.int32, (ts, wk), 1)
    st = _rel_bias_tile(tab_ref, head, qs, ks)
    st_ref[...] = jnp.where(ks < past + ts, st, NEG)


def _bias_call(rel_bias, tq, past, ts):
    h = rel_bias.shape[1]
    wk = past + LANES
    return pl.pallas_call(
        functools.partial(_bias_kernel, tq=tq, past=past, ts=ts),
        out_shape=[jax.ShapeDtypeStruct((h, 2, tq, 2 * tq), F32),
                   jax.ShapeDtypeStruct((h, ts, wk), F32)],
        grid=(h,),
        in_specs=[pl.BlockSpec(memory_space=pltpu.SMEM)],
        out_specs=[pl.BlockSpec((None, 2, tq, 2 * tq), lambda i: (i, 0, 0, 0)),
                   pl.BlockSpec((None, ts, wk), lambda i: (i, 0, 0))],
        compiler_params=_cparams(("arbitrary",)),
        name="rel_bias",
    )(rel_bias)


def _lam_value(lam_ref, lam_init):
    lp = lam_ref[...]
    s1 = jnp.sum(lp[0:1, :] * lp[1:2, :], axis=-1, keepdims=True)
    s2 = jnp.sum(lp[2:3, :] * lp[3:4, :], axis=-1, keepdims=True)
    return jnp.exp(s1) - jnp.exp(s2) + lam_init


def _stack_maps(q):
    lane = lax.broadcasted_iota(jnp.int32, q.shape, 1)
    zero = jnp.zeros_like(q)
    half = HEAD // 2
    return jnp.concatenate([jnp.where(lane < half, q, zero), jnp.where(lane >= half, q, zero)], axis=0)


def _merge_out(o, lam_init, subln, ma, gb):
    o = o * lax.rsqrt(jnp.mean(o * o, axis=-1, keepdims=True) + LN_EPS) * subln * (1.0 - lam_init)
    return (ma + _sigmoid(gb.astype(F32)) * o).astype(BF16)


def _attn_block(n_blocks, q2, kbf, vt, bias_ref, tq):
    kv = n_blocks * tq
    pieces = []
    if n_blocks > 2:
        pieces.append((0, kv - 2 * tq, None))
    if n_blocks >= 2:
        pieces.append((kv - 2 * tq, tq, 1))
    pieces.append((kv - tq, tq, 0))
    scores = []
    for start, size, tile in pieces:
        s = _dot_nt(kbf[start:start + size, :], q2)
        if tile is not None:
            s = s + bias_ref[tile]
        scores.append(s)
    m = None
    for s in scores:
        mx = jnp.max(s, axis=0, keepdims=True)
        m = mx if m is None else jnp.maximum(m, mx)
    l = None
    acc = None
    for (start, size, _), s in zip(pieces, scores):
        p = jnp.exp2(s - m)
        ls = jnp.sum(p, axis=0, keepdims=True)
        pv = _dot(vt[:, start:start + size], p.astype(BF16))
        l = ls if l is None else l + ls
        acc = pv if acc is None else acc + pv
    return acc, l


def _attn_kernel(q_ref, k_ref, v_ref, bias_ref, lam_ref, subln_ref, ma_ref, gb_ref,
                 out_ref, vt, *, tq, nq, lam_init):
    vt[...] = v_ref[...].T
    lam_val = _lam_value(lam_ref, lam_init)
    subln = subln_ref[...]

    for n in range(1, nq + 1):
        rows = slice((n - 1) * tq, n * tq)
        q2 = _stack_maps(q_ref[rows, :])
        acc, l = _attn_block(n, q2, k_ref, vt, bias_ref, tq)
        o_t = acc[:, 0:tq] / l[:, 0:tq] - lam_val * (acc[:, tq:2 * tq] / l[:, tq:2 * tq])
        out_ref[rows, :] = _merge_out(o_t.T, lam_init, subln, ma_ref[rows, :], gb_ref[rows, :])


def _attn_call(q, k, v, bias, lam, subln, ma, main, layer, nb, t, tq, lam_init):
    n, d = q.shape
    h = d // HEAD
    nq = t // tq
    assert tq + 1 >= _far_distance() and tq % CHUNK == 0
    gb_col = (N_MAIN_TILES - 1) * h
    col = pl.BlockSpec((t, HEAD), lambda b, hh: (b, hh))
    in_specs = [
        col, col, col,
        pl.BlockSpec((None, 2, tq, 2 * tq), lambda b, hh: (hh, 0, 0, 0)),
        pl.BlockSpec((None, 4, HEAD // 2), lambda b, hh: (layer, 0, 0)),
        pl.BlockSpec((None, 1, HEAD), lambda b, hh: (layer, 0, 0)),
        col,
        pl.BlockSpec((t, HEAD), lambda b, hh: (b, gb_col + hh)),
    ]
    return pl.pallas_call(
        functools.partial(_attn_kernel, tq=tq, nq=nq, lam_init=lam_init),
        out_shape=jax.ShapeDtypeStruct((n, d), BF16),
        grid=(nb, h),
        in_specs=in_specs,
        out_specs=col,
        scratch_shapes=[pltpu.VMEM((HEAD, t), BF16)],
        compiler_params=_cparams(("parallel", "parallel")),
        name="attn",
    )(q, k, v, bias, lam, subln, ma, main)


def _attn_step_kernel(q_ref, kc_ref, vc_ref, kn_ref, vn_ref, bias_ref, lam_ref, subln_ref,
                      ma_ref, gb_ref, out_ref, *, ts, past, lam_init):
    n_heads = kc_ref.shape[1]
    pad = jnp.zeros((LANES - ts, HEAD), BF16)
    lam_val = _lam_value(lam_ref, lam_init)
    subln = subln_ref[...]
    for hh in range(n_heads):
        cs = slice(hh * HEAD, (hh + 1) * HEAD)
        q2 = _stack_maps(q_ref[:, cs])
        kc = kc_ref[:, hh, :].astype(BF16)
        vc = vc_ref[:, hh, :].astype(BF16)
        kn = jnp.concatenate([kn_ref[:, cs], pad], axis=0)
        vn = jnp.concatenate([vn_ref[:, cs], pad], axis=0)
        bias = bias_ref[hh]
        bias2 = jnp.concatenate([bias, bias], axis=0)
        s_c = _dot_nt(q2, kc) + bias2[:, 0:past]
        s_n = _dot_nt(q2, kn) + bias2[:, past:past + LANES]
        m = jnp.maximum(jnp.max(s_c, axis=-1, keepdims=True), jnp.max(s_n, axis=-1, keepdims=True))
        p_c = jnp.exp2(s_c - m)
        p_n = jnp.exp2(s_n - m)
        l = jnp.sum(p_c, axis=-1, keepdims=True) + jnp.sum(p_n, axis=-1, keepdims=True)
        acc = _dot(p_c.astype(BF16), vc) + _dot(p_n.astype(BF16), vn)
        o = acc[0:ts] / l[0:ts] - lam_val * (acc[ts:2 * ts] / l[ts:2 * ts])
        out_ref[:, cs] = _merge_out(o, lam_init, subln, ma_ref[:, cs], gb_ref[:, cs])


def _attn_step_call(q, cache_k, cache_v, k_new, v_new, bias, lam, subln, ma, main, layer, ts, lam_init):
    n, d = q.shape
    _, nb, past, h, _ = cache_k.shape
    wk = past + LANES
    row = pl.BlockSpec((ts, d), lambda b: (b, 0))
    cache = pl.BlockSpec((None, None, past, h, HEAD), lambda b: (layer, b, 0, 0, 0))
    in_specs = [
        row, cache, cache, row, row,
        _resident((h, ts, wk), lambda b: (0, 0, 0)),
        pl.BlockSpec((None, 4, HEAD // 2), lambda b: (layer, 0, 0)),
        pl.BlockSpec((None, 1, HEAD), lambda b: (layer, 0, 0)),
        row,
        pl.BlockSpec((ts, d), lambda b: (b, N_MAIN_TILES - 1)),
    ]
    return pl.pallas_call(
        functools.partial(_attn_step_kernel, ts=ts, past=past, lam_init=lam_init),
        out_shape=jax.ShapeDtypeStruct((n, d), BF16),
        grid=(nb,),
        in_specs=in_specs,
        out_specs=row,
        compiler_params=_cparams(("parallel",)),
        name="attn_step",
    )(q, cache_k, cache_v, k_new, v_new, bias, lam, subln, ma, main)


def _wo_ffn_kernel(m_ref, x_ref, gt1_ref, sc_ref, sh_ref, gt2_ref, w_o_ref, wi_ref, wo_ref,
                   g1_ref, b1_ref, g2_ref, b2_ref, o_ref, *, alpha, nf):
    y = _dot(m_ref[...], w_o_ref[...])
    x = _layer_norm(alpha * x_ref[...] + gt1_ref[...] * y, g1_ref[...], b1_ref[...])
    h = (x * (1.0 + sc_ref[...]) + sh_ref[...]).astype(BF16)
    dff = wo_ref.shape[0]
    tf = dff // nf
    y = None
    for j in range(nf):
        u = _dot(h, wi_ref[:, j * tf:(j + 1) * tf])
        v = _dot(h, wi_ref[:, dff + j * tf:dff + (j + 1) * tf])
        part = _dot((_silu(u) * v).astype(BF16), wo_ref[j * tf:(j + 1) * tf, :])
        y = part if y is None else y + part
    o_ref[...] = _layer_norm(alpha * x + gt2_ref[...] * y, g2_ref[...], b2_ref[...])


def _wo_ffn_call(merged, x, mod, w_o, w_ff_in, w_ff_out, ln1_g, ln1_b, ln2_g, ln2_b, layer, tm,
                 tiles_per_group, alpha, nf):
    n, d = x.shape
    dff = w_ff_out.shape[1]
    vec = pl.BlockSpec((None, 1, d), lambda i: (layer, 0, 0))
    rows = pl.BlockSpec((tm, d), lambda i: (i, 0))
    return pl.pallas_call(
        functools.partial(_wo_ffn_kernel, alpha=alpha, nf=nf),
        out_shape=jax.ShapeDtypeStruct((n, d), F32),
        grid=(n // tm,),
        in_specs=[rows, rows,
                  _mod_spec(mod, d, tiles_per_group, 2),
                  _mod_spec(mod, d, tiles_per_group, 4),
                  _mod_spec(mod, d, tiles_per_group, 3),
                  _mod_spec(mod, d, tiles_per_group, 5),
                  _resident((None, d, d), lambda i: (layer, 0, 0)),
                  _resident((None, d, 2 * dff), lambda i: (layer, 0, 0)),
                  _resident((None, dff, d), lambda i: (layer, 0, 0)),
                  vec, vec, vec, vec],
        out_specs=rows,
        compiler_params=_cparams(("parallel",)),
        name="wo_ffn",
    )(merged, x, mod, mod, mod, mod, w_o, w_ff_in, w_ff_out, ln1_g, ln1_b, ln2_g, ln2_b)


def _pick_tile(n, pref):
    t = min(n, pref)
    while n % t:
        t //= 2
    return t


def kernel(x_prompt, x_sample, cache_k, cache_v, state_conv, state_delta, c_prompt, c_sample,
           ln_in_g, ln_in_b, rel_bias, w_ada, b_ada, w_in, conv_w, a_log, dt_bias, norm_a,
           lam, subln_g, w_o, ln1_g, ln1_b, w_ff_in, w_ff_out, ln2_g, ln2_b):
    bp, tp, d = x_prompt.shape
    bs, ts, _ = x_sample.shape
    depth = w_in.shape[0]
    h = d // HEAD
    past = cache_k.shape[2]
    dff = w_ff_out.shape[1]
    n_p = bp * tp
    n_s = bs * ts
    alpha = (2 * depth) ** 0.25
    hb = h
    ng = h // hb

    o1 = 4 * d + 2 * h
    qkvz = w_in[:, :, 0:4 * d]
    gates = w_in[:, :, o1 + 3 * d:o1 + 5 * d]
    qb = w_in[:, :, o1:o1 + d] * ((HEAD // 2) ** -0.5 * LOG2E)
    kvb = w_in[:, :, o1 + d:o1 + 3 * d]
    w_all = jnp.concatenate([qkvz, gates, qb, kvb], axis=-1).astype(BF16)
    wb = w_in[:, :, 4 * d:4 * d + h].reshape(depth, d, ng, hb)
    wa = w_in[:, :, 4 * d + h:4 * d + 2 * h].reshape(depth, d, ng, hb)
    w_ba = jnp.concatenate([wb, wa, jnp.zeros((depth, d, ng, LANES - 2 * hb), F32)], axis=-1)
    w_ba = w_ba.reshape(depth, d, ng * LANES).astype(BF16)
    w_ada_b = w_ada.astype(BF16)
    w_o_b = w_o.astype(BF16)
    w_ff_in_b = w_ff_in.astype(BF16)
    w_ff_out_b = w_ff_out.astype(BF16)

    def lane_rows(v):
        v = v.reshape(depth, ng, hb)
        z = jnp.zeros((depth, ng, hb), F32)
        zz = jnp.zeros((depth, ng, LANES - 2 * hb), F32)
        return jnp.concatenate([z, v, zz], axis=-1).reshape(depth, ng * LANES)

    par = jnp.zeros((depth, 8, ng * LANES), F32)
    par = par.at[:, 0].set(lane_rows(dt_bias)).at[:, 1].set(lane_rows(a_log))
    par = par.at[:, 2].set(jnp.tile(norm_a, (1, ng)))
    subln = subln_g.reshape(depth, 1, HEAD)
    ln1g, ln1b = ln1_g.reshape(depth, 1, d), ln1_b.reshape(depth, 1, d)
    ln2g, ln2b = ln2_g.reshape(depth, 1, d), ln2_b.reshape(depth, 1, d)

    mod = _ada_call(jnp.concatenate([c_prompt, c_sample], axis=0), w_ada_b, b_ada)
    mod_p = mod[:, :bp].reshape(depth, bp, 1, 6 * d)
    mod_s = jnp.repeat(mod[:, bp:], ts, axis=1).reshape(depth, 1, n_s, 6 * d)

    bias_p, bias_s = _bias_call(rel_bias, TQ, past, ts)

    tm_p = _pick_tile(tp, TM_PROJ)
    tpg_p = tp // tm_p
    xp = _ln_call(x_prompt.reshape(n_p, d), ln_in_g, ln_in_b, tm_p)
    xs = _ln_call(x_sample.reshape(n_s, d), ln_in_g, ln_in_b, n_s)

    conv0 = jnp.zeros((bp, CONV_W - 1, 3 * d), F32)
    s0 = jnp.zeros((bp, h, HEAD, HEAD), F32)
    nf = 2 if dff % (2 * LANES) == 0 else 1

    kp, vp = (jnp.zeros((depth, n_p, h, HEAD), F32) for _ in range(2))
    ks, vs = (jnp.zeros((depth, n_s, h, HEAD), F32) for _ in range(2))
    conv_p, conv_s, st_p, st_s = [], [], [], []
    for l in range(depth):
        lam_init = 0.8 - 0.6 * math.exp(-0.3 * l)
        main, qb_p, kb_l, vb_l, kp, vp, ba, tail = _inproj_call(
            xp, mod_p[l], w_all, w_ba, conv_w, conv0, kp, vp, l, tm_p, tpg_p, BF16, True)
        ma, sn = _delta_call(main, ba, conv_w, par, conv0, s0, l, bp, tp, CHUNK, DELTA_CPS, hb, True)
        merged = _attn_call(qb_p, kb_l, vb_l, bias_p, lam, subln, ma, main, l, bp, tp, TQ, lam_init)
        xp = _wo_ffn_call(merged, xp, mod_p[l], w_o_b, w_ff_in_b, w_ff_out_b, ln1g, ln1b, ln2g, ln2b,
                          l, tm_p, tpg_p, alpha, nf)
        conv_p.append(tail.reshape(bp, tpg_p, CONV_PAD, 3 * d)[:, tpg_p - 1, CONV_PAD - (CONV_W - 1):, :])
        st_p.append(sn)
        main, qb_s, kb_l, vb_l, ks, vs, ba, _ = _inproj_call(
            xs, mod_s[l], w_all, w_ba, conv_w, state_conv[l], ks, vs, l, n_s, 1, F32, False)
        ma, sn = _delta_call(main, ba, conv_w, par, state_conv[l], state_delta[l], l, bs, ts, ts, 1, hb, False)
        merged = _attn_step_call(qb_s, cache_k, cache_v, kb_l, vb_l, bias_s, lam, subln, ma, main,
                                 l, ts, lam_init)
        xs = _wo_ffn_call(merged, xs, mod_s[l], w_o_b, w_ff_in_b, w_ff_out_b, ln1g, ln1b, ln2g, ln2b,
                          l, n_s, 1, alpha, nf)
        conv_s.append(main.reshape(bs, ts, -1)[:, ts - (CONV_W - 1):, 0:3 * d])
        st_s.append(sn)

    def heads_out(buf, nb, t):
        return buf.reshape(depth, nb, t, h, HEAD)

    return (xp.reshape(bp, tp, d), xs.reshape(bs, ts, d),
            heads_out(kp, bp, tp), heads_out(vp, bp, tp), jnp.stack(conv_p), jnp.stack(st_p),
            heads_out(ks, bs, ts), heads_out(vs, bs, ts), jnp.stack(conv_s), jnp.stack(st_s))
```

```python
import functools
import math

import jax
import jax.numpy as jnp
from jax import lax
from jax.experimental import pallas as pl
from jax.experimental.pallas import tpu as pltpu

F32 = jnp.float32
BF16 = jnp.bfloat16
HIGHEST = lax.Precision.HIGHEST

LN_EPS = 1e-5
CHUNK = 64
HEAD = 128
CONV_W = 4
N_BUCKETS = 32
MAX_DIST = 128
NEG = -1e30
LOG2E = math.log2(math.e)
LANES = 128
CONV_PAD = 8
VMEM_LIMIT = 56 * 1024 * 1024

TM_PROJ = 512
TQ = 256
DELTA_CPS = 2


def _cparams(sem):
    return pltpu.CompilerParams(dimension_semantics=sem, vmem_limit_bytes=VMEM_LIMIT)


def _sigmoid(x):
    return 1.0 / (1.0 + jnp.exp(-x))


def _silu(x):
    return x * _sigmoid(x)


def _layer_norm(x, g, b):
    mu = jnp.mean(x, axis=-1, keepdims=True)
    xc = x - mu
    var = jnp.mean(xc * xc, axis=-1, keepdims=True)
    return xc * lax.rsqrt(var + LN_EPS) * g + b


def _dot(a, b):
    return jnp.dot(a, b, preferred_element_type=F32)


def _hdot(a, b):
    return jnp.dot(a, b, preferred_element_type=F32, precision=HIGHEST)


def _bdot(a, b):
    return _dot(a.astype(BF16), b.astype(BF16))


def _dot_nt(a, b):
    return lax.dot_general(a, b, (((1,), (1,)), ((), ())), preferred_element_type=F32)


def _dot_tn(a, b):
    return lax.dot_general(a, b, (((0,), (0,)), ((), ())), preferred_element_type=F32)


def _ln_kernel(x_ref, g_ref, b_ref, o_ref):
    o_ref[...] = _layer_norm(x_ref[...], g_ref[...], b_ref[...])


def _ln_call(x, g, b, tm):
    n, d = x.shape
    return pl.pallas_call(
        _ln_kernel,
        out_shape=jax.ShapeDtypeStruct((n, d), F32),
        grid=(n // tm,),
        in_specs=[pl.BlockSpec((tm, d), lambda i: (i, 0)),
                  pl.BlockSpec((1, d), lambda i: (0, 0)),
                  pl.BlockSpec((1, d), lambda i: (0, 0))],
        out_specs=pl.BlockSpec((tm, d), lambda i: (i, 0)),
        compiler_params=_cparams(("parallel",)),
        name="ln_in",
    )(x, g.reshape(1, d), b.reshape(1, d))


def _ada_kernel(c_ref, w_ref, b_ref, o_ref):
    c = c_ref[...]
    o_ref[...] = _dot(_silu(c).astype(BF16), w_ref[...]) + b_ref[...]


def _ada_call(c_all, w_ada, b_ada):
    depth, d, d6 = w_ada.shape
    nb = c_all.shape[0]
    return pl.pallas_call(
        _ada_kernel,
        out_shape=jax.ShapeDtypeStruct((depth, nb, d6), F32),
        grid=(depth, d6 // d),
        in_specs=[pl.BlockSpec((nb, d), lambda l, j: (0, 0)),
                  pl.BlockSpec((None, d, d), lambda l, j: (l, 0, j)),
                  pl.BlockSpec((None, 1, d), lambda l, j: (l, 0, j))],
        out_specs=pl.BlockSpec((None, nb, d), lambda l, j: (l, 0, j)),
        compiler_params=_cparams(("parallel", "parallel")),
        name="ada",
    )(c_all, w_ada, b_ada.reshape(depth, 1, d6))


def _mod_spec(mod, d, tiles_per_group, chunk):
    rows = mod.shape[1]
    return pl.BlockSpec((None, rows, d), lambda i, *_: (i // tiles_per_group, 0, chunk))


N_MAIN_TILES = 6
N_W_TILES = 9
N_CONV_TILES = 3


def _conv_taps(cat_ref, cw, rows):
    acc = None
    for i in range(CONV_W):
        term = cat_ref[pl.ds(CONV_PAD - (CONV_W - 1) + i, rows), :] * cw[i:i + 1, :]
        acc = term if acc is None else acc + term
    return acc


def _conv_taps_rolled(prev, cur, cw):
    rows = cur.shape[0]
    cat = jnp.concatenate([prev, cur], axis=0)
    acc = cur * cw[CONV_W - 1:CONV_W, :]
    for back in range(1, CONV_W):
        shifted = pltpu.roll(cat, back, axis=0)[CONV_PAD:CONV_PAD + rows, :]
        acc = acc + shifted * cw[CONV_W - 1 - back:CONV_W - back, :]
    return acc


def _l2norm_heads(y, scale):
    outs = []
    for hh in range(y.shape[1] // HEAD):
        seg = y[:, hh * HEAD:(hh + 1) * HEAD]
        outs.append(seg * (lax.rsqrt(jnp.sum(seg * seg, axis=-1, keepdims=True) + 1e-6) * scale))
    return jnp.concatenate(outs, axis=1)


def _inproj_kernel(x_ref, sc_ref, sh_ref, w_ref, wba_ref, cw_ref, cb_ref, kin_ref, vin_ref,
                   main_ref, q_ref, kb_ref, vb_ref, k_ref, v_ref, ba_ref, tail_ref,
                   carry_scr, *, fuse_conv, tiles_per_group):
    tm, d = x_ref.shape
    h = (x_ref[...] * (1.0 + sc_ref[...]) + sh_ref[...]).astype(BF16)

    def cols(j):
        return _dot(h, w_ref[:, j * d:(j + 1) * d])

    if fuse_conv:
        @pl.when(pl.program_id(0) % tiles_per_group == 0)
        def _():
            for j in range(N_CONV_TILES):
                carry_scr[j, CONV_PAD - (CONV_W - 1):CONV_PAD, :] = cb_ref[:, j * d:(j + 1) * d]

    def finish(j, res):
        if j < N_CONV_TILES:
            last = res[tm - CONV_PAD:tm, :]
            tail_ref[:, j * d:(j + 1) * d] = last
            if fuse_conv:
                prev = carry_scr[j]
                carry_scr[j] = last
                res = _silu(_conv_taps_rolled(prev, res, cw_ref[:, j * d:(j + 1) * d]))
                if j == 0:
                    res = _l2norm_heads(res, HEAD ** -0.5)
                elif j == 1:
                    res = _l2norm_heads(res, 1.0)
        if j < N_MAIN_TILES:
            main_ref[:, j * d:(j + 1) * d] = res.astype(main_ref.dtype)
        elif j == N_MAIN_TILES:
            q_ref[...] = res.astype(BF16)
        else:
            b_ref, o_ref = ((kb_ref, k_ref), (vb_ref, v_ref))[j - N_MAIN_TILES - 1]
            b_ref[...] = res.astype(BF16)
            o_ref[...] = res.reshape(o_ref.shape)

    order = []
    for j in range(N_CONV_TILES):
        order += [j, N_CONV_TILES + j]
    order += list(range(2 * N_CONV_TILES, N_W_TILES))
    nxt = cols(order[0])
    for pos, j in enumerate(order):
        res = nxt
        if pos + 1 < len(order):
            nxt = cols(order[pos + 1])
        finish(j, res)
    ba_ref[...] = _dot(h, wba_ref[...])


def _resident(block_shape, index_map):
    return pl.BlockSpec(block_shape, index_map, pipeline_mode=pl.Buffered(1))


def _inproj_call(x, mod, w_all, w_ba, conv_w, conv_buf, kbuf, vbuf, layer, tm, tiles_per_group,
                 main_dtype, fuse_conv):
    n, d = x.shape
    nba = w_ba.shape[-1]
    wide = N_W_TILES * d
    cwide = N_CONV_TILES * d
    in_specs = [
        pl.BlockSpec((tm, d), lambda i: (i, 0)),
        _mod_spec(mod, d, tiles_per_group, 1),
        _mod_spec(mod, d, tiles_per_group, 0),
        _resident((None, d, wide), lambda i: (layer, 0, 0)),
        _resident((None, d, nba), lambda i: (layer, 0, 0)),
        _resident((None, CONV_W, cwide), lambda i: (layer, 0, 0)),
        pl.BlockSpec((None, CONV_W - 1, cwide), lambda i: (i // tiles_per_group if fuse_conv else 0, 0, 0)),
        pl.BlockSpec(memory_space=pl.ANY),
        pl.BlockSpec(memory_space=pl.ANY),
    ]
    out_shape = [
        jax.ShapeDtypeStruct((n, N_MAIN_TILES * d), main_dtype),
        jax.ShapeDtypeStruct((n, d), BF16),
        jax.ShapeDtypeStruct((n, d), BF16),
        jax.ShapeDtypeStruct((n, d), BF16),
        jax.ShapeDtypeStruct(kbuf.shape, F32),
        jax.ShapeDtypeStruct(vbuf.shape, F32),
        jax.ShapeDtypeStruct((n, nba), F32),
        jax.ShapeDtypeStruct((n // tm, CONV_PAD, N_CONV_TILES * d), F32),
    ]
    out_specs = [
        pl.BlockSpec((tm, N_MAIN_TILES * d), lambda i: (i, 0)),
        pl.BlockSpec((tm, d), lambda i: (i, 0)),
        pl.BlockSpec((tm, d), lambda i: (i, 0)),
        pl.BlockSpec((tm, d), lambda i: (i, 0)),
        pl.BlockSpec((None, tm, d // HEAD, HEAD), lambda i: (layer, i, 0, 0)),
        pl.BlockSpec((None, tm, d // HEAD, HEAD), lambda i: (layer, i, 0, 0)),
        pl.BlockSpec((tm, nba), lambda i: (i, 0)),
        pl.BlockSpec((None, CONV_PAD, N_CONV_TILES * d), lambda i: (i, 0, 0)),
    ]
    return pl.pallas_call(
        functools.partial(_inproj_kernel, fuse_conv=fuse_conv, tiles_per_group=tiles_per_group),
        out_shape=out_shape,
        grid=(n // tm,),
        in_specs=in_specs,
        out_specs=out_specs,
        scratch_shapes=[pltpu.VMEM((N_CONV_TILES, CONV_PAD, d), F32)],
        input_output_aliases={7: 4, 8: 5},
        compiler_params=_cparams(("arbitrary",)),
        name="inproj",
    )(x, mod, mod, w_all, w_ba, conv_w, conv_buf, kbuf, vbuf)


def _tri_inverse(a_list, eye, blk, size):
    s = 8
    d = [jnp.where(blk[s], a, 0.0) for a in a_list]
    d2 = [_bdot(x, x) for x in d]
    p = [_bdot(eye - x, eye + y) for x, y in zip(d, d2)]
    d4 = [_bdot(x, x) for x in d2]
    t = [_bdot(x, eye + y) for x, y in zip(p, d4)]
    while s < size:
        lower = jnp.logical_and(blk[2 * s], jnp.logical_not(blk[s]))
        off = [jnp.where(lower, a, 0.0) for a in a_list]
        x = [_bdot(ti, oi) for ti, oi in zip(t, off)]
        t = [ti - _bdot(xi, ti) for ti, xi in zip(t, x)]
        s *= 2
    return t


def _delta_kernel(qu_ref, ku_ref, vu_ref, z_ref, ga_ref, ba_ref,
                  cwq_ref, cwk_ref, cwv_ref, par_ref, cbq_ref, cbk_ref, cbv_ref, s0_ref,
                  ma_ref, sn_ref, s_scr, cat_scr, *, chunk, cps, hb, pre_conv):
    c = pl.program_id(2)
    rows_blk = chunk * cps
    tail = CONV_W - 1

    @pl.when(c == 0)
    def _():
        s_scr[...] = s0_ref[...]
        if not pre_conv:
            for s, cb in enumerate((cbq_ref, cbk_ref, cbv_ref)):
                cat_scr[s, CONV_PAD - tail:CONV_PAD, :] = cb[...]

    if pre_conv:
        convs = [u_ref[...].astype(F32) for u_ref in (qu_ref, ku_ref, vu_ref)]
    else:
        convs = []
        for s, (u_ref, cw_ref) in enumerate(((qu_ref, cwq_ref), (ku_ref, cwk_ref), (vu_ref, cwv_ref))):
            cat_scr[s, CONV_PAD:CONV_PAD + rows_blk, :] = u_ref[...].astype(F32)
            y = _silu(_conv_taps(cat_scr.at[s], cw_ref[...], rows_blk))
            convs.append(y if s == 2 else _l2norm_heads(y, HEAD ** -0.5 if s == 0 else 1.0))
            cat_scr[s, CONV_PAD - tail:CONV_PAD, :] = cat_scr[s, CONV_PAD + rows_blk - tail:CONV_PAD + rows_blk, :]

    ba = ba_ref[...]
    dt_row = par_ref[0:1, :]
    alog_row = par_ref[1:2, :]
    norm_row = par_ref[2:3, 0:HEAD]
    beta_all = _sigmoid(ba)
    xa = ba + dt_row
    softplus = jnp.maximum(xa, 0.0) + jnp.log(1.0 + jnp.exp(-jnp.abs(xa)))
    g_all = -jnp.exp(alog_row) * softplus

    r = lax.broadcasted_iota(jnp.int32, (chunk, chunk), 0)
    cc = lax.broadcasted_iota(jnp.int32, (chunk, chunk), 1)
    causal = r >= cc
    strict = r > cc
    eye = jnp.where(r == cc, 1.0, 0.0).astype(F32)
    tril = jnp.where(causal, 1.0, 0.0).astype(F32)
    blk = {}
    s = 8
    while s <= chunk:
        blk[s] = jnp.bitwise_xor(r, cc) < s
        s *= 2

    heads = range(hb)
    pairs = [(ci, hh) for ci in range(cps) for hh in heads]
    rsl = [slice(ci * chunk, (ci + 1) * chunk) for ci in range(cps)]
    csl = [slice(hh * HEAD, (hh + 1) * HEAD) for hh in heads]
    gcum_all = [_hdot(tril, g_all[rs, :]) for rs in rsl]
    gcum_t = [g.T for g in gcum_all]

    q, k, v, beta, gcum, g_last, decay = [], [], [], [], [], [], []
    for ci, hh in pairs:
        rs, cs = rsl[ci], csl[hh]
        q.append(convs[0][rs, cs])
        k.append(convs[1][rs, cs])
        v.append(convs[2][rs, cs])
        beta.append(beta_all[rs, hh:hh + 1])
        gc = gcum_all[ci][:, hb + hh:hb + hh + 1]
        gr = gcum_t[ci][hb + hh:hb + hh + 1, :]
        gcum.append(gc)
        g_last.append(gc[chunk - 1:chunk, :])
        decay.append(jnp.where(causal, jnp.exp(jnp.where(causal, gc - gr, 0.0)), 0.0))
    idx = range(len(pairs))
    e_g = [jnp.exp(gcum[i]) for i in idx]
    kb = [k[i] * beta[i] for i in idx]
    k_b16 = [k[i].astype(BF16) for i in idx]
    kk_mat = [_dot_nt(kb[i].astype(BF16), k_b16[i]) for i in idx]
    qk = [_dot_nt(q[i].astype(BF16), k_b16[i]) * decay[i] for i in idx]
    a_mat = [jnp.where(strict, kk_mat[i] * decay[i], 0.0) for i in idx]
    t_inv = _tri_inverse(a_mat, eye, blk, chunk)
    uw = [_bdot(t_inv[i], jnp.concatenate([v[i] * beta[i], kb[i] * e_g[i]], axis=1)) for i in idx]
    q_g = [(q[i] * e_g[i]).astype(BF16) for i in idx]
    k_tail = [(k[i] * jnp.exp(g_last[i] - gcum[i])).astype(BF16) for i in idx]
    qk = [x.astype(BF16) for x in qk]

    for ci in range(cps):
        ids = [ci * hb + hh for hh in heads]
        state = [s_scr[hh] for hh in heads]
        state_b = [x.astype(BF16) for x in state]
        ws = [_dot(uw[i][:, HEAD:2 * HEAD].astype(BF16), state_b[hh]) for hh, i in zip(heads, ids)]
        u = [uw[i][:, 0:HEAD] - ws[hh] for hh, i in zip(heads, ids)]
        u_b = [x.astype(BF16) for x in u]
        qs = [_dot(q_g[i], state_b[hh]) for hh, i in zip(heads, ids)]
        ku = [_dot_tn(k_tail[i], u_b[hh]) for hh, i in zip(heads, ids)]
        qu = [_dot(qk[i], u_b[hh]) for hh, i in zip(heads, ids)]
        for hh, i in zip(heads, ids):
            s_scr[hh] = jnp.exp(g_last[i]) * state[hh] + ku[hh]
        for hh in heads:
            rs, cs = rsl[ci], csl[hh]
            o = qs[hh] + qu[hh]
            o = o * lax.rsqrt(jnp.mean(o * o, axis=-1, keepdims=True) + LN_EPS) * norm_row
            o = o * _silu(z_ref[rs, cs].astype(F32))
            ma_ref[rs, cs] = _sigmoid(ga_ref[rs, cs].astype(F32)) * o

    @pl.when(c == pl.num_programs(2) - 1)
    def _():
        sn_ref[...] = s_scr[...]


def _delta_call(main, ba, conv_w, par, conv_buf, s0, layer, nb, t, chunk, cps, hb, pre_conv):
    n = main.shape[0]
    d = main.shape[1] // N_MAIN_TILES
    h = d // HEAD
    ng = h // hb
    wb = hb * HEAD
    rows_blk = chunk * cps
    ncb = t // rows_blk
    grid = (nb, ng, ncb)

    def seg(s):
        return pl.BlockSpec((rows_blk, wb), lambda b, g, c: (b * ncb + c, s * ng + g))

    def cw(s):
        return pl.BlockSpec((None, CONV_W, wb), lambda b, g, c: (layer, 0, s * ng + g))

    def cb(s):
        return pl.BlockSpec((None, CONV_W - 1, wb), lambda b, g, c: (b, 0, s * ng + g))

    in_specs = [seg(0), seg(1), seg(2), seg(3), seg(4),
                pl.BlockSpec((rows_blk, LANES), lambda b, g, c: (b * ncb + c, g)),
                cw(0), cw(1), cw(2),
                pl.BlockSpec((None, 8, LANES), lambda b, g, c: (layer, 0, g)),
                cb(0), cb(1), cb(2),
                pl.BlockSpec((None, hb, HEAD, HEAD), lambda b, g, c: (b, g, 0, 0))]
    out_shape = [jax.ShapeDtypeStruct((n, d), F32),
                 jax.ShapeDtypeStruct((nb, h, HEAD, HEAD), F32)]
    out_specs = [pl.BlockSpec((rows_blk, wb), lambda b, g, c: (b * ncb + c, g)),
                 pl.BlockSpec((None, hb, HEAD, HEAD), lambda b, g, c: (b, g, 0, 0))]
    return pl.pallas_call(
        functools.partial(_delta_kernel, chunk=chunk, cps=cps, hb=hb, pre_conv=pre_conv),
        out_shape=out_shape,
        grid=grid,
        in_specs=in_specs,
        out_specs=out_specs,
        scratch_shapes=[pltpu.VMEM((hb, HEAD, HEAD), F32),
                        pltpu.VMEM((3, CONV_PAD + rows_blk, wb), F32)],
        compiler_params=_cparams(("parallel", "parallel", "arbitrary")),
        name="delta",
    )(main, main, main, main, main, ba, conv_w, conv_w, conv_w, par,
      conv_buf, conv_buf, conv_buf, s0)


def _bucket_thresholds():
    nb = N_BUCKETS // 2
    max_exact = nb // 2
    ratio = MAX_DIST // max_exact
    steps = nb - max_exact
    out = []
    for kk in range(1, steps):
        n = max_exact
        while n ** steps < (max_exact ** steps) * (ratio ** kk):
            n += 1
        out.append(n)
    return out


def _far_distance():
    return _bucket_thresholds()[-1]


def _rel_bias_tile(tab_ref, head, qpos, kpos):
    nb = N_BUCKETS // 2
    max_exact = nb // 2
    rel = kpos - qpos
    n = jnp.abs(rel)
    large = jnp.full(rel.shape, max_exact, jnp.int32)
    for thr in _bucket_thresholds():
        large = large + jnp.where(n >= thr, 1, 0)
    bucket = jnp.where(rel > 0, nb, 0) + jnp.where(n < max_exact, n, large)
    far = tab_ref[nb - 1, head]
    bias = jnp.zeros(rel.shape, F32)
    for b in range(N_BUCKETS):
        bias = jnp.where(bucket == b, (tab_ref[b, head] - far) * LOG2E, bias)
    shift = CHUNK.bit_length() - 1
    mask = lax.shift_right_logical(kpos, shift) <= lax.shift_right_logical(qpos, shift)
    return jnp.where(mask, bias, NEG)


def _bias_kernel(tab_ref, pt_ref, st_ref, *, tq, past, ts):
    head = pl.program_id(0)
    ki = lax.broadcasted_iota(jnp.int32, (tq, 2 * tq), 0)
    qi = lax.broadcasted_iota(jnp.int32, (tq, 2 * tq), 1)
    qi = jnp.where(qi >= tq, qi - tq, qi)
    pt_ref[0] = _rel_bias_tile(tab_ref, head, qi + tq, ki + tq)
    pt_ref[1] = _rel_bias_tile(tab_ref, head, qi + tq, ki)
    wk = past + LANES
    qs = lax.broadcasted_iota(jnp.int32, (ts, wk), 0) + past
    ks = lax.broadcasted_iota(jnp.int32, (ts, wk), 1)
    st = _rel_bias_tile(tab_ref, head, qs, ks)
    st_ref[...] = jnp.where(ks < past + ts, st, NEG)


def _bias_call(rel_bias, tq, past, ts):
    h = rel_bias.shape[1]
    wk = past + LANES
    return pl.pallas_call(
        functools.partial(_bias_kernel, tq=tq, past=past, ts=ts),
        out_shape=[jax.ShapeDtypeStruct((h, 2, tq, 2 * tq), F32),
                   jax.ShapeDtypeStruct((h, ts, wk), F32)],
        grid=(h,),
        in_specs=[pl.BlockSpec(memory_space=pltpu.SMEM)],
        out_specs=[pl.BlockSpec((None, 2, tq, 2 * tq), lambda i: (i, 0, 0, 0)),
                   pl.BlockSpec((None, ts, wk), lambda i: (i, 0, 0))],
        compiler_params=_cparams(("arbitrary",)),
        name="rel_bias",
    )(rel_bias)


def _lam_value(lam_ref, lam_init):
    lp = lam_ref[...]
    s1 = jnp.sum(lp[0:1, :] * lp[1:2, :], axis=-1, keepdims=True)
    s2 = jnp.sum(lp[2:3, :] * lp[3:4, :], axis=-1, keepdims=True)
    return jnp.exp(s1) - jnp.exp(s2) + lam_init


def _stack_maps(q):
    lane = lax.broadcasted_iota(jnp.int32, q.shape, 1)
    zero = jnp.zeros_like(q)
    half = HEAD // 2
    return jnp.concatenate([jnp.where(lane < half, q, zero), jnp.where(lane >= half, q, zero)], axis=0)


def _merge_out(o, lam_init, subln, ma, gb):
    o = o * lax.rsqrt(jnp.mean(o * o, axis=-1, keepdims=True) + LN_EPS) * subln * (1.0 - lam_init)
    return (ma + _sigmoid(gb.astype(F32)) * o).astype(BF16)


def _attn_scores(n_blocks, q2, k_ref, bias_ref, tq):
    kv = n_blocks * tq
    pieces = []
    if n_blocks > 2:
        pieces.append((0, kv - 2 * tq, None))
    if n_blocks >= 2:
        pieces.append((kv - 2 * tq, tq, 1))
    pieces.append((kv - tq, tq, 0))
    out = []
    for start, size, tile in pieces:
        s = _dot_nt(k_ref[start:start + size, :], q2)
        if tile is not None:
            s = s + bias_ref[tile]
        out.append((start, size, s))
    return out


def _attn_values(scored, vt):
    m = None
    for _, _, s in scored:
        mx = jnp.max(s, axis=0, keepdims=True)
        m = mx if m is None else jnp.maximum(m, mx)
    l = None
    acc = None
    for start, size, s in scored:
        p = jnp.exp2(s - m)
        ls = jnp.sum(p, axis=0, keepdims=True)
        pv = _dot(vt[:, start:start + size], p.astype(BF16))
        l = ls if l is None else l + ls
        acc = pv if acc is None else acc + pv
    return acc, l


def _attn_kernel(q_ref, k_ref, v_ref, bias_ref, lam_ref, subln_ref, ma_ref, gb_ref,
                 out_ref, vt, *, tq, nq, lam_init):
    vt[...] = v_ref[...].T
    lam_val = _lam_value(lam_ref, lam_init)
    subln = subln_ref[...]

    def scores(n):
        return _attn_scores(n, _stack_maps(q_ref[(n - 1) * tq:n * tq, :]), k_ref, bias_ref, tq)

    nxt = scores(1)
    for n in range(1, nq + 1):
        cur = nxt
        if n < nq:
            nxt = scores(n + 1)
        rows = slice((n - 1) * tq, n * tq)
        acc, l = _attn_values(cur, vt)
        o_t = acc[:, 0:tq] / l[:, 0:tq] - lam_val * (acc[:, tq:2 * tq] / l[:, tq:2 * tq])
        out_ref[rows, :] = _merge_out(o_t.T, lam_init, subln, ma_ref[rows, :], gb_ref[rows, :])


def _attn_call(q, k, v, bias, lam, subln, ma, main, layer, nb, t, tq, lam_init):
    n, d = q.shape
    h = d // HEAD
    nq = t // tq
    assert tq + 1 >= _far_distance() and tq % CHUNK == 0
    gb_col = (N_MAIN_TILES - 1) * h
    col = pl.BlockSpec((t, HEAD), lambda b, hh: (b, hh))
    in_specs = [
        col, col, col,
        pl.BlockSpec((None, 2, tq, 2 * tq), lambda b, hh: (hh, 0, 0, 0)),
        pl.BlockSpec((None, 4, HEAD // 2), lambda b, hh: (layer, 0, 0)),
        pl.BlockSpec((None, 1, HEAD), lambda b, hh: (layer, 0, 0)),
        col,
        pl.BlockSpec((t, HEAD), lambda b, hh: (b, gb_col + hh)),
    ]
    return pl.pallas_call(
        functools.partial(_attn_kernel, tq=tq, nq=nq, lam_init=lam_init),
        out_shape=jax.ShapeDtypeStruct((n, d), BF16),
        grid=(nb, h),
        in_specs=in_specs,
        out_specs=col,
        scratch_shapes=[pltpu.VMEM((HEAD, t), BF16)],
        compiler_params=_cparams(("parallel", "parallel")),
        name="attn",
    )(q, k, v, bias, lam, subln, ma, main)


def _attn_step_kernel(q_ref, kc_ref, vc_ref, kn_ref, vn_ref, bias_ref, lam_ref, subln_ref,
                      ma_ref, gb_ref, out_ref, *, ts, past, lam_init):
    n_heads = kc_ref.shape[1]
    d = n_heads * HEAD
    kc_all = kc_ref[...].reshape(past, d).astype(BF16)
    vc_all = vc_ref[...].reshape(past, d).astype(BF16)
    pad = jnp.zeros((LANES - ts, HEAD), BF16)
    lam_val = _lam_value(lam_ref, lam_init)
    subln = subln_ref[...]
    for hh in range(n_heads):
        cs = slice(hh * HEAD, (hh + 1) * HEAD)
        q2 = _stack_maps(q_ref[:, cs])
        kc = kc_all[:, cs]
        vc = vc_all[:, cs]
        kn = jnp.concatenate([kn_ref[:, cs], pad], axis=0)
        vn = jnp.concatenate([vn_ref[:, cs], pad], axis=0)
        bias = bias_ref[hh]
        bias2 = jnp.concatenate([bias, bias], axis=0)
        s_c = _dot_nt(q2, kc) + bias2[:, 0:past]
        s_n = _dot_nt(q2, kn) + bias2[:, past:past + LANES]
        m = jnp.maximum(jnp.max(s_c, axis=-1, keepdims=True), jnp.max(s_n, axis=-1, keepdims=True))
        p_c = jnp.exp2(s_c - m)
        p_n = jnp.exp2(s_n - m)
        l = jnp.sum(p_c, axis=-1, keepdims=True) + jnp.sum(p_n, axis=-1, keepdims=True)
        acc = _dot(p_c.astype(BF16), vc) + _dot(p_n.astype(BF16), vn)
        o = acc[0:ts] / l[0:ts] - lam_val * (acc[ts:2 * ts] / l[ts:2 * ts])
        out_ref[:, cs] = _merge_out(o, lam_init, subln, ma_ref[:, cs], gb_ref[:, cs])


def _attn_step_call(q, cache_k, cache_v, k_new, v_new, bias, lam, subln, ma, main, layer, ts, lam_init):
    n, d = q.shape
    _, nb, past, h, _ = cache_k.shape
    wk = past + LANES
    row = pl.BlockSpec((ts, d), lambda b: (b, 0))
    cache = pl.BlockSpec((None, None, past, h, HEAD), lambda b: (layer, b, 0, 0, 0))
    in_specs = [
        row, cache, cache, row, row,
        _resident((h, ts, wk), lambda b: (0, 0, 0)),
        pl.BlockSpec((None, 4, HEAD // 2), lambda b: (layer, 0, 0)),
        pl.BlockSpec((None, 1, HEAD), lambda b: (layer, 0, 0)),
        row,
        pl.BlockSpec((ts, d), lambda b: (b, N_MAIN_TILES - 1)),
    ]
    return pl.pallas_call(
        functools.partial(_attn_step_kernel, ts=ts, past=past, lam_init=lam_init),
        out_shape=jax.ShapeDtypeStruct((n, d), BF16),
        grid=(nb,),
        in_specs=in_specs,
        out_specs=row,
        compiler_params=_cparams(("parallel",)),
        name="attn_step",
    )(q, cache_k, cache_v, k_new, v_new, bias, lam, subln, ma, main)


def _wo_ffn_kernel(m_ref, x_ref, gt1_ref, sc_ref, sh_ref, gt2_ref, w_o_ref, wi_ref, wo_ref,
                   g1_ref, b1_ref, g2_ref, b2_ref, o_ref, *, alpha, nf):
    y = _dot(m_ref[...], w_o_ref[...])
    x = _layer_norm(alpha * x_ref[...] + gt1_ref[...] * y, g1_ref[...], b1_ref[...])
    h = (x * (1.0 + sc_ref[...]) + sh_ref[...]).astype(BF16)
    dff = wo_ref.shape[0]
    tf = dff // nf
    y = None
    for j in range(nf):
        u = _dot(h, wi_ref[:, j * tf:(j + 1) * tf])
        v = _dot(h, wi_ref[:, dff + j * tf:dff + (j + 1) * tf])
        part = _dot((_silu(u) * v).astype(BF16), wo_ref[j * tf:(j + 1) * tf, :])
        y = part if y is None else y + part
    o_ref[...] = _layer_norm(alpha * x + gt2_ref[...] * y, g2_ref[...], b2_ref[...])


def _wo_ffn_call(merged, x, mod, w_o, w_ff_in, w_ff_out, ln1_g, ln1_b, ln2_g, ln2_b, layer, tm,
                 tiles_per_group, alpha, nf):
    n, d = x.shape
    dff = w_ff_out.shape[1]
    vec = pl.BlockSpec((None, 1, d), lambda i: (layer, 0, 0))
    rows = pl.BlockSpec((tm, d), lambda i: (i, 0))
    return pl.pallas_call(
        functools.partial(_wo_ffn_kernel, alpha=alpha, nf=nf),
        out_shape=jax.ShapeDtypeStruct((n, d), F32),
        grid=(n // tm,),
        in_specs=[rows, rows,
                  _mod_spec(mod, d, tiles_per_group, 2),
                  _mod_spec(mod, d, tiles_per_group, 4),
                  _mod_spec(mod, d, tiles_per_group, 3),
                  _mod_spec(mod, d, tiles_per_group, 5),
                  _resident((None, d, d), lambda i: (layer, 0, 0)),
                  _resident((None, d, 2 * dff), lambda i: (layer, 0, 0)),
                  _resident((None, dff, d), lambda i: (layer, 0, 0)),
                  vec, vec, vec, vec],
        out_specs=rows,
        compiler_params=_cparams(("parallel",)),
        name="wo_ffn",
    )(merged, x, mod, mod, mod, mod, w_o, w_ff_in, w_ff_out, ln1_g, ln1_b, ln2_g, ln2_b)


def _pick_tile(n, pref):
    t = min(n, pref)
    while n % t:
        t //= 2
    return t


def kernel(x_prompt, x_sample, cache_k, cache_v, state_conv, state_delta, c_prompt, c_sample,
           ln_in_g, ln_in_b, rel_bias, w_ada, b_ada, w_in, conv_w, a_log, dt_bias, norm_a,
           lam, subln_g, w_o, ln1_g, ln1_b, w_ff_in, w_ff_out, ln2_g, ln2_b):
    bp, tp, d = x_prompt.shape
    bs, ts, _ = x_sample.shape
    depth = w_in.shape[0]
    h = d // HEAD
    past = cache_k.shape[2]
    dff = w_ff_out.shape[1]
    n_p = bp * tp
    n_s = bs * ts
    alpha = (2 * depth) ** 0.25
    hb = h
    ng = h // hb

    o1 = 4 * d + 2 * h
    qkvz = w_in[:, :, 0:4 * d]
    gates = w_in[:, :, o1 + 3 * d:o1 + 5 * d]
    qb = w_in[:, :, o1:o1 + d] * ((HEAD // 2) ** -0.5 * LOG2E)
    kvb = w_in[:, :, o1 + d:o1 + 3 * d]
    w_all = jnp.concatenate([qkvz, gates, qb, kvb], axis=-1).astype(BF16)
    wb = w_in[:, :, 4 * d:4 * d + h].reshape(depth, d, ng, hb)
    wa = w_in[:, :, 4 * d + h:4 * d + 2 * h].reshape(depth, d, ng, hb)
    w_ba = jnp.concatenate([wb, wa, jnp.zeros((depth, d, ng, LANES - 2 * hb), F32)], axis=-1)
    w_ba = w_ba.reshape(depth, d, ng * LANES).astype(BF16)
    w_ada_b = w_ada.astype(BF16)
    w_o_b = w_o.astype(BF16)
    w_ff_in_b = w_ff_in.astype(BF16)
    w_ff_out_b = w_ff_out.astype(BF16)

    def lane_rows(v):
        v = v.reshape(depth, ng, hb)
        z = jnp.zeros((depth, ng, hb), F32)
        zz = jnp.zeros((depth, ng, LANES - 2 * hb), F32)
        return jnp.concatenate([z, v, zz], axis=-1).reshape(depth, ng * LANES)

    par = jnp.zeros((depth, 8, ng * LANES), F32)
    par = par.at[:, 0].set(lane_rows(dt_bias)).at[:, 1].set(lane_rows(a_log))
    par = par.at[:, 2].set(jnp.tile(norm_a, (1, ng)))
    subln = subln_g.reshape(depth, 1, HEAD)
    ln1g, ln1b = ln1_g.reshape(depth, 1, d), ln1_b.reshape(depth, 1, d)
    ln2g, ln2b = ln2_g.reshape(depth, 1, d), ln2_b.reshape(depth, 1, d)

    mod = _ada_call(jnp.concatenate([c_prompt, c_sample], axis=0), w_ada_b, b_ada)
    mod_p = mod[:, :bp].reshape(depth, bp, 1, 6 * d)
    mod_s = jnp.repeat(mod[:, bp:], ts, axis=1).reshape(depth, 1, n_s, 6 * d)

    bias_p, bias_s = _bias_call(rel_bias, TQ, past, ts)

    tm_p = _pick_tile(tp, TM_PROJ)
    tpg_p = tp // tm_p
    xp = _ln_call(x_prompt.reshape(n_p, d), ln_in_g, ln_in_b, tm_p)
    xs = _ln_call(x_sample.reshape(n_s, d), ln_in_g, ln_in_b, n_s)

    conv0 = jnp.zeros((bp, CONV_W - 1, 3 * d), F32)
    s0 = jnp.zeros((bp, h, HEAD, HEAD), F32)
    nf = 2 if dff % (2 * LANES) == 0 else 1

    kp, vp = (jnp.zeros((depth, n_p, h, HEAD), F32) for _ in range(2))
    ks, vs = (jnp.zeros((depth, n_s, h, HEAD), F32) for _ in range(2))
    conv_p, conv_s, st_p, st_s = [], [], [], []
    for l in range(depth):
        lam_init = 0.8 - 0.6 * math.exp(-0.3 * l)
        main, qb_p, kb_l, vb_l, kp, vp, ba, tail = _inproj_call(
            xp, mod_p[l], w_all, w_ba, conv_w, conv0, kp, vp, l, tm_p, tpg_p, BF16, True)
        ma, sn = _delta_call(main, ba, conv_w, par, conv0, s0, l, bp, tp, CHUNK, DELTA_CPS, hb, True)
        merged = _attn_call(qb_p, kb_l, vb_l, bias_p, lam, subln, ma, main, l, bp, tp, TQ, lam_init)
        xp = _wo_ffn_call(merged, xp, mod_p[l], w_o_b, w_ff_in_b, w_ff_out_b, ln1g, ln1b, ln2g, ln2b,
                          l, tm_p, tpg_p, alpha, nf)
        conv_p.append(tail.reshape(bp, tpg_p, CONV_PAD, 3 * d)[:, tpg_p - 1, CONV_PAD - (CONV_W - 1):, :])
        st_p.append(sn)
        main, qb_s, kb_l, vb_l, ks, vs, ba, _ = _inproj_call(
            xs, mod_s[l], w_all, w_ba, conv_w, state_conv[l], ks, vs, l, n_s, 1, F32, False)
        ma, sn = _delta_call(main, ba, conv_w, par, state_conv[l], state_delta[l], l, bs, ts, ts, 1, hb, False)
        merged = _attn_step_call(qb_s, cache_k, cache_v, kb_l, vb_l, bias_s, lam, subln, ma, main,
                                 l, ts, lam_init)
        xs = _wo_ffn_call(merged, xs, mod_s[l], w_o_b, w_ff_in_b, w_ff_out_b, ln1g, ln1b, ln2g, ln2b,
                          l, n_s, 1, alpha, nf)
        conv_s.append(main.reshape(bs, ts, -1)[:, ts - (CONV_W - 1):, 0:3 * d])
        st_s.append(sn)

    def heads_out(buf, nb, t):
        return buf.reshape(depth, nb, t, h, HEAD)

    return (xp.reshape(bp, tp, d), xs.reshape(bs, ts, d),
            heads_out(kp, bp, tp), heads_out(vp, bp, tp), jnp.stack(conv_p), jnp.stack(st_p),
            heads_out(ks, bs, ts), heads_out(vs, bs, ts), jnp.stack(conv_s), jnp.stack(st_s))
```

```python
import functools
import math

import jax
import jax.numpy as jnp
from jax import lax
from jax.experimental import pallas as pl
from jax.experimental.pallas import tpu as pltpu

F32 = jnp.float32
BF16 = jnp.bfloat16
HIGHEST = lax.Precision.HIGHEST

LN_EPS = 1e-5
CHUNK = 64
HEAD = 128
CONV_W = 4
N_BUCKETS = 32
MAX_DIST = 128
NEG = -1e30
LOG2E = math.log2(math.e)
LANES = 128
CONV_PAD = 8
VMEM_LIMIT = 56 * 1024 * 1024

TM_PROJ = 512
TQ = 256
DELTA_CPS = 8


def _cparams(sem):
    return pltpu.CompilerParams(dimension_semantics=sem, vmem_limit_bytes=VMEM_LIMIT)


def _sigmoid(x):
    return 1.0 / (1.0 + jnp.exp(-x))


def _silu(x):
    return x * _sigmoid(x)


def _layer_norm(x, g, b):
    mu = jnp.mean(x, axis=-1, keepdims=True)
    xc = x - mu
    var = jnp.mean(xc * xc, axis=-1, keepdims=True)
    return xc * lax.rsqrt(var + LN_EPS) * g + b


def _dot(a, b):
    return jnp.dot(a, b, preferred_element_type=F32)


def _hdot(a, b):
    return jnp.dot(a, b, preferred_element_type=F32, precision=HIGHEST)


def _bdot(a, b):
    return _dot(a.astype(BF16), b.astype(BF16))


def _dot_nt(a, b):
    return lax.dot_general(a, b, (((1,), (1,)), ((), ())), preferred_element_type=F32)


def _dot_tn(a, b):
    return lax.dot_general(a, b, (((0,), (0,)), ((), ())), preferred_element_type=F32)


def _ln_kernel(x_ref, g_ref, b_ref, o_ref):
    o_ref[...] = _layer_norm(x_ref[...], g_ref[...], b_ref[...])


def _ln_call(x, g, b, tm):
    n, d = x.shape
    return pl.pallas_call(
        _ln_kernel,
        out_shape=jax.ShapeDtypeStruct((n, d), F32),
        grid=(n // tm,),
        in_specs=[pl.BlockSpec((tm, d), lambda i: (i, 0)),
                  pl.BlockSpec((1, d), lambda i: (0, 0)),
                  pl.BlockSpec((1, d), lambda i: (0, 0))],
        out_specs=pl.BlockSpec((tm, d), lambda i: (i, 0)),
        compiler_params=_cparams(("parallel",)),
        name="ln_in",
    )(x, g.reshape(1, d), b.reshape(1, d))


def _ada_kernel(c_ref, w_ref, b_ref, o_ref):
    c = c_ref[...]
    o_ref[...] = _dot(_silu(c).astype(BF16), w_ref[...]) + b_ref[...]


def _ada_call(c_all, w_ada, b_ada):
    depth, d, d6 = w_ada.shape
    nb = c_all.shape[0]
    return pl.pallas_call(
        _ada_kernel,
        out_shape=jax.ShapeDtypeStruct((depth, nb, d6), F32),
        grid=(depth, d6 // d),
        in_specs=[pl.BlockSpec((nb, d), lambda l, j: (0, 0)),
                  pl.BlockSpec((None, d, d), lambda l, j: (l, 0, j)),
                  pl.BlockSpec((None, 1, d), lambda l, j: (l, 0, j))],
        out_specs=pl.BlockSpec((None, nb, d), lambda l, j: (l, 0, j)),
        compiler_params=_cparams(("parallel", "parallel")),
        name="ada",
    )(c_all, w_ada, b_ada.reshape(depth, 1, d6))


def _mod_spec(mod, d, tiles_per_group, chunk):
    rows = mod.shape[1]
    return pl.BlockSpec((None, rows, d), lambda i, *_: (i // tiles_per_group, 0, chunk))


N_MAIN_TILES = 6
N_W_TILES = 9
N_CONV_TILES = 3


def _conv_taps(cat_ref, cw, rows):
    acc = None
    for i in range(CONV_W):
        term = cat_ref[pl.ds(CONV_PAD - (CONV_W - 1) + i, rows), :] * cw[i:i + 1, :]
        acc = term if acc is None else acc + term
    return acc


def _conv_taps_rolled(prev, cur, cw):
    assert CONV_W == 4 and CONV_PAD >= 2 * (CONV_W - 1)
    rows = cur.shape[0]
    cat = jnp.concatenate([prev, cur], axis=0)
    cat1 = pltpu.roll(cat, 1, axis=0)
    near = cat * cw[3:4, :] + cat1 * cw[2:3, :]
    far = cat * cw[1:2, :] + cat1 * cw[0:1, :]
    return (near + pltpu.roll(far, 2, axis=0))[CONV_PAD:CONV_PAD + rows, :]


def _l2norm_heads(y, scale):
    outs = []
    for hh in range(y.shape[1] // HEAD):
        seg = y[:, hh * HEAD:(hh + 1) * HEAD]
        outs.append(seg * (lax.rsqrt(jnp.sum(seg * seg, axis=-1, keepdims=True) + 1e-6) * scale))
    return jnp.concatenate(outs, axis=1)


def _inproj_kernel(x_ref, sc_ref, sh_ref, w_ref, wba_ref, cw_ref, cb_ref, kin_ref, vin_ref,
                   main_ref, q_ref, kb_ref, vb_ref, k_ref, v_ref, ba_ref, tail_ref,
                   carry_scr, *, fuse_conv, tiles_per_group):
    tm, d = x_ref.shape
    h = (x_ref[...] * (1.0 + sc_ref[...]) + sh_ref[...]).astype(BF16)

    def cols(j):
        return _dot(h, w_ref[:, j * d:(j + 1) * d])

    if fuse_conv:
        @pl.when(pl.program_id(0) % tiles_per_group == 0)
        def _():
            for j in range(N_CONV_TILES):
                carry_scr[j, CONV_PAD - (CONV_W - 1):CONV_PAD, :] = cb_ref[:, j * d:(j + 1) * d]

    def finish(j, res):
        if j < N_CONV_TILES:
            last = res[tm - CONV_PAD:tm, :]
            tail_ref[:, j * d:(j + 1) * d] = last
            if fuse_conv:
                prev = carry_scr[j]
                carry_scr[j] = last
                res = _silu(_conv_taps_rolled(prev, res, cw_ref[:, j * d:(j + 1) * d]))
                if j == 0:
                    res = _l2norm_heads(res, HEAD ** -0.5)
                elif j == 1:
                    res = _l2norm_heads(res, 1.0)
        if j < N_MAIN_TILES:
            main_ref[:, j * d:(j + 1) * d] = res.astype(main_ref.dtype)
        elif j == N_MAIN_TILES:
            q_ref[...] = res.astype(BF16)
        else:
            b_ref, o_ref = ((kb_ref, k_ref), (vb_ref, v_ref))[j - N_MAIN_TILES - 1]
            b_ref[...] = res.astype(BF16)
            o_ref[...] = res.reshape(o_ref.shape)

    order = []
    for j in range(N_CONV_TILES):
        order += [j, N_CONV_TILES + j]
    order += list(range(2 * N_CONV_TILES, N_W_TILES))
    nxt = cols(order[0])
    for pos, j in enumerate(order):
        res = nxt
        if pos + 1 < len(order):
            nxt = cols(order[pos + 1])
        finish(j, res)
    ba_ref[...] = _dot(h, wba_ref[...])


def _resident(block_shape, index_map):
    return pl.BlockSpec(block_shape, index_map, pipeline_mode=pl.Buffered(1))


def _inproj_call(x, mod, w_all, w_ba, conv_w, conv_buf, kbuf, vbuf, layer, tm, tiles_per_group,
                 main_dtype, fuse_conv):
    n, d = x.shape
    nba = w_ba.shape[-1]
    wide = N_W_TILES * d
    cwide = N_CONV_TILES * d
    in_specs = [
        pl.BlockSpec((tm, d), lambda i: (i, 0)),
        _mod_spec(mod, d, tiles_per_group, 1),
        _mod_spec(mod, d, tiles_per_group, 0),
        _resident((None, d, wide), lambda i: (layer, 0, 0)),
        _resident((None, d, nba), lambda i: (layer, 0, 0)),
        _resident((None, CONV_W, cwide), lambda i: (layer, 0, 0)),
        pl.BlockSpec((None, CONV_W - 1, cwide), lambda i: (i // tiles_per_group if fuse_conv else 0, 0, 0)),
        pl.BlockSpec(memory_space=pl.ANY),
        pl.BlockSpec(memory_space=pl.ANY),
    ]
    out_shape = [
        jax.ShapeDtypeStruct((n, N_MAIN_TILES * d), main_dtype),
        jax.ShapeDtypeStruct((n, d), BF16),
        jax.ShapeDtypeStruct((n, d), BF16),
        jax.ShapeDtypeStruct((n, d), BF16),
        jax.ShapeDtypeStruct(kbuf.shape, F32),
        jax.ShapeDtypeStruct(vbuf.shape, F32),
        jax.ShapeDtypeStruct((n, nba), F32),
        jax.ShapeDtypeStruct((n // tm, CONV_PAD, N_CONV_TILES * d), F32),
    ]
    out_specs = [
        pl.BlockSpec((tm, N_MAIN_TILES * d), lambda i: (i, 0)),
        pl.BlockSpec((tm, d), lambda i: (i, 0)),
        pl.BlockSpec((tm, d), lambda i: (i, 0)),
        pl.BlockSpec((tm, d), lambda i: (i, 0)),
        pl.BlockSpec((None, tm, d // HEAD, HEAD), lambda i: (layer, i, 0, 0)),
        pl.BlockSpec((None, tm, d // HEAD, HEAD), lambda i: (layer, i, 0, 0)),
        pl.BlockSpec((tm, nba), lambda i: (i, 0)),
        pl.BlockSpec((None, CONV_PAD, N_CONV_TILES * d), lambda i: (i, 0, 0)),
    ]
    return pl.pallas_call(
        functools.partial(_inproj_kernel, fuse_conv=fuse_conv, tiles_per_group=tiles_per_group),
        out_shape=out_shape,
        grid=(n // tm,),
        in_specs=in_specs,
        out_specs=out_specs,
        scratch_shapes=[pltpu.VMEM((N_CONV_TILES, CONV_PAD, d), F32)],
        input_output_aliases={7: 4, 8: 5},
        compiler_params=_cparams(("arbitrary",)),
        name="inproj",
    )(x, mod, mod, w_all, w_ba, conv_w, conv_buf, kbuf, vbuf)


def _tri_inverse(a_list, eye, blk, size):
    s = 8
    d = [jnp.where(blk[s], a, 0.0) for a in a_list]
    d2 = [_bdot(x, x) for x in d]
    p = [_bdot(eye - x, eye + y) for x, y in zip(d, d2)]
    d4 = [_bdot(x, x) for x in d2]
    t = [_bdot(x, eye + y) for x, y in zip(p, d4)]
    while s < size:
        lower = jnp.logical_and(blk[2 * s], jnp.logical_not(blk[s]))
        off = [jnp.where(lower, a, 0.0) for a in a_list]
        x = [_bdot(ti, oi) for ti, oi in zip(t, off)]
        t = [ti - _bdot(xi, ti) for ti, xi in zip(t, x)]
        s *= 2
    return t


def _delta_kernel(qu_ref, ku_ref, vu_ref, z_ref, ga_ref, ba_ref,
                  cwq_ref, cwk_ref, cwv_ref, par_ref, cbq_ref, cbk_ref, cbv_ref, s0_ref,
                  ma_ref, sn_ref, s_scr, cat_scr, *, chunk, cps, hb, pre_conv):
    c = pl.program_id(2)
    rows_blk = chunk * cps
    tail = CONV_W - 1

    @pl.when(c == 0)
    def _():
        s_scr[...] = s0_ref[...]
        if not pre_conv:
            for s, cb in enumerate((cbq_ref, cbk_ref, cbv_ref)):
                cat_scr[s, CONV_PAD - tail:CONV_PAD, :] = cb[...]

    if pre_conv:
        convs = [u_ref[...].astype(F32) for u_ref in (qu_ref, ku_ref, vu_ref)]
    else:
        convs = []
        for s, (u_ref, cw_ref) in enumerate(((qu_ref, cwq_ref), (ku_ref, cwk_ref), (vu_ref, cwv_ref))):
            cat_scr[s, CONV_PAD:CONV_PAD + rows_blk, :] = u_ref[...].astype(F32)
            y = _silu(_conv_taps(cat_scr.at[s], cw_ref[...], rows_blk))
            convs.append(y if s == 2 else _l2norm_heads(y, HEAD ** -0.5 if s == 0 else 1.0))
            cat_scr[s, CONV_PAD - tail:CONV_PAD, :] = cat_scr[s, CONV_PAD + rows_blk - tail:CONV_PAD + rows_blk, :]

    ba = ba_ref[...]
    dt_row = par_ref[0:1, :]
    alog_row = par_ref[1:2, :]
    norm_row = par_ref[2:3, 0:HEAD]
    beta_all = _sigmoid(ba)
    xa = ba + dt_row
    softplus = jnp.maximum(xa, 0.0) + jnp.log(1.0 + jnp.exp(-jnp.abs(xa)))
    g_all = -jnp.exp(alog_row) * softplus

    r = lax.broadcasted_iota(jnp.int32, (chunk, chunk), 0)
    cc = lax.broadcasted_iota(jnp.int32, (chunk, chunk), 1)
    causal = r >= cc
    strict = r > cc
    eye = jnp.where(r == cc, 1.0, 0.0).astype(F32)
    tril = jnp.where(causal, 1.0, 0.0).astype(F32)
    blk = {}
    s = 8
    while s <= chunk:
        blk[s] = jnp.bitwise_xor(r, cc) < s
        s *= 2

    heads = range(hb)
    pairs = [(ci, hh) for ci in range(cps) for hh in heads]
    rsl = [slice(ci * chunk, (ci + 1) * chunk) for ci in range(cps)]
    csl = [slice(hh * HEAD, (hh + 1) * HEAD) for hh in heads]
    gcum_all = [_hdot(tril, g_all[rs, :]) for rs in rsl]
    gcum_t = [g.T for g in gcum_all]

    q, k, v, beta, gcum, g_last, decay = [], [], [], [], [], [], []
    for ci, hh in pairs:
        rs, cs = rsl[ci], csl[hh]
        q.append(convs[0][rs, cs])
        k.append(convs[1][rs, cs])
        v.append(convs[2][rs, cs])
        beta.append(beta_all[rs, hh:hh + 1])
        gc = gcum_all[ci][:, hb + hh:hb + hh + 1]
        gr = gcum_t[ci][hb + hh:hb + hh + 1, :]
        gcum.append(gc)
        g_last.append(gc[chunk - 1:chunk, :])
        decay.append(jnp.where(causal, jnp.exp(jnp.where(causal, gc - gr, 0.0)), 0.0))
    idx = range(len(pairs))
    e_g = [jnp.exp(gcum[i]) for i in idx]
    kb = [k[i] * beta[i] for i in idx]
    k_b16 = [k[i].astype(BF16) for i in idx]
    kk_mat = [_dot_nt(kb[i].astype(BF16), k_b16[i]) for i in idx]
    qk = [_dot_nt(q[i].astype(BF16), k_b16[i]) * decay[i] for i in idx]
    a_mat = [jnp.where(strict, kk_mat[i] * decay[i], 0.0) for i in idx]
    t_inv = _tri_inverse(a_mat, eye, blk, chunk)
    uw = [_bdot(t_inv[i], jnp.concatenate([v[i] * beta[i], kb[i] * e_g[i]], axis=1)) for i in idx]
    q_g = [(q[i] * e_g[i]).astype(BF16) for i in idx]
    k_tail = [(k[i] * jnp.exp(g_last[i] - gcum[i])).astype(BF16) for i in idx]
    qk = [x.astype(BF16) for x in qk]

    for ci in range(cps):
        ids = [ci * hb + hh for hh in heads]
        state = [s_scr[hh] for hh in heads]
        state_b = [x.astype(BF16) for x in state]
        ws = [_dot(uw[i][:, HEAD:2 * HEAD].astype(BF16), state_b[hh]) for hh, i in zip(heads, ids)]
        u = [uw[i][:, 0:HEAD] - ws[hh] for hh, i in zip(heads, ids)]
        u_b = [x.astype(BF16) for x in u]
        qs = [_dot(q_g[i], state_b[hh]) for hh, i in zip(heads, ids)]
        ku = [_dot_tn(k_tail[i], u_b[hh]) for hh, i in zip(heads, ids)]
        qu = [_dot(qk[i], u_b[hh]) for hh, i in zip(heads, ids)]
        for hh, i in zip(heads, ids):
            s_scr[hh] = jnp.exp(g_last[i]) * state[hh] + ku[hh]
        for hh in heads:
            rs, cs = rsl[ci], csl[hh]
            o = qs[hh] + qu[hh]
            o = o * lax.rsqrt(jnp.mean(o * o, axis=-1, keepdims=True) + LN_EPS) * norm_row
            o = o * _silu(z_ref[rs, cs].astype(F32))
            ma_ref[rs, cs] = (_sigmoid(ga_ref[rs, cs].astype(F32)) * o).astype(ma_ref.dtype)

    @pl.when(c == pl.num_programs(2) - 1)
    def _():
        sn_ref[...] = s_scr[...]


def _delta_call(main, ba, conv_w, par, conv_buf, s0, layer, nb, t, chunk, cps, hb, pre_conv):
    n = main.shape[0]
    d = main.shape[1] // N_MAIN_TILES
    h = d // HEAD
    ng = h // hb
    wb = hb * HEAD
    rows_blk = chunk * cps
    ncb = t // rows_blk
    grid = (nb, ng, ncb)

    def seg(s):
        return pl.BlockSpec((rows_blk, wb), lambda b, g, c: (b * ncb + c, s * ng + g))

    def cw(s):
        return pl.BlockSpec((None, CONV_W, wb), lambda b, g, c: (layer, 0, s * ng + g))

    def cb(s):
        return pl.BlockSpec((None, CONV_W - 1, wb), lambda b, g, c: (b, 0, s * ng + g))

    in_specs = [seg(0), seg(1), seg(2), seg(3), seg(4),
                pl.BlockSpec((rows_blk, LANES), lambda b, g, c: (b * ncb + c, g)),
                cw(0), cw(1), cw(2),
                pl.BlockSpec((None, 8, LANES), lambda b, g, c: (layer, 0, g)),
                cb(0), cb(1), cb(2),
                pl.BlockSpec((None, hb, HEAD, HEAD), lambda b, g, c: (b, g, 0, 0))]
    out_shape = [jax.ShapeDtypeStruct((n, d), BF16),
                 jax.ShapeDtypeStruct((nb, h, HEAD, HEAD), F32)]
    out_specs = [pl.BlockSpec((rows_blk, wb), lambda b, g, c: (b * ncb + c, g)),
                 pl.BlockSpec((None, hb, HEAD, HEAD), lambda b, g, c: (b, g, 0, 0))]
    return pl.pallas_call(
        functools.partial(_delta_kernel, chunk=chunk, cps=cps, hb=hb, pre_conv=pre_conv),
        out_shape=out_shape,
        grid=grid,
        in_specs=in_specs,
        out_specs=out_specs,
        scratch_shapes=[pltpu.VMEM((hb, HEAD, HEAD), F32),
                        pltpu.VMEM((3, CONV_PAD + rows_blk, wb), F32)],
        compiler_params=_cparams(("parallel", "parallel", "arbitrary")),
        name="delta",
    )(main, main, main, main, main, ba, conv_w, conv_w, conv_w, par,
      conv_buf, conv_buf, conv_buf, s0)


def _bucket_thresholds():
    nb = N_BUCKETS // 2
    max_exact = nb // 2
    ratio = MAX_DIST // max_exact
    steps = nb - max_exact
    out = []
    for kk in range(1, steps):
        n = max_exact
        while n ** steps < (max_exact ** steps) * (ratio ** kk):
            n += 1
        out.append(n)
    return out


def _far_distance():
    return _bucket_thresholds()[-1]


def _rel_bias_tile(tab_ref, head, qpos, kpos):
    nb = N_BUCKETS // 2
    max_exact = nb // 2
    rel = kpos - qpos
    n = jnp.abs(rel)
    large = jnp.full(rel.shape, max_exact, jnp.int32)
    for thr in _bucket_thresholds():
        large = large + jnp.where(n >= thr, 1, 0)
    bucket = jnp.where(rel > 0, nb, 0) + jnp.where(n < max_exact, n, large)
    far = tab_ref[nb - 1, head]
    bias = jnp.zeros(rel.shape, F32)
    for b in range(N_BUCKETS):
        bias = jnp.where(bucket == b, (tab_ref[b, head] - far) * LOG2E, bias)
    shift = CHUNK.bit_length() - 1
    mask = lax.shift_right_logical(kpos, shift) <= lax.shift_right_logical(qpos, shift)
    return jnp.where(mask, bias, NEG)


def _bias_kernel(tab_ref, pt_ref, st_ref, *, tq, past, ts):
    head = pl.program_id(0)
    ki = lax.broadcasted_iota(jnp.int32, (tq, 2 * tq), 0)
    qi = lax.broadcasted_iota(jnp.int32, (tq, 2 * tq), 1)
    qi = jnp.where(qi >= tq, qi - tq, qi)
    pt_ref[0] = _rel_bias_tile(tab_ref, head, qi + tq, ki + tq)
    pt_ref[1] = _rel_bias_tile(tab_ref, head, qi + tq, ki)
    wk = past + LANES
    qs = lax.broadcasted_iota(jnp.int32, (ts, wk), 0) + past
    ks = lax.broadcasted_iota(jnp.int32, (ts, wk), 1)
    st = _rel_bias_tile(tab_ref, head, qs, ks)
    st_ref[...] = jnp.where(ks < past + ts, st, NEG)


def _bias_call(rel_bias, tq, past, ts):
    h = rel_bias.shape[1]
    wk = past + LANES
    return pl.pallas_call(
        functools.partial(_bias_kernel, tq=tq, past=past, ts=ts),
        out_shape=[jax.ShapeDtypeStruct((h, 2, tq, 2 * tq), F32),
                   jax.ShapeDtypeStruct((h, ts, wk), F32)],
        grid=(h,),
        in_specs=[pl.BlockSpec(memory_space=pltpu.SMEM)],
        out_specs=[pl.BlockSpec((None, 2, tq, 2 * tq), lambda i: (i, 0, 0, 0)),
                   pl.BlockSpec((None, ts, wk), lambda i: (i, 0, 0))],
        compiler_params=_cparams(("arbitrary",)),
        name="rel_bias",
    )(rel_bias)


def _lam_value(lam_ref, lam_init):
    lp = lam_ref[...]
    s1 = jnp.sum(lp[0:1, :] * lp[1:2, :], axis=-1, keepdims=True)
    s2 = jnp.sum(lp[2:3, :] * lp[3:4, :], axis=-1, keepdims=True)
    return jnp.exp(s1) - jnp.exp(s2) + lam_init


def _stack_maps(q):
    lane = lax.broadcasted_iota(jnp.int32, q.shape, 1)
    zero = jnp.zeros_like(q)
    half = HEAD // 2
    return jnp.concatenate([jnp.where(lane < half, q, zero), jnp.where(lane >= half, q, zero)], axis=0)


def _merge_out(o, lam_init, subln, ma, gb):
    o = o * lax.rsqrt(jnp.mean(o * o, axis=-1, keepdims=True) + LN_EPS) * subln * (1.0 - lam_init)
    return (ma.astype(F32) + _sigmoid(gb.astype(F32)) * o).astype(BF16)


def _attn_scores(n_blocks, q2, k_ref, bias_ref, tq):
    kv = n_blocks * tq
    pieces = []
    if n_blocks > 2:
        pieces.append((0, kv - 2 * tq, None))
    if n_blocks >= 2:
        pieces.append((kv - 2 * tq, tq, 1))
    pieces.append((kv - tq, tq, 0))
    out = []
    for start, size, tile in pieces:
        s = _dot_nt(k_ref[start:start + size, :], q2)
        if tile is not None:
            s = s + bias_ref[tile]
        out.append((start, size, s))
    return out


def _attn_values(scored, vt):
    m = None
    for _, _, s in scored:
        mx = jnp.max(s, axis=0, keepdims=True)
        m = mx if m is None else jnp.maximum(m, mx)
    l = None
    acc = None
    for start, size, s in scored:
        p = jnp.exp2(s - m)
        ls = jnp.sum(p, axis=0, keepdims=True)
        pv = _dot(vt[:, start:start + size], p.astype(BF16))
        l = ls if l is None else l + ls
        acc = pv if acc is None else acc + pv
    return acc, l


def _attn_kernel(q_ref, k_ref, v_ref, bias_ref, lam_ref, subln_ref, ma_ref, gb_ref,
                 out_ref, vt, *, tq, nq, lam_init):
    vt[...] = v_ref[...].T
    lam_val = _lam_value(lam_ref, lam_init)
    subln = subln_ref[...]

    def scores(n):
        return _attn_scores(n, _stack_maps(q_ref[(n - 1) * tq:n * tq, :]), k_ref, bias_ref, tq)

    nxt = scores(1)
    for n in range(1, nq + 1):
        cur = nxt
        if n < nq:
            nxt = scores(n + 1)
        rows = slice((n - 1) * tq, n * tq)
        acc, l = _attn_values(cur, vt)
        o_t = acc[:, 0:tq] / l[:, 0:tq] - lam_val * (acc[:, tq:2 * tq] / l[:, tq:2 * tq])
        out_ref[rows, :] = _merge_out(o_t.T, lam_init, subln, ma_ref[rows, :], gb_ref[rows, :])


def _attn_call(q, k, v, bias, lam, subln, ma, main, layer, nb, t, tq, lam_init):
    n, d = q.shape
    h = d // HEAD
    nq = t // tq
    assert tq + 1 >= _far_distance() and tq % CHUNK == 0
    gb_col = (N_MAIN_TILES - 1) * h
    col = pl.BlockSpec((t, HEAD), lambda b, hh: (b, hh))
    in_specs = [
        col, col, col,
        pl.BlockSpec((None, 2, tq, 2 * tq), lambda b, hh: (hh, 0, 0, 0)),
        pl.BlockSpec((None, 4, HEAD // 2), lambda b, hh: (layer, 0, 0)),
        pl.BlockSpec((None, 1, HEAD), lambda b, hh: (layer, 0, 0)),
        col,
        pl.BlockSpec((t, HEAD), lambda b, hh: (b, gb_col + hh)),
    ]
    return pl.pallas_call(
        functools.partial(_attn_kernel, tq=tq, nq=nq, lam_init=lam_init),
        out_shape=jax.ShapeDtypeStruct((n, d), BF16),
        grid=(nb, h),
        in_specs=in_specs,
        out_specs=col,
        scratch_shapes=[pltpu.VMEM((HEAD, t), BF16)],
        compiler_params=_cparams(("parallel", "parallel")),
        name="attn",
    )(q, k, v, bias, lam, subln, ma, main)


def _attn_step_kernel(q_ref, kc_ref, vc_ref, kn_ref, vn_ref, bias_ref, lam_ref, subln_ref,
                      ma_ref, gb_ref, out_ref, *, ts, past, lam_init):
    n_heads = kc_ref.shape[1]
    d = n_heads * HEAD
    kc_all = kc_ref[...].reshape(past, d).astype(BF16)
    vc_all = vc_ref[...].reshape(past, d).astype(BF16)
    pad = jnp.zeros((LANES - ts, HEAD), BF16)
    lam_val = _lam_value(lam_ref, lam_init)
    subln = subln_ref[...]
    for hh in range(n_heads):
        cs = slice(hh * HEAD, (hh + 1) * HEAD)
        q2 = _stack_maps(q_ref[:, cs])
        kc = kc_all[:, cs]
        vc = vc_all[:, cs]
        kn = jnp.concatenate([kn_ref[:, cs], pad], axis=0)
        vn = jnp.concatenate([vn_ref[:, cs], pad], axis=0)
        bias = bias_ref[hh]
        bias2 = jnp.concatenate([bias, bias], axis=0)
        s_c = _dot_nt(q2, kc) + bias2[:, 0:past]
        s_n = _dot_nt(q2, kn) + bias2[:, past:past + LANES]
        m = jnp.maximum(jnp.max(s_c, axis=-1, keepdims=True), jnp.max(s_n, axis=-1, keepdims=True))
        p_c = jnp.exp2(s_c - m)
        p_n = jnp.exp2(s_n - m)
        l = jnp.sum(p_c, axis=-1, keepdims=True) + jnp.sum(p_n, axis=-1, keepdims=True)
        acc = _dot(p_c.astype(BF16), vc) + _dot(p_n.astype(BF16), vn)
        o = acc[0:ts] / l[0:ts] - lam_val * (acc[ts:2 * ts] / l[ts:2 * ts])
        out_ref[:, cs] = _merge_out(o, lam_init, subln, ma_ref[:, cs], gb_ref[:, cs])


def _attn_step_call(q, cache_k, cache_v, k_new, v_new, bias, lam, subln, ma, main, layer, ts, lam_init):
    n, d = q.shape
    _, nb, past, h, _ = cache_k.shape
    wk = past + LANES
    row = pl.BlockSpec((ts, d), lambda b: (b, 0))
    cache = pl.BlockSpec((None, None, past, h, HEAD), lambda b: (layer, b, 0, 0, 0))
    in_specs = [
        row, cache, cache, row, row,
        _resident((h, ts, wk), lambda b: (0, 0, 0)),
        pl.BlockSpec((None, 4, HEAD // 2), lambda b: (layer, 0, 0)),
        pl.BlockSpec((None, 1, HEAD), lambda b: (layer, 0, 0)),
        row,
        pl.BlockSpec((ts, d), lambda b: (b, N_MAIN_TILES - 1)),
    ]
    return pl.pallas_call(
        functools.partial(_attn_step_kernel, ts=ts, past=past, lam_init=lam_init),
        out_shape=jax.ShapeDtypeStruct((n, d), BF16),
        grid=(nb,),
        in_specs=in_specs,
        out_specs=row,
        compiler_params=_cparams(("parallel",)),
        name="attn_step",
    )(q, cache_k, cache_v, k_new, v_new, bias, lam, subln, ma, main)


def _wo_ffn_kernel(m_ref, x_ref, gt1_ref, sc_ref, sh_ref, gt2_ref, w_o_ref, wi_ref, wo_ref,
                   g1_ref, b1_ref, g2_ref, b2_ref, o_ref, *, alpha, nf):
    y = _dot(m_ref[...], w_o_ref[...])
    x = _layer_norm(alpha * x_ref[...] + gt1_ref[...] * y, g1_ref[...], b1_ref[...])
    h = (x * (1.0 + sc_ref[...]) + sh_ref[...]).astype(BF16)
    dff = wo_ref.shape[0]
    tf = dff // nf
    y = None
    for j in range(nf):
        u = _dot(h, wi_ref[:, j * tf:(j + 1) * tf])
        v = _dot(h, wi_ref[:, dff + j * tf:dff + (j + 1) * tf])
        part = _dot((_silu(u) * v).astype(BF16), wo_ref[j * tf:(j + 1) * tf, :])
        y = part if y is None else y + part
    o_ref[...] = _layer_norm(alpha * x + gt2_ref[...] * y, g2_ref[...], b2_ref[...])


def _wo_ffn_call(merged, x, mod, w_o, w_ff_in, w_ff_out, ln1_g, ln1_b, ln2_g, ln2_b, layer, tm,
                 tiles_per_group, alpha, nf):
    n, d = x.shape
    dff = w_ff_out.shape[1]
    vec = pl.BlockSpec((None, 1, d), lambda i: (layer, 0, 0))
    rows = pl.BlockSpec((tm, d), lambda i: (i, 0))
    return pl.pallas_call(
        functools.partial(_wo_ffn_kernel, alpha=alpha, nf=nf),
        out_shape=jax.ShapeDtypeStruct((n, d), F32),
        grid=(n // tm,),
        in_specs=[rows, rows,
                  _mod_spec(mod, d, tiles_per_group, 2),
                  _mod_spec(mod, d, tiles_per_group, 4),
                  _mod_spec(mod, d, tiles_per_group, 3),
                  _mod_spec(mod, d, tiles_per_group, 5),
                  _resident((None, d, d), lambda i: (layer, 0, 0)),
                  _resident((None, d, 2 * dff), lambda i: (layer, 0, 0)),
                  _resident((None, dff, d), lambda i: (layer, 0, 0)),
                  vec, vec, vec, vec],
        out_specs=rows,
        compiler_params=_cparams(("parallel",)),
        name="wo_ffn",
    )(merged, x, mod, mod, mod, mod, w_o, w_ff_in, w_ff_out, ln1_g, ln1_b, ln2_g, ln2_b)


def _pick_tile(n, pref):
    t = min(n, pref)
    while n % t:
        t //= 2
    return t


def kernel(x_prompt, x_sample, cache_k, cache_v, state_conv, state_delta, c_prompt, c_sample,
           ln_in_g, ln_in_b, rel_bias, w_ada, b_ada, w_in, conv_w, a_log, dt_bias, norm_a,
           lam, subln_g, w_o, ln1_g, ln1_b, w_ff_in, w_ff_out, ln2_g, ln2_b):
    bp, tp, d = x_prompt.shape
    bs, ts, _ = x_sample.shape
    depth = w_in.shape[0]
    h = d // HEAD
    past = cache_k.shape[2]
    dff = w_ff_out.shape[1]
    n_p = bp * tp
    n_s = bs * ts
    alpha = (2 * depth) ** 0.25
    hb = h
    ng = h // hb

    o1 = 4 * d + 2 * h
    qkvz = w_in[:, :, 0:4 * d]
    gates = w_in[:, :, o1 + 3 * d:o1 + 5 * d]
    qb = w_in[:, :, o1:o1 + d] * ((HEAD // 2) ** -0.5 * LOG2E)
    kvb = w_in[:, :, o1 + d:o1 + 3 * d]
    w_all = jnp.concatenate([qkvz, gates, qb, kvb], axis=-1).astype(BF16)
    wb = w_in[:, :, 4 * d:4 * d + h].reshape(depth, d, ng, hb)
    wa = w_in[:, :, 4 * d + h:4 * d + 2 * h].reshape(depth, d, ng, hb)
    w_ba = jnp.concatenate([wb, wa, jnp.zeros((depth, d, ng, LANES - 2 * hb), F32)], axis=-1)
    w_ba = w_ba.reshape(depth, d, ng * LANES).astype(BF16)
    w_ada_b = w_ada.astype(BF16)
    w_o_b = w_o.astype(BF16)
    w_ff_in_b = w_ff_in.astype(BF16)
    w_ff_out_b = w_ff_out.astype(BF16)

    def lane_rows(v):
        v = v.reshape(depth, ng, hb)
        z = jnp.zeros((depth, ng, hb), F32)
        zz = jnp.zeros((depth, ng, LANES - 2 * hb), F32)
        return jnp.concatenate([z, v, zz], axis=-1).reshape(depth, ng * LANES)

    par = jnp.zeros((depth, 8, ng * LANES), F32)
    par = par.at[:, 0].set(lane_rows(dt_bias)).at[:, 1].set(lane_rows(a_log))
    par = par.at[:, 2].set(jnp.tile(norm_a, (1, ng)))
    subln = subln_g.reshape(depth, 1, HEAD)
    ln1g, ln1b = ln1_g.reshape(depth, 1, d), ln1_b.reshape(depth, 1, d)
    ln2g, ln2b = ln2_g.reshape(depth, 1, d), ln2_b.reshape(depth, 1, d)

    mod = _ada_call(jnp.concatenate([c_prompt, c_sample], axis=0), w_ada_b, b_ada)
    mod_p = mod[:, :bp].reshape(depth, bp, 1, 6 * d)
    mod_s = jnp.repeat(mod[:, bp:], ts, axis=1).reshape(depth, 1, n_s, 6 * d)

    bias_p, bias_s = _bias_call(rel_bias, TQ, past, ts)

    tm_p = _pick_tile(tp, TM_PROJ)
    tpg_p = tp // tm_p
    xp = _ln_call(x_prompt.reshape(n_p, d), ln_in_g, ln_in_b, tm_p)
    xs = _ln_call(x_sample.reshape(n_s, d), ln_in_g, ln_in_b, n_s)

    conv0 = jnp.zeros((bp, CONV_W - 1, 3 * d), F32)
    s0 = jnp.zeros((bp, h, HEAD, HEAD), F32)
    nf = 2 if dff % (2 * LANES) == 0 else 1

    kp, vp = (jnp.zeros((depth, n_p, h, HEAD), F32) for _ in range(2))
    ks, vs = (jnp.zeros((depth, n_s, h, HEAD), F32) for _ in range(2))
    conv_p, conv_s, st_p, st_s = [], [], [], []
    for l in range(depth):
        lam_init = 0.8 - 0.6 * math.exp(-0.3 * l)
        main, qb_p, kb_l, vb_l, kp, vp, ba, tail = _inproj_call(
            xp, mod_p[l], w_all, w_ba, conv_w, conv0, kp, vp, l, tm_p, tpg_p, BF16, True)
        ma, sn = _delta_call(main, ba, conv_w, par, conv0, s0, l, bp, tp, CHUNK, DELTA_CPS, hb, True)
        merged = _attn_call(qb_p, kb_l, vb_l, bias_p, lam, subln, ma, main, l, bp, tp, TQ, lam_init)
        xp = _wo_ffn_call(merged, xp, mod_p[l], w_o_b, w_ff_in_b, w_ff_out_b, ln1g, ln1b, ln2g, ln2b,
                          l, tm_p, tpg_p, alpha, nf)
        conv_p.append(tail.reshape(bp, tpg_p, CONV_PAD, 3 * d)[:, tpg_p - 1, CONV_PAD - (CONV_W - 1):, :])
        st_p.append(sn)
        main, qb_s, kb_l, vb_l, ks, vs, ba, _ = _inproj_call(
            xs, mod_s[l], w_all, w_ba, conv_w, state_conv[l], ks, vs, l, n_s, 1, F32, False)
        ma, sn = _delta_call(main, ba, conv_w, par, state_conv[l], state_delta[l], l, bs, ts, ts, 1, hb, False)
        merged = _attn_step_call(qb_s, cache_k, cache_v, kb_l, vb_l, bias_s, lam, subln, ma, main,
                                 l, ts, lam_init)
        xs = _wo_ffn_call(merged, xs, mod_s[l], w_o_b, w_ff_in_b, w_ff_out_b, ln1g, ln1b, ln2g, ln2b,
                          l, n_s, 1, alpha, nf)
        conv_s.append(main.reshape(bs, ts, -1)[:, ts - (CONV_W - 1):, 0:3 * d])
        st_s.append(sn)

    def heads_out(buf, nb, t):
        return buf.reshape(depth, nb, t, h, HEAD)

    return (xp.reshape(bp, tp, d), xs.reshape(bs, ts, d),
            heads_out(kp, bp, tp), heads_out(vp, bp, tp), jnp.stack(conv_p), jnp.stack(st_p),
            heads_out(ks, bs, ts), heads_out(vs, bs, ts), jnp.stack(conv_s), jnp.stack(st_s))
```

```python
import functools
import math

import jax
import jax.numpy as jnp
from jax import lax
from jax.experimental import pallas as pl
from jax.experimental.pallas import tpu as pltpu

F32 = jnp.float32
BF16 = jnp.bfloat16
HIGHEST = lax.Precision.HIGHEST

LN_EPS = 1e-5
CHUNK = 64
HEAD = 128
CONV_W = 4
N_BUCKETS = 32
MAX_DIST = 128
NEG = -1e30
LOG2E = math.log2(math.e)
LANES = 128
CONV_PAD = 8
VMEM_LIMIT = 56 * 1024 * 1024

TM_PROJ = 512
TQ = 256
DELTA_CPS = 8


def _cparams(sem):
    return pltpu.CompilerParams(dimension_semantics=sem, vmem_limit_bytes=VMEM_LIMIT)


def _sigmoid(x):
    return 1.0 / (1.0 + jnp.exp(-x))


def _silu(x):
    return x * _sigmoid(x)


def _layer_norm(x, g, b):
    mu = jnp.mean(x, axis=-1, keepdims=True)
    xc = x - mu
    var = jnp.mean(xc * xc, axis=-1, keepdims=True)
    return xc * lax.rsqrt(var + LN_EPS) * g + b


def _dot(a, b):
    return jnp.dot(a, b, preferred_element_type=F32)


def _hdot(a, b):
    return jnp.dot(a, b, preferred_element_type=F32, precision=HIGHEST)


def _bdot(a, b):
    return _dot(a.astype(BF16), b.astype(BF16))


def _dot_nt(a, b):
    return lax.dot_general(a, b, (((1,), (1,)), ((), ())), preferred_element_type=F32)


def _dot_tn(a, b):
    return lax.dot_general(a, b, (((0,), (0,)), ((), ())), preferred_element_type=F32)


def _ln_kernel(x_ref, g_ref, b_ref, o_ref):
    o_ref[...] = _layer_norm(x_ref[...], g_ref[...], b_ref[...])


def _ln_call(x, g, b, tm):
    n, d = x.shape
    return pl.pallas_call(
        _ln_kernel,
        out_shape=jax.ShapeDtypeStruct((n, d), F32),
        grid=(n // tm,),
        in_specs=[pl.BlockSpec((tm, d), lambda i: (i, 0)),
                  pl.BlockSpec((1, d), lambda i: (0, 0)),
                  pl.BlockSpec((1, d), lambda i: (0, 0))],
        out_specs=pl.BlockSpec((tm, d), lambda i: (i, 0)),
        compiler_params=_cparams(("parallel",)),
        name="ln_in",
    )(x, g.reshape(1, d), b.reshape(1, d))


def _ada_kernel(c_ref, w_ref, b_ref, o_ref):
    c = c_ref[...]
    o_ref[...] = _dot(_silu(c).astype(BF16), w_ref[...]) + b_ref[...]


def _ada_call(c_all, w_ada, b_ada):
    depth, d, d6 = w_ada.shape
    nb = c_all.shape[0]
    return pl.pallas_call(
        _ada_kernel,
        out_shape=jax.ShapeDtypeStruct((depth, nb, d6), F32),
        grid=(depth, d6 // d),
        in_specs=[pl.BlockSpec((nb, d), lambda l, j: (0, 0)),
                  pl.BlockSpec((None, d, d), lambda l, j: (l, 0, j)),
                  pl.BlockSpec((None, 1, d), lambda l, j: (l, 0, j))],
        out_specs=pl.BlockSpec((None, nb, d), lambda l, j: (l, 0, j)),
        compiler_params=_cparams(("parallel", "parallel")),
        name="ada",
    )(c_all, w_ada, b_ada.reshape(depth, 1, d6))


def _mod_spec(mod, d, tiles_per_group, chunk):
    rows = mod.shape[1]
    return pl.BlockSpec((None, rows, d), lambda i, *_: (i // tiles_per_group, 0, chunk))


N_MAIN_TILES = 6
N_W_TILES = 9
N_CONV_TILES = 3


def _conv_taps(cat_ref, cw, rows):
    acc = None
    for i in range(CONV_W):
        term = cat_ref[pl.ds(CONV_PAD - (CONV_W - 1) + i, rows), :] * cw[i:i + 1, :]
        acc = term if acc is None else acc + term
    return acc


def _conv_taps_rolled(prev, cur, cw):
    assert CONV_W == 4 and CONV_PAD >= 2 * (CONV_W - 1)
    rows = cur.shape[0]
    cat = jnp.concatenate([prev, cur], axis=0)
    cat1 = pltpu.roll(cat, 1, axis=0)
    near = cat * cw[3:4, :] + cat1 * cw[2:3, :]
    far = cat * cw[1:2, :] + cat1 * cw[0:1, :]
    return (near + pltpu.roll(far, 2, axis=0))[CONV_PAD:CONV_PAD + rows, :]


def _l2norm_heads(y, scale):
    outs = []
    for hh in range(y.shape[1] // HEAD):
        seg = y[:, hh * HEAD:(hh + 1) * HEAD]
        outs.append(seg * (lax.rsqrt(jnp.sum(seg * seg, axis=-1, keepdims=True) + 1e-6) * scale))
    return jnp.concatenate(outs, axis=1)


def _inproj_kernel(x_ref, sc_ref, sh_ref, w_ref, wba_ref, cw_ref, cb_ref, kin_ref, vin_ref,
                   main_ref, q_ref, kb_ref, vb_ref, k_ref, v_ref, ba_ref, tail_ref,
                   carry_scr, *, fuse_conv, tiles_per_group):
    tm, d = x_ref.shape
    h = (x_ref[...] * (1.0 + sc_ref[...]) + sh_ref[...]).astype(BF16)

    def cols(j):
        return _dot(h, w_ref[:, j * d:(j + 1) * d])

    if fuse_conv:
        @pl.when(pl.program_id(0) % tiles_per_group == 0)
        def _():
            for j in range(N_CONV_TILES):
                carry_scr[j, CONV_PAD - (CONV_W - 1):CONV_PAD, :] = cb_ref[:, j * d:(j + 1) * d]

    def finish(j, res):
        if j < N_CONV_TILES:
            last = res[tm - CONV_PAD:tm, :]
            tail_ref[:, j * d:(j + 1) * d] = last
            if fuse_conv:
                prev = carry_scr[j]
                carry_scr[j] = last
                res = _silu(_conv_taps_rolled(prev, res, cw_ref[:, j * d:(j + 1) * d]))
                if j == 0:
                    res = _l2norm_heads(res, HEAD ** -0.5)
                elif j == 1:
                    res = _l2norm_heads(res, 1.0)
        if j < N_MAIN_TILES:
            main_ref[:, j * d:(j + 1) * d] = res.astype(main_ref.dtype)
        elif j == N_MAIN_TILES:
            q_ref[...] = res.astype(BF16)
        else:
            b_ref, o_ref = ((kb_ref, k_ref), (vb_ref, v_ref))[j - N_MAIN_TILES - 1]
            b_ref[...] = res.astype(BF16)
            o_ref[...] = res.reshape(o_ref.shape)

    order = []
    for j in range(N_CONV_TILES):
        order += [j, N_CONV_TILES + j]
    order += list(range(2 * N_CONV_TILES, N_W_TILES))
    nxt = cols(order[0])
    for pos, j in enumerate(order):
        res = nxt
        if pos + 1 < len(order):
            nxt = cols(order[pos + 1])
        finish(j, res)
    ba_ref[...] = _dot(h, wba_ref[...])


def _resident(block_shape, index_map):
    return pl.BlockSpec(block_shape, index_map, pipeline_mode=pl.Buffered(1))


def _inproj_call(x, mod, w_all, w_ba, conv_w, conv_buf, kbuf, vbuf, layer, tm, tiles_per_group,
                 main_dtype, fuse_conv):
    n, d = x.shape
    nba = w_ba.shape[-1]
    wide = N_W_TILES * d
    cwide = N_CONV_TILES * d
    in_specs = [
        pl.BlockSpec((tm, d), lambda i: (i, 0)),
        _mod_spec(mod, d, tiles_per_group, 1),
        _mod_spec(mod, d, tiles_per_group, 0),
        _resident((None, d, wide), lambda i: (layer, 0, 0)),
        _resident((None, d, nba), lambda i: (layer, 0, 0)),
        _resident((None, CONV_W, cwide), lambda i: (layer, 0, 0)),
        pl.BlockSpec((None, CONV_W - 1, cwide), lambda i: (i // tiles_per_group if fuse_conv else 0, 0, 0)),
        pl.BlockSpec(memory_space=pl.ANY),
        pl.BlockSpec(memory_space=pl.ANY),
    ]
    out_shape = [
        jax.ShapeDtypeStruct((n, N_MAIN_TILES * d), main_dtype),
        jax.ShapeDtypeStruct((n, d), BF16),
        jax.ShapeDtypeStruct((n, d), BF16),
        jax.ShapeDtypeStruct((n, d), BF16),
        jax.ShapeDtypeStruct(kbuf.shape, F32),
        jax.ShapeDtypeStruct(vbuf.shape, F32),
        jax.ShapeDtypeStruct((n, nba), F32),
        jax.ShapeDtypeStruct((n // tm, CONV_PAD, N_CONV_TILES * d), F32),
    ]
    out_specs = [
        pl.BlockSpec((tm, N_MAIN_TILES * d), lambda i: (i, 0)),
        pl.BlockSpec((tm, d), lambda i: (i, 0)),
        pl.BlockSpec((tm, d), lambda i: (i, 0)),
        pl.BlockSpec((tm, d), lambda i: (i, 0)),
        pl.BlockSpec((None, tm, d // HEAD, HEAD), lambda i: (layer, i, 0, 0)),
        pl.BlockSpec((None, tm, d // HEAD, HEAD), lambda i: (layer, i, 0, 0)),
        pl.BlockSpec((tm, nba), lambda i: (i, 0)),
        pl.BlockSpec((None, CONV_PAD, N_CONV_TILES * d), lambda i: (i, 0, 0)),
    ]
    return pl.pallas_call(
        functools.partial(_inproj_kernel, fuse_conv=fuse_conv, tiles_per_group=tiles_per_group),
        out_shape=out_shape,
        grid=(n // tm,),
        in_specs=in_specs,
        out_specs=out_specs,
        scratch_shapes=[pltpu.VMEM((N_CONV_TILES, CONV_PAD, d), F32)],
        input_output_aliases={7: 4, 8: 5},
        compiler_params=_cparams(("arbitrary",)),
        name="inproj",
    )(x, mod, mod, w_all, w_ba, conv_w, conv_buf, kbuf, vbuf)


def _tri_inverse(a_list, eye, blk, size):
    s = 8
    d = [jnp.where(blk[s], a, 0.0) for a in a_list]
    d2 = [_bdot(x, x) for x in d]
    p = [_bdot(eye - x, eye + y) for x, y in zip(d, d2)]
    d4 = [_bdot(x, x) for x in d2]
    t = [_bdot(x, eye + y) for x, y in zip(p, d4)]
    while s < size:
        lower = jnp.logical_and(blk[2 * s], jnp.logical_not(blk[s]))
        off = [jnp.where(lower, a, 0.0) for a in a_list]
        x = [_bdot(ti, oi) for ti, oi in zip(t, off)]
        t = [ti - _bdot(xi, ti) for ti, xi in zip(t, x)]
        s *= 2
    return t


def _delta_kernel(qu_ref, ku_ref, vu_ref, z_ref, ga_ref, ba_ref,
                  cwq_ref, cwk_ref, cwv_ref, par_ref, cbq_ref, cbk_ref, cbv_ref, s0_ref,
                  ma_ref, sn_ref, s_scr, cat_scr, *, chunk, cps, hb, pre_conv):
    c = pl.program_id(2)
    rows_blk = chunk * cps
    tail = CONV_W - 1

    @pl.when(c == 0)
    def _():
        s_scr[...] = s0_ref[...]
        if not pre_conv:
            for s, cb in enumerate((cbq_ref, cbk_ref, cbv_ref)):
                cat_scr[s, CONV_PAD - tail:CONV_PAD, :] = cb[...]

    if pre_conv:
        convs = [u_ref[...].astype(F32) for u_ref in (qu_ref, ku_ref, vu_ref)]
    else:
        convs = []
        for s, (u_ref, cw_ref) in enumerate(((qu_ref, cwq_ref), (ku_ref, cwk_ref), (vu_ref, cwv_ref))):
            cat_scr[s, CONV_PAD:CONV_PAD + rows_blk, :] = u_ref[...].astype(F32)
            y = _silu(_conv_taps(cat_scr.at[s], cw_ref[...], rows_blk))
            convs.append(y if s == 2 else _l2norm_heads(y, HEAD ** -0.5 if s == 0 else 1.0))
            cat_scr[s, CONV_PAD - tail:CONV_PAD, :] = cat_scr[s, CONV_PAD + rows_blk - tail:CONV_PAD + rows_blk, :]

    ba = ba_ref[...]
    dt_row = par_ref[0:1, :]
    alog_row = par_ref[1:2, :]
    norm_row = par_ref[2:3, 0:HEAD]
    beta_all = _sigmoid(ba)
    xa = ba + dt_row
    softplus = jnp.maximum(xa, 0.0) + jnp.log(1.0 + jnp.exp(-jnp.abs(xa)))
    g_all = -jnp.exp(alog_row) * softplus

    r = lax.broadcasted_iota(jnp.int32, (chunk, chunk), 0)
    cc = lax.broadcasted_iota(jnp.int32, (chunk, chunk), 1)
    causal = r >= cc
    strict = r > cc
    eye = jnp.where(r == cc, 1.0, 0.0).astype(F32)
    tril = jnp.where(causal, 1.0, 0.0).astype(F32)
    blk = {}
    s = 8
    while s <= chunk:
        blk[s] = jnp.bitwise_xor(r, cc) < s
        s *= 2

    heads = range(hb)
    pairs = [(ci, hh) for ci in range(cps) for hh in heads]
    rsl = [slice(ci * chunk, (ci + 1) * chunk) for ci in range(cps)]
    csl = [slice(hh * HEAD, (hh + 1) * HEAD) for hh in heads]
    gcum_all = [_hdot(tril, g_all[rs, :]) for rs in rsl]
    gcum_t = [g.T for g in gcum_all]

    q, k, v, beta, gcum, g_last, decay = [], [], [], [], [], [], []
    for ci, hh in pairs:
        rs, cs = rsl[ci], csl[hh]
        q.append(convs[0][rs, cs])
        k.append(convs[1][rs, cs])
        v.append(convs[2][rs, cs])
        beta.append(beta_all[rs, hh:hh + 1])
        gc = gcum_all[ci][:, hb + hh:hb + hh + 1]
        gr = gcum_t[ci][hb + hh:hb + hh + 1, :]
        gcum.append(gc)
        g_last.append(gc[chunk - 1:chunk, :])
        decay.append(jnp.where(causal, jnp.exp(jnp.where(causal, gc - gr, 0.0)), 0.0))
    idx = range(len(pairs))
    e_g = [jnp.exp(gcum[i]) for i in idx]
    kb = [k[i] * beta[i] for i in idx]
    k_b16 = [k[i].astype(BF16) for i in idx]
    kk_mat = [_dot_nt(kb[i].astype(BF16), k_b16[i]) for i in idx]
    qk = [_dot_nt(q[i].astype(BF16), k_b16[i]) * decay[i] for i in idx]
    a_mat = [jnp.where(strict, kk_mat[i] * decay[i], 0.0) for i in idx]
    t_inv = _tri_inverse(a_mat, eye, blk, chunk)
    uw = [_bdot(t_inv[i], jnp.concatenate([v[i] * beta[i], kb[i] * e_g[i]], axis=1)) for i in idx]
    q_g = [(q[i] * e_g[i]).astype(BF16) for i in idx]
    k_tail = [(k[i] * jnp.exp(g_last[i] - gcum[i])).astype(BF16) for i in idx]
    qk = [x.astype(BF16) for x in qk]

    for ci in range(cps):
        ids = [ci * hb + hh for hh in heads]
        state = [s_scr[hh] for hh in heads]
        state_b = [x.astype(BF16) for x in state]
        ws = [_dot(uw[i][:, HEAD:2 * HEAD].astype(BF16), state_b[hh]) for hh, i in zip(heads, ids)]
        u = [uw[i][:, 0:HEAD] - ws[hh] for hh, i in zip(heads, ids)]
        u_b = [x.astype(BF16) for x in u]
        qs = [_dot(q_g[i], state_b[hh]) for hh, i in zip(heads, ids)]
        ku = [_dot_tn(k_tail[i], u_b[hh]) for hh, i in zip(heads, ids)]
        qu = [_dot(qk[i], u_b[hh]) for hh, i in zip(heads, ids)]
        for hh, i in zip(heads, ids):
            s_scr[hh] = jnp.exp(g_last[i]) * state[hh] + ku[hh]
        for hh in heads:
            rs, cs = rsl[ci], csl[hh]
            o = qs[hh] + qu[hh]
            o = o * lax.rsqrt(jnp.mean(o * o, axis=-1, keepdims=True) + LN_EPS) * norm_row
            o = o * _silu(z_ref[rs, cs].astype(F32))
            ma_ref[rs, cs] = (_sigmoid(ga_ref[rs, cs].astype(F32)) * o).astype(ma_ref.dtype)

    @pl.when(c == pl.num_programs(2) - 1)
    def _():
        sn_ref[...] = s_scr[...]


def _delta_call(main, ba, conv_w, par, conv_buf, s0, layer, nb, t, chunk, cps, hb, pre_conv):
    n = main.shape[0]
    d = main.shape[1] // N_MAIN_TILES
    h = d // HEAD
    ng = h // hb
    wb = hb * HEAD
    rows_blk = chunk * cps
    ncb = t // rows_blk
    grid = (nb, ng, ncb)

    def seg(s):
        return pl.BlockSpec((rows_blk, wb), lambda b, g, c: (b * ncb + c, s * ng + g))

    def cw(s):
        return pl.BlockSpec((None, CONV_W, wb), lambda b, g, c: (layer, 0, s * ng + g))

    def cb(s):
        return pl.BlockSpec((None, CONV_W - 1, wb), lambda b, g, c: (b, 0, s * ng + g))

    in_specs = [seg(0), seg(1), seg(2), seg(3), seg(4),
                pl.BlockSpec((rows_blk, LANES), lambda b, g, c: (b * ncb + c, g)),
                cw(0), cw(1), cw(2),
                pl.BlockSpec((None, 8, LANES), lambda b, g, c: (layer, 0, g)),
                cb(0), cb(1), cb(2),
                pl.BlockSpec((None, hb, HEAD, HEAD), lambda b, g, c: (b, g, 0, 0))]
    out_shape = [jax.ShapeDtypeStruct((n, d), BF16),
                 jax.ShapeDtypeStruct((nb, h, HEAD, HEAD), F32)]
    out_specs = [pl.BlockSpec((rows_blk, wb), lambda b, g, c: (b * ncb + c, g)),
                 pl.BlockSpec((None, hb, HEAD, HEAD), lambda b, g, c: (b, g, 0, 0))]
    return pl.pallas_call(
        functools.partial(_delta_kernel, chunk=chunk, cps=cps, hb=hb, pre_conv=pre_conv),
        out_shape=out_shape,
        grid=grid,
        in_specs=in_specs,
        out_specs=out_specs,
        scratch_shapes=[pltpu.VMEM((hb, HEAD, HEAD), F32),
                        pltpu.VMEM((3, CONV_PAD + rows_blk, wb), F32)],
        compiler_params=_cparams(("parallel", "parallel", "arbitrary")),
        name="delta",
    )(main, main, main, main, main, ba, conv_w, conv_w, conv_w, par,
      conv_buf, conv_buf, conv_buf, s0)


def _bucket_thresholds():
    nb = N_BUCKETS // 2
    max_exact = nb // 2
    ratio = MAX_DIST // max_exact
    steps = nb - max_exact
    out = []
    for kk in range(1, steps):
        n = max_exact
        while n ** steps < (max_exact ** steps) * (ratio ** kk):
            n += 1
        out.append(n)
    return out


def _far_distance():
    return _bucket_thresholds()[-1]


def _rel_bias_tile(tab_ref, head, qpos, kpos):
    nb = N_BUCKETS // 2
    max_exact = nb // 2
    rel = kpos - qpos
    n = jnp.abs(rel)
    large = jnp.full(rel.shape, max_exact, jnp.int32)
    for thr in _bucket_thresholds():
        large = large + jnp.where(n >= thr, 1, 0)
    bucket = jnp.where(rel > 0, nb, 0) + jnp.where(n < max_exact, n, large)
    far = tab_ref[nb - 1, head]
    bias = jnp.zeros(rel.shape, F32)
    for b in range(N_BUCKETS):
        bias = jnp.where(bucket == b, (tab_ref[b, head] - far) * LOG2E, bias)
    shift = CHUNK.bit_length() - 1
    mask = lax.shift_right_logical(kpos, shift) <= lax.shift_right_logical(qpos, shift)
    return jnp.where(mask, bias, NEG)


def _bias_kernel(tab_ref, pt_ref, st_ref, *, tq, past, ts):
    head = pl.program_id(0)
    ki = lax.broadcasted_iota(jnp.int32, (tq, 2 * tq), 0)
    qi = lax.broadcasted_iota(jnp.int32, (tq, 2 * tq), 1)
    qi = jnp.where(qi >= tq, qi - tq, qi)
    pt_ref[0] = _rel_bias_tile(tab_ref, head, qi + tq, ki + tq)
    pt_ref[1] = _rel_bias_tile(tab_ref, head, qi + tq, ki)
    wk = past + LANES
    qs = lax.broadcasted_iota(jnp.int32, (ts, wk), 0) + past
    ks = lax.broadcasted_iota(jnp.int32, (ts, wk), 1)
    st = _rel_bias_tile(tab_ref, head, qs, ks)
    st_ref[...] = jnp.where(ks < past + ts, st, NEG)


def _bias_call(rel_bias, tq, past, ts):
    h = rel_bias.shape[1]
    wk = past + LANES
    return pl.pallas_call(
        functools.partial(_bias_kernel, tq=tq, past=past, ts=ts),
        out_shape=[jax.ShapeDtypeStruct((h, 2, tq, 2 * tq), F32),
                   jax.ShapeDtypeStruct((h, ts, wk), F32)],
        grid=(h,),
        in_specs=[pl.BlockSpec(memory_space=pltpu.SMEM)],
        out_specs=[pl.BlockSpec((None, 2, tq, 2 * tq), lambda i: (i, 0, 0, 0)),
                   pl.BlockSpec((None, ts, wk), lambda i: (i, 0, 0))],
        compiler_params=_cparams(("arbitrary",)),
        name="rel_bias",
    )(rel_bias)


def _lam_value(lam_ref, lam_init):
    lp = lam_ref[...]
    s1 = jnp.sum(lp[0:1, :] * lp[1:2, :], axis=-1, keepdims=True)
    s2 = jnp.sum(lp[2:3, :] * lp[3:4, :], axis=-1, keepdims=True)
    return jnp.exp(s1) - jnp.exp(s2) + lam_init


def _stack_maps(q):
    lane = lax.broadcasted_iota(jnp.int32, q.shape, 1)
    zero = jnp.zeros_like(q)
    half = HEAD // 2
    return jnp.concatenate([jnp.where(lane < half, q, zero), jnp.where(lane >= half, q, zero)], axis=0)


def _merge_out(o, lam_init, subln, ma, gb):
    o = o * lax.rsqrt(jnp.mean(o * o, axis=-1, keepdims=True) + LN_EPS) * subln * (1.0 - lam_init)
    return (ma.astype(F32) + _sigmoid(gb.astype(F32)) * o).astype(BF16)


def _attn_scores(n_blocks, q2, k_ref, bias_ref, tq):
    kv = n_blocks * tq
    pieces = []
    if n_blocks > 2:
        pieces.append((0, kv - 2 * tq, None))
    if n_blocks >= 2:
        pieces.append((kv - 2 * tq, tq, 1))
    pieces.append((kv - tq, tq, 0))
    out = []
    for start, size, tile in pieces:
        s = _dot_nt(k_ref[start:start + size, :], q2)
        if tile is not None:
            s = s + bias_ref[tile]
        out.append((start, size, s))
    return out


def _attn_values(scored, vt):
    m = None
    for _, _, s in scored:
        mx = jnp.max(s, axis=0, keepdims=True)
        m = mx if m is None else jnp.maximum(m, mx)
    l = None
    acc = None
    for start, size, s in scored:
        p = jnp.exp2(s - m)
        ls = jnp.sum(p, axis=0, keepdims=True)
        pv = _dot(vt[:, start:start + size], p.astype(BF16))
        l = ls if l is None else l + ls
        acc = pv if acc is None else acc + pv
    return acc, l


def _attn_kernel(q_ref, k_ref, v_ref, bias_ref, lam_ref, subln_ref, ma_ref, gb_ref,
                 out_ref, vt, *, tq, nq, lam_init):
    vt[...] = v_ref[...].T
    lam_val = _lam_value(lam_ref, lam_init)
    subln = subln_ref[...]

    def scores(n):
        return _attn_scores(n, _stack_maps(q_ref[(n - 1) * tq:n * tq, :]), k_ref, bias_ref, tq)

    nxt = scores(1)
    for n in range(1, nq + 1):
        cur = nxt
        if n < nq:
            nxt = scores(n + 1)
        rows = slice((n - 1) * tq, n * tq)
        acc, l = _attn_values(cur, vt)
        o_t = acc[:, 0:tq] / l[:, 0:tq] - lam_val * (acc[:, tq:2 * tq] / l[:, tq:2 * tq])
        out_ref[rows, :] = _merge_out(o_t.T, lam_init, subln, ma_ref[rows, :], gb_ref[rows, :])


def _attn_call(q, k, v, bias, lam, subln, ma, main, layer, nb, t, tq, lam_init):
    n, d = q.shape
    h = d // HEAD
    nq = t // tq
    assert tq + 1 >= _far_distance() and tq % CHUNK == 0
    gb_col = (N_MAIN_TILES - 1) * h
    col = pl.BlockSpec((t, HEAD), lambda b, hh: (b, hh))
    in_specs = [
        col, col, col,
        pl.BlockSpec((None, 2, tq, 2 * tq), lambda b, hh: (hh, 0, 0, 0)),
        pl.BlockSpec((None, 4, HEAD // 2), lambda b, hh: (layer, 0, 0)),
        pl.BlockSpec((None, 1, HEAD), lambda b, hh: (layer, 0, 0)),
        col,
        pl.BlockSpec((t, HEAD), lambda b, hh: (b, gb_col + hh)),
    ]
    return pl.pallas_call(
        functools.partial(_attn_kernel, tq=tq, nq=nq, lam_init=lam_init),
        out_shape=jax.ShapeDtypeStruct((n, d), BF16),
        grid=(nb, h),
        in_specs=in_specs,
        out_specs=col,
        scratch_shapes=[pltpu.VMEM((HEAD, t), BF16)],
        compiler_params=_cparams(("parallel", "parallel")),
        name="attn",
    )(q, k, v, bias, lam, subln, ma, main)


def _attn_step_kernel(q_ref, kc_ref, vc_ref, kn_ref, vn_ref, bias_ref, lam_ref, subln_ref,
                      ma_ref, gb_ref, out_ref, *, ts, past, lam_init):
    n_heads = kc_ref.shape[1]
    d = n_heads * HEAD
    kc_all = kc_ref[...].reshape(past, d).astype(BF16)
    vc_all = vc_ref[...].reshape(past, d).astype(BF16)
    pad = jnp.zeros((LANES - ts, HEAD), BF16)
    lam_val = _lam_value(lam_ref, lam_init)
    subln = subln_ref[...]
    for hh in range(n_heads):
        cs = slice(hh * HEAD, (hh + 1) * HEAD)
        q2 = _stack_maps(q_ref[:, cs])
        kc = kc_all[:, cs]
        vc = vc_all[:, cs]
        kn = jnp.concatenate([kn_ref[:, cs], pad], axis=0)
        vn = jnp.concatenate([vn_ref[:, cs], pad], axis=0)
        bias = bias_ref[hh]
        bias2 = jnp.concatenate([bias, bias], axis=0)
        s_c = _dot_nt(q2, kc) + bias2[:, 0:past]
        s_n = _dot_nt(q2, kn) + bias2[:, past:past + LANES]
        m = jnp.maximum(jnp.max(s_c, axis=-1, keepdims=True), jnp.max(s_n, axis=-1, keepdims=True))
        p_c = jnp.exp2(s_c - m)
        p_n = jnp.exp2(s_n - m)
        l = jnp.sum(p_c, axis=-1, keepdims=True) + jnp.sum(p_n, axis=-1, keepdims=True)
        acc = _dot(p_c.astype(BF16), vc) + _dot(p_n.astype(BF16), vn)
        o = acc[0:ts] / l[0:ts] - lam_val * (acc[ts:2 * ts] / l[ts:2 * ts])
        out_ref[:, cs] = _merge_out(o, lam_init, subln, ma_ref[:, cs], gb_ref[:, cs])


def _attn_step_call(q, cache_k, cache_v, k_new, v_new, bias, lam, subln, ma, main, layer, ts, lam_init):
    n, d = q.shape
    _, nb, past, h, _ = cache_k.shape
    wk = past + LANES
    row = pl.BlockSpec((ts, d), lambda b: (b, 0))
    cache = pl.BlockSpec((None, None, past, h, HEAD), lambda b: (layer, b, 0, 0, 0))
    in_specs = [
        row, cache, cache, row, row,
        _resident((h, ts, wk), lambda b: (0, 0, 0)),
        pl.BlockSpec((None, 4, HEAD // 2), lambda b: (layer, 0, 0)),
        pl.BlockSpec((None, 1, HEAD), lambda b: (layer, 0, 0)),
        row,
        pl.BlockSpec((ts, d), lambda b: (b, N_MAIN_TILES - 1)),
    ]
    return pl.pallas_call(
        functools.partial(_attn_step_kernel, ts=ts, past=past, lam_init=lam_init),
        out_shape=jax.ShapeDtypeStruct((n, d), BF16),
        grid=(nb,),
        in_specs=in_specs,
        out_specs=row,
        compiler_params=_cparams(("parallel",)),
        name="attn_step",
    )(q, cache_k, cache_v, k_new, v_new, bias, lam, subln, ma, main)


def _wo_ffn_kernel(m_ref, x_ref, gt1_ref, sc_ref, sh_ref, gt2_ref, w_o_ref, wi_ref, wo_ref,
                   g1_ref, b1_ref, g2_ref, b2_ref, o_ref, *, alpha, nf):
    y = _dot(m_ref[...], w_o_ref[...])
    x = _layer_norm(alpha * x_ref[...] + gt1_ref[...] * y, g1_ref[...], b1_ref[...])
    h = (x * (1.0 + sc_ref[...]) + sh_ref[...]).astype(BF16)
    dff = wo_ref.shape[0]
    tf = dff // nf
    y = None
    for j in range(nf):
        u = _dot(h, wi_ref[:, j * tf:(j + 1) * tf])
        v = _dot(h, wi_ref[:, dff + j * tf:dff + (j + 1) * tf])
        part = _dot((_silu(u) * v).astype(BF16), wo_ref[j * tf:(j + 1) * tf, :])
        y = part if y is None else y + part
    o_ref[...] = _layer_norm(alpha * x + gt2_ref[...] * y, g2_ref[...], b2_ref[...])


def _wo_ffn_call(merged, x, mod, w_o, w_ff_in, w_ff_out, ln1_g, ln1_b, ln2_g, ln2_b, layer, tm,
                 tiles_per_group, alpha, nf):
    n, d = x.shape
    dff = w_ff_out.shape[1]
    vec = pl.BlockSpec((None, 1, d), lambda i: (layer, 0, 0))
    rows = pl.BlockSpec((tm, d), lambda i: (i, 0))
    return pl.pallas_call(
        functools.partial(_wo_ffn_kernel, alpha=alpha, nf=nf),
        out_shape=jax.ShapeDtypeStruct((n, d), F32),
        grid=(n // tm,),
        in_specs=[rows, rows,
                  _mod_spec(mod, d, tiles_per_group, 2),
                  _mod_spec(mod, d, tiles_per_group, 4),
                  _mod_spec(mod, d, tiles_per_group, 3),
                  _mod_spec(mod, d, tiles_per_group, 5),
                  _resident((None, d, d), lambda i: (layer, 0, 0)),
                  _resident((None, d, 2 * dff), lambda i: (layer, 0, 0)),
                  _resident((None, dff, d), lambda i: (layer, 0, 0)),
                  vec, vec, vec, vec],
        out_specs=rows,
        compiler_params=_cparams(("parallel",)),
        name="wo_ffn",
    )(merged, x, mod, mod, mod, mod, w_o, w_ff_in, w_ff_out, ln1_g, ln1_b, ln2_g, ln2_b)


def _pick_tile(n, pref):
    t = min(n, pref)
    while n % t:
        t //= 2
    return t


def kernel(x_prompt, x_sample, cache_k, cache_v, state_conv, state_delta, c_prompt, c_sample,
           ln_in_g, ln_in_b, rel_bias, w_ada, b_ada, w_in, conv_w, a_log, dt_bias, norm_a,
           lam, subln_g, w_o, ln1_g, ln1_b, w_ff_in, w_ff_out, ln2_g, ln2_b):
    bp, tp, d = x_prompt.shape
    bs, ts, _ = x_sample.shape
    depth = w_in.shape[0]
    h = d // HEAD
    past = cache_k.shape[2]
    dff = w_ff_out.shape[1]
    n_p = bp * tp
    n_s = bs * ts
    alpha = (2 * depth) ** 0.25
    hb = h
    ng = h // hb

    o1 = 4 * d + 2 * h
    qkvz = w_in[:, :, 0:4 * d]
    gates = w_in[:, :, o1 + 3 * d:o1 + 5 * d]
    qb = w_in[:, :, o1:o1 + d] * ((HEAD // 2) ** -0.5 * LOG2E)
    kvb = w_in[:, :, o1 + d:o1 + 3 * d]
    w_all = jnp.concatenate([qkvz, gates, qb, kvb], axis=-1).astype(BF16)
    wb = w_in[:, :, 4 * d:4 * d + h].reshape(depth, d, ng, hb)
    wa = w_in[:, :, 4 * d + h:4 * d + 2 * h].reshape(depth, d, ng, hb)
    w_ba = jnp.concatenate([wb, wa, jnp.zeros((depth, d, ng, LANES - 2 * hb), F32)], axis=-1)
    w_ba = w_ba.reshape(depth, d, ng * LANES).astype(BF16)
    w_ada_b = w_ada.astype(BF16)
    w_o_b = w_o.astype(BF16)
    w_ff_in_b = w_ff_in.astype(BF16)
    w_ff_out_b = w_ff_out.astype(BF16)

    def lane_rows(v):
        v = v.reshape(depth, ng, hb)
        z = jnp.zeros((depth, ng, hb), F32)
        zz = jnp.zeros((depth, ng, LANES - 2 * hb), F32)
        return jnp.concatenate([z, v, zz], axis=-1).reshape(depth, ng * LANES)

    par = jnp.zeros((depth, 8, ng * LANES), F32)
    par = par.at[:, 0].set(lane_rows(dt_bias)).at[:, 1].set(lane_rows(a_log))
    par = par.at[:, 2].set(jnp.tile(norm_a, (1, ng)))
    subln = subln_g.reshape(depth, 1, HEAD)
    ln1g, ln1b = ln1_g.reshape(depth, 1, d), ln1_b.reshape(depth, 1, d)
    ln2g, ln2b = ln2_g.reshape(depth, 1, d), ln2_b.reshape(depth, 1, d)

    mod = _ada_call(jnp.concatenate([c_prompt, c_sample], axis=0), w_ada_b, b_ada)
    mod_p = mod[:, :bp].reshape(depth, bp, 1, 6 * d)
    mod_s = jnp.repeat(mod[:, bp:], ts, axis=1).reshape(depth, 1, n_s, 6 * d)

    bias_p, bias_s = _bias_call(rel_bias, TQ, past, ts)

    tm_p = _pick_tile(tp, TM_PROJ)
    tpg_p = tp // tm_p
    xp = _ln_call(x_prompt.reshape(n_p, d), ln_in_g, ln_in_b, tm_p)
    xs = _ln_call(x_sample.reshape(n_s, d), ln_in_g, ln_in_b, n_s)

    conv0 = jnp.zeros((bp, CONV_W - 1, 3 * d), F32)
    s0 = jnp.zeros((bp, h, HEAD, HEAD), F32)
    nf = 1

    kp, vp = (jnp.zeros((depth, n_p, h, HEAD), F32) for _ in range(2))
    ks, vs = (jnp.zeros((depth, n_s, h, HEAD), F32) for _ in range(2))
    conv_p, conv_s, st_p, st_s = [], [], [], []
    for l in range(depth):
        lam_init = 0.8 - 0.6 * math.exp(-0.3 * l)
        main, qb_p, kb_l, vb_l, kp, vp, ba, tail = _inproj_call(
            xp, mod_p[l], w_all, w_ba, conv_w, conv0, kp, vp, l, tm_p, tpg_p, BF16, True)
        ma, sn = _delta_call(main, ba, conv_w, par, conv0, s0, l, bp, tp, CHUNK, DELTA_CPS, hb, True)
        merged = _attn_call(qb_p, kb_l, vb_l, bias_p, lam, subln, ma, main, l, bp, tp, TQ, lam_init)
        xp = _wo_ffn_call(merged, xp, mod_p[l], w_o_b, w_ff_in_b, w_ff_out_b, ln1g, ln1b, ln2g, ln2b,
                          l, tm_p, tpg_p, alpha, nf)
        conv_p.append(tail.reshape(bp, tpg_p, CONV_PAD, 3 * d)[:, tpg_p - 1, CONV_PAD - (CONV_W - 1):, :])
        st_p.append(sn)
        main, qb_s, kb_l, vb_l, ks, vs, ba, _ = _inproj_call(
            xs, mod_s[l], w_all, w_ba, conv_w, state_conv[l], ks, vs, l, n_s, 1, F32, False)
        ma, sn = _delta_call(main, ba, conv_w, par, state_conv[l], state_delta[l], l, bs, ts, ts, 1, hb, False)
        merged = _attn_step_call(qb_s, cache_k, cache_v, kb_l, vb_l, bias_s, lam, subln, ma, main,
                                 l, ts, lam_init)
        xs = _wo_ffn_call(merged, xs, mod_s[l], w_o_b, w_ff_in_b, w_ff_out_b, ln1g, ln1b, ln2g, ln2b,
                          l, n_s, 1, alpha, nf)
        conv_s.append(main.reshape(bs, ts, -1)[:, ts - (CONV_W - 1):, 0:3 * d])
        st_s.append(sn)

    def heads_out(buf, nb, t):
        return buf.reshape(depth, nb, t, h, HEAD)

    return (xp.reshape(bp, tp, d), xs.reshape(bs, ts, d),
            heads_out(kp, bp, tp), heads_out(vp, bp, tp), jnp.stack(conv_p), jnp.stack(st_p),
            heads_out(ks, bs, ts), heads_out(vs, bs, ts), jnp.stack(conv_s), jnp.stack(st_s))
```

```python
import functools
import math

import jax
import jax.numpy as jnp
from jax import lax
from jax.experimental import pallas as pl
from jax.experimental.pallas import tpu as pltpu

F32 = jnp.float32
BF16 = jnp.bfloat16
HIGHEST = lax.Precision.HIGHEST

LN_EPS = 1e-5
CHUNK = 64
HEAD = 128
CONV_W = 4
N_BUCKETS = 32
MAX_DIST = 128
NEG = -1e30
LOG2E = math.log2(math.e)
LANES = 128
CONV_PAD = 8
VMEM_LIMIT = 56 * 1024 * 1024

TM_PROJ = 512
TQ = 256
DELTA_CPS = 8
ATTN_HPS = 2


def _cparams(sem):
    return pltpu.CompilerParams(dimension_semantics=sem, vmem_limit_bytes=VMEM_LIMIT)


def _sigmoid(x):
    return 1.0 / (1.0 + jnp.exp(-x))


def _silu(x):
    return x * _sigmoid(x)


def _layer_norm(x, g, b):
    mu = jnp.mean(x, axis=-1, keepdims=True)
    xc = x - mu
    var = jnp.mean(xc * xc, axis=-1, keepdims=True)
    return xc * lax.rsqrt(var + LN_EPS) * g + b


def _dot(a, b):
    return jnp.dot(a, b, preferred_element_type=F32)


def _hdot(a, b):
    return jnp.dot(a, b, preferred_element_type=F32, precision=HIGHEST)


def _bdot(a, b):
    return _dot(a.astype(BF16), b.astype(BF16))


def _dot_nt(a, b):
    return lax.dot_general(a, b, (((1,), (1,)), ((), ())), preferred_element_type=F32)


def _dot_tn(a, b):
    return lax.dot_general(a, b, (((0,), (0,)), ((), ())), preferred_element_type=F32)


def _ln_kernel(x_ref, g_ref, b_ref, o_ref):
    o_ref[...] = _layer_norm(x_ref[...], g_ref[...], b_ref[...])


def _ln_call(x, g, b, tm):
    n, d = x.shape
    return pl.pallas_call(
        _ln_kernel,
        out_shape=jax.ShapeDtypeStruct((n, d), F32),
        grid=(n // tm,),
        in_specs=[pl.BlockSpec((tm, d), lambda i: (i, 0)),
                  pl.BlockSpec((1, d), lambda i: (0, 0)),
                  pl.BlockSpec((1, d), lambda i: (0, 0))],
        out_specs=pl.BlockSpec((tm, d), lambda i: (i, 0)),
        compiler_params=_cparams(("parallel",)),
        name="ln_in",
    )(x, g.reshape(1, d), b.reshape(1, d))


def _ada_kernel(c_ref, w_ref, b_ref, o_ref):
    c = c_ref[...]
    o_ref[...] = _dot(_silu(c).astype(BF16), w_ref[...]) + b_ref[...]


def _ada_call(c_all, w_ada, b_ada):
    depth, d, d6 = w_ada.shape
    nb = c_all.shape[0]
    return pl.pallas_call(
        _ada_kernel,
        out_shape=jax.ShapeDtypeStruct((depth, nb, d6), F32),
        grid=(depth, d6 // d),
        in_specs=[pl.BlockSpec((nb, d), lambda l, j: (0, 0)),
                  pl.BlockSpec((None, d, d), lambda l, j: (l, 0, j)),
                  pl.BlockSpec((None, 1, d), lambda l, j: (l, 0, j))],
        out_specs=pl.BlockSpec((None, nb, d), lambda l, j: (l, 0, j)),
        compiler_params=_cparams(("parallel", "parallel")),
        name="ada",
    )(c_all, w_ada, b_ada.reshape(depth, 1, d6))


def _mod_spec(mod, d, tiles_per_group, chunk):
    rows = mod.shape[1]
    return pl.BlockSpec((None, rows, d), lambda i, *_: (i // tiles_per_group, 0, chunk))


N_MAIN_TILES = 6
N_W_TILES = 9
N_CONV_TILES = 3


def _conv_taps(cat_ref, cw, rows):
    acc = None
    for i in range(CONV_W):
        term = cat_ref[pl.ds(CONV_PAD - (CONV_W - 1) + i, rows), :] * cw[i:i + 1, :]
        acc = term if acc is None else acc + term
    return acc


def _conv_taps_rolled(prev, cur, cw):
    assert CONV_W == 4 and CONV_PAD >= 2 * (CONV_W - 1)
    rows = cur.shape[0]
    cat = jnp.concatenate([prev, cur], axis=0)
    cat1 = pltpu.roll(cat, 1, axis=0)
    near = cat * cw[3:4, :] + cat1 * cw[2:3, :]
    far = cat * cw[1:2, :] + cat1 * cw[0:1, :]
    return (near + pltpu.roll(far, 2, axis=0))[CONV_PAD:CONV_PAD + rows, :]


def _l2norm_heads(y, scale):
    outs = []
    for hh in range(y.shape[1] // HEAD):
        seg = y[:, hh * HEAD:(hh + 1) * HEAD]
        outs.append(seg * (lax.rsqrt(jnp.sum(seg * seg, axis=-1, keepdims=True) + 1e-6) * scale))
    return jnp.concatenate(outs, axis=1)


def _inproj_kernel(x_ref, sc_ref, sh_ref, w_ref, wba_ref, cw_ref, cb_ref, kin_ref, vin_ref,
                   main_ref, q_ref, kb_ref, vb_ref, k_ref, v_ref, ba_ref, tail_ref,
                   carry_scr, *, fuse_conv, tiles_per_group):
    tm, d = x_ref.shape
    h = (x_ref[...] * (1.0 + sc_ref[...]) + sh_ref[...]).astype(BF16)

    def cols(j):
        return _dot(h, w_ref[:, j * d:(j + 1) * d])

    if fuse_conv:
        @pl.when(pl.program_id(0) % tiles_per_group == 0)
        def _():
            for j in range(N_CONV_TILES):
                carry_scr[j, CONV_PAD - (CONV_W - 1):CONV_PAD, :] = cb_ref[:, j * d:(j + 1) * d]

    def finish(j, res):
        if j < N_CONV_TILES:
            last = res[tm - CONV_PAD:tm, :]
            tail_ref[:, j * d:(j + 1) * d] = last
            if fuse_conv:
                prev = carry_scr[j]
                carry_scr[j] = last
                res = _silu(_conv_taps_rolled(prev, res, cw_ref[:, j * d:(j + 1) * d]))
                if j == 0:
                    res = _l2norm_heads(res, HEAD ** -0.5)
                elif j == 1:
                    res = _l2norm_heads(res, 1.0)
        if j < N_MAIN_TILES:
            main_ref[:, j * d:(j + 1) * d] = res.astype(main_ref.dtype)
        elif j == N_MAIN_TILES:
            q_ref[...] = res.astype(BF16)
        else:
            b_ref, o_ref = ((kb_ref, k_ref), (vb_ref, v_ref))[j - N_MAIN_TILES - 1]
            b_ref[...] = res.astype(BF16)
            o_ref[...] = res.reshape(o_ref.shape)

    order = []
    for j in range(N_CONV_TILES):
        order += [j, N_CONV_TILES + j]
    order += list(range(2 * N_CONV_TILES, N_W_TILES))
    nxt = cols(order[0])
    for pos, j in enumerate(order):
        res = nxt
        if pos + 1 < len(order):
            nxt = cols(order[pos + 1])
        finish(j, res)
    ba_ref[...] = _dot(h, wba_ref[...])


def _resident(block_shape, index_map):
    return pl.BlockSpec(block_shape, index_map, pipeline_mode=pl.Buffered(1))


def _inproj_call(x, mod, w_all, w_ba, conv_w, conv_buf, kbuf, vbuf, layer, tm, tiles_per_group,
                 main_dtype, fuse_conv):
    n, d = x.shape
    nba = w_ba.shape[-1]
    wide = N_W_TILES * d
    cwide = N_CONV_TILES * d
    in_specs = [
        pl.BlockSpec((tm, d), lambda i: (i, 0)),
        _mod_spec(mod, d, tiles_per_group, 1),
        _mod_spec(mod, d, tiles_per_group, 0),
        _resident((None, d, wide), lambda i: (layer, 0, 0)),
        _resident((None, d, nba), lambda i: (layer, 0, 0)),
        _resident((None, CONV_W, cwide), lambda i: (layer, 0, 0)),
        pl.BlockSpec((None, CONV_W - 1, cwide), lambda i: (i // tiles_per_group if fuse_conv else 0, 0, 0)),
        pl.BlockSpec(memory_space=pl.ANY),
        pl.BlockSpec(memory_space=pl.ANY),
    ]
    out_shape = [
        jax.ShapeDtypeStruct((n, N_MAIN_TILES * d), main_dtype),
        jax.ShapeDtypeStruct((n, d), BF16),
        jax.ShapeDtypeStruct((n, d), BF16),
        jax.ShapeDtypeStruct((n, d), BF16),
        jax.ShapeDtypeStruct(kbuf.shape, F32),
        jax.ShapeDtypeStruct(vbuf.shape, F32),
        jax.ShapeDtypeStruct((n, nba), F32),
        jax.ShapeDtypeStruct((n // tm, CONV_PAD, N_CONV_TILES * d), F32),
    ]
    out_specs = [
        pl.BlockSpec((tm, N_MAIN_TILES * d), lambda i: (i, 0)),
        pl.BlockSpec((tm, d), lambda i: (i, 0)),
        pl.BlockSpec((tm, d), lambda i: (i, 0)),
        pl.BlockSpec((tm, d), lambda i: (i, 0)),
        pl.BlockSpec((None, tm, d // HEAD, HEAD), lambda i: (layer, i, 0, 0)),
        pl.BlockSpec((None, tm, d // HEAD, HEAD), lambda i: (layer, i, 0, 0)),
        pl.BlockSpec((tm, nba), lambda i: (i, 0)),
        pl.BlockSpec((None, CONV_PAD, N_CONV_TILES * d), lambda i: (i, 0, 0)),
    ]
    return pl.pallas_call(
        functools.partial(_inproj_kernel, fuse_conv=fuse_conv, tiles_per_group=tiles_per_group),
        out_shape=out_shape,
        grid=(n // tm,),
        in_specs=in_specs,
        out_specs=out_specs,
        scratch_shapes=[pltpu.VMEM((N_CONV_TILES, CONV_PAD, d), F32)],
        input_output_aliases={7: 4, 8: 5},
        compiler_params=_cparams(("arbitrary",)),
        name="inproj",
    )(x, mod, mod, w_all, w_ba, conv_w, conv_buf, kbuf, vbuf)


def _tri_inverse(a_list, eye, blk, size):
    s = 8
    d = [jnp.where(blk[s], a, 0.0) for a in a_list]
    d2 = [_bdot(x, x) for x in d]
    p = [_bdot(eye - x, eye + y) for x, y in zip(d, d2)]
    d4 = [_bdot(x, x) for x in d2]
    t = [_bdot(x, eye + y) for x, y in zip(p, d4)]
    while s < size:
        lower = jnp.logical_and(blk[2 * s], jnp.logical_not(blk[s]))
        off = [jnp.where(lower, a, 0.0) for a in a_list]
        x = [_bdot(ti, oi) for ti, oi in zip(t, off)]
        t = [ti - _bdot(xi, ti) for ti, xi in zip(t, x)]
        s *= 2
    return t


def _delta_kernel(qu_ref, ku_ref, vu_ref, z_ref, ga_ref, ba_ref,
                  cwq_ref, cwk_ref, cwv_ref, par_ref, cbq_ref, cbk_ref, cbv_ref, s0_ref,
                  ma_ref, sn_ref, s_scr, cat_scr, *, chunk, cps, hb, pre_conv):
    c = pl.program_id(2)
    rows_blk = chunk * cps
    tail = CONV_W - 1

    @pl.when(c == 0)
    def _():
        s_scr[...] = s0_ref[...]
        if not pre_conv:
            for s, cb in enumerate((cbq_ref, cbk_ref, cbv_ref)):
                cat_scr[s, CONV_PAD - tail:CONV_PAD, :] = cb[...]

    if pre_conv:
        convs = [u_ref[...].astype(F32) for u_ref in (qu_ref, ku_ref, vu_ref)]
    else:
        convs = []
        for s, (u_ref, cw_ref) in enumerate(((qu_ref, cwq_ref), (ku_ref, cwk_ref), (vu_ref, cwv_ref))):
            cat_scr[s, CONV_PAD:CONV_PAD + rows_blk, :] = u_ref[...].astype(F32)
            y = _silu(_conv_taps(cat_scr.at[s], cw_ref[...], rows_blk))
            convs.append(y if s == 2 else _l2norm_heads(y, HEAD ** -0.5 if s == 0 else 1.0))
            cat_scr[s, CONV_PAD - tail:CONV_PAD, :] = cat_scr[s, CONV_PAD + rows_blk - tail:CONV_PAD + rows_blk, :]

    ba = ba_ref[...]
    dt_row = par_ref[0:1, :]
    alog_row = par_ref[1:2, :]
    norm_row = par_ref[2:3, 0:HEAD]
    beta_all = _sigmoid(ba)
    xa = ba + dt_row
    softplus = jnp.maximum(xa, 0.0) + jnp.log(1.0 + jnp.exp(-jnp.abs(xa)))
    g_all = -jnp.exp(alog_row) * softplus

    r = lax.broadcasted_iota(jnp.int32, (chunk, chunk), 0)
    cc = lax.broadcasted_iota(jnp.int32, (chunk, chunk), 1)
    causal = r >= cc
    strict = r > cc
    eye = jnp.where(r == cc, 1.0, 0.0).astype(F32)
    tril = jnp.where(causal, 1.0, 0.0).astype(F32)
    blk = {}
    s = 8
    while s <= chunk:
        blk[s] = jnp.bitwise_xor(r, cc) < s
        s *= 2

    heads = range(hb)
    pairs = [(ci, hh) for ci in range(cps) for hh in heads]
    rsl = [slice(ci * chunk, (ci + 1) * chunk) for ci in range(cps)]
    csl = [slice(hh * HEAD, (hh + 1) * HEAD) for hh in heads]
    gcum_all = [_hdot(tril, g_all[rs, :]) for rs in rsl]
    gcum_t = [g.T for g in gcum_all]

    q, k, v, beta, gcum, g_last, decay = [], [], [], [], [], [], []
    for ci, hh in pairs:
        rs, cs = rsl[ci], csl[hh]
        q.append(convs[0][rs, cs])
        k.append(convs[1][rs, cs])
        v.append(convs[2][rs, cs])
        beta.append(beta_all[rs, hh:hh + 1])
        gc = gcum_all[ci][:, hb + hh:hb + hh + 1]
        gr = gcum_t[ci][hb + hh:hb + hh + 1, :]
        gcum.append(gc)
        g_last.append(gc[chunk - 1:chunk, :])
        decay.append(jnp.where(causal, jnp.exp(jnp.where(causal, gc - gr, 0.0)), 0.0))
    idx = range(len(pairs))
    e_g = [jnp.exp(gcum[i]) for i in idx]
    kb = [k[i] * beta[i] for i in idx]
    k_b16 = [k[i].astype(BF16) for i in idx]
    kk_mat = [_dot_nt(kb[i].astype(BF16), k_b16[i]) for i in idx]
    qk = [_dot_nt(q[i].astype(BF16), k_b16[i]) * decay[i] for i in idx]
    a_mat = [jnp.where(strict, kk_mat[i] * decay[i], 0.0) for i in idx]
    t_inv = _tri_inverse(a_mat, eye, blk, chunk)
    uw = [_bdot(t_inv[i], jnp.concatenate([v[i] * beta[i], kb[i] * e_g[i]], axis=1)) for i in idx]
    q_g = [(q[i] * e_g[i]).astype(BF16) for i in idx]
    k_tail = [(k[i] * jnp.exp(g_last[i] - gcum[i])).astype(BF16) for i in idx]
    qk = [x.astype(BF16) for x in qk]

    for ci in range(cps):
        ids = [ci * hb + hh for hh in heads]
        state = [s_scr[hh] for hh in heads]
        state_b = [x.astype(BF16) for x in state]
        ws = [_dot(uw[i][:, HEAD:2 * HEAD].astype(BF16), state_b[hh]) for hh, i in zip(heads, ids)]
        u = [uw[i][:, 0:HEAD] - ws[hh] for hh, i in zip(heads, ids)]
        u_b = [x.astype(BF16) for x in u]
        qs = [_dot(q_g[i], state_b[hh]) for hh, i in zip(heads, ids)]
        ku = [_dot_tn(k_tail[i], u_b[hh]) for hh, i in zip(heads, ids)]
        qu = [_dot(qk[i], u_b[hh]) for hh, i in zip(heads, ids)]
        for hh, i in zip(heads, ids):
            s_scr[hh] = jnp.exp(g_last[i]) * state[hh] + ku[hh]
        for hh in heads:
            rs, cs = rsl[ci], csl[hh]
            o = qs[hh] + qu[hh]
            o = o * lax.rsqrt(jnp.mean(o * o, axis=-1, keepdims=True) + LN_EPS) * norm_row
            o = o * _silu(z_ref[rs, cs].astype(F32))
            ma_ref[rs, cs] = (_sigmoid(ga_ref[rs, cs].astype(F32)) * o).astype(ma_ref.dtype)

    @pl.when(c == pl.num_programs(2) - 1)
    def _():
        sn_ref[...] = s_scr[...]


def _delta_call(main, ba, conv_w, par, conv_buf, s0, layer, nb, t, chunk, cps, hb, pre_conv):
    n = main.shape[0]
    d = main.shape[1] // N_MAIN_TILES
    h = d // HEAD
    ng = h // hb
    wb = hb * HEAD
    rows_blk = chunk * cps
    ncb = t // rows_blk
    grid = (nb, ng, ncb)

    def seg(s):
        return pl.BlockSpec((rows_blk, wb), lambda b, g, c: (b * ncb + c, s * ng + g))

    def cw(s):
        return pl.BlockSpec((None, CONV_W, wb), lambda b, g, c: (layer, 0, s * ng + g))

    def cb(s):
        return pl.BlockSpec((None, CONV_W - 1, wb), lambda b, g, c: (b, 0, s * ng + g))

    in_specs = [seg(0), seg(1), seg(2), seg(3), seg(4),
                pl.BlockSpec((rows_blk, LANES), lambda b, g, c: (b * ncb + c, g)),
                cw(0), cw(1), cw(2),
                pl.BlockSpec((None, 8, LANES), lambda b, g, c: (layer, 0, g)),
                cb(0), cb(1), cb(2),
                pl.BlockSpec((None, hb, HEAD, HEAD), lambda b, g, c: (b, g, 0, 0))]
    out_shape = [jax.ShapeDtypeStruct((n, d), BF16),
                 jax.ShapeDtypeStruct((nb, h, HEAD, HEAD), F32)]
    out_specs = [pl.BlockSpec((rows_blk, wb), lambda b, g, c: (b * ncb + c, g)),
                 pl.BlockSpec((None, hb, HEAD, HEAD), lambda b, g, c: (b, g, 0, 0))]
    return pl.pallas_call(
        functools.partial(_delta_kernel, chunk=chunk, cps=cps, hb=hb, pre_conv=pre_conv),
        out_shape=out_shape,
        grid=grid,
        in_specs=in_specs,
        out_specs=out_specs,
        scratch_shapes=[pltpu.VMEM((hb, HEAD, HEAD), F32),
                        pltpu.VMEM((3, CONV_PAD + rows_blk, wb), F32)],
        compiler_params=_cparams(("parallel", "parallel", "arbitrary")),
        name="delta",
    )(main, main, main, main, main, ba, conv_w, conv_w, conv_w, par,
      conv_buf, conv_buf, conv_buf, s0)


def _bucket_thresholds():
    nb = N_BUCKETS // 2
    max_exact = nb // 2
    ratio = MAX_DIST // max_exact
    steps = nb - max_exact
    out = []
    for kk in range(1, steps):
        n = max_exact
        while n ** steps < (max_exact ** steps) * (ratio ** kk):
            n += 1
        out.append(n)
    return out


def _far_distance():
    return _bucket_thresholds()[-1]


def _rel_bias_tile(tab_ref, head, qpos, kpos):
    nb = N_BUCKETS // 2
    max_exact = nb // 2
    rel = kpos - qpos
    n = jnp.abs(rel)
    large = jnp.full(rel.shape, max_exact, jnp.int32)
    for thr in _bucket_thresholds():
        large = large + jnp.where(n >= thr, 1, 0)
    bucket = jnp.where(rel > 0, nb, 0) + jnp.where(n < max_exact, n, large)
    far = tab_ref[nb - 1, head]
    bias = jnp.zeros(rel.shape, F32)
    for b in range(N_BUCKETS):
        bias = jnp.where(bucket == b, (tab_ref[b, head] - far) * LOG2E, bias)
    shift = CHUNK.bit_length() - 1
    mask = lax.shift_right_logical(kpos, shift) <= lax.shift_right_logical(qpos, shift)
    return jnp.where(mask, bias, NEG)


def _bias_kernel(tab_ref, pt_ref, st_ref, *, tq, past, ts):
    head = pl.program_id(0)
    ki = lax.broadcasted_iota(jnp.int32, (tq, 2 * tq), 0)
    qi = lax.broadcasted_iota(jnp.int32, (tq, 2 * tq), 1)
    qi = jnp.where(qi >= tq, qi - tq, qi)
    pt_ref[0] = _rel_bias_tile(tab_ref, head, qi + tq, ki + tq)
    pt_ref[1] = _rel_bias_tile(tab_ref, head, qi + tq, ki)
    wk = past + LANES
    qs = lax.broadcasted_iota(jnp.int32, (ts, wk), 0) + past
    ks = lax.broadcasted_iota(jnp.int32, (ts, wk), 1)
    st = _rel_bias_tile(tab_ref, head, qs, ks)
    st_ref[...] = jnp.where(ks < past + ts, st, NEG)


def _bias_call(rel_bias, tq, past, ts):
    h = rel_bias.shape[1]
    wk = past + LANES
    return pl.pallas_call(
        functools.partial(_bias_kernel, tq=tq, past=past, ts=ts),
        out_shape=[jax.ShapeDtypeStruct((h, 2, tq, 2 * tq), F32),
                   jax.ShapeDtypeStruct((h, ts, wk), F32)],
        grid=(h,),
        in_specs=[pl.BlockSpec(memory_space=pltpu.SMEM)],
        out_specs=[pl.BlockSpec((None, 2, tq, 2 * tq), lambda i: (i, 0, 0, 0)),
                   pl.BlockSpec((None, ts, wk), lambda i: (i, 0, 0))],
        compiler_params=_cparams(("arbitrary",)),
        name="rel_bias",
    )(rel_bias)


def _lam_value(lam_ref, lam_init):
    lp = lam_ref[...]
    s1 = jnp.sum(lp[0:1, :] * lp[1:2, :], axis=-1, keepdims=True)
    s2 = jnp.sum(lp[2:3, :] * lp[3:4, :], axis=-1, keepdims=True)
    return jnp.exp(s1) - jnp.exp(s2) + lam_init


def _stack_maps(q):
    lane = lax.broadcasted_iota(jnp.int32, q.shape, 1)
    zero = jnp.zeros_like(q)
    half = HEAD // 2
    return jnp.concatenate([jnp.where(lane < half, q, zero), jnp.where(lane >= half, q, zero)], axis=0)


def _merge_out(o, lam_init, subln, ma, gb):
    o = o * lax.rsqrt(jnp.mean(o * o, axis=-1, keepdims=True) + LN_EPS) * subln * (1.0 - lam_init)
    return (ma.astype(F32) + _sigmoid(gb.astype(F32)) * o).astype(BF16)


def _attn_scores(n_blocks, q2, k_ref, cs, bias_ref, hh, tq):
    kv = n_blocks * tq
    pieces = []
    if n_blocks > 2:
        pieces.append((0, kv - 2 * tq, None))
    if n_blocks >= 2:
        pieces.append((kv - 2 * tq, tq, 1))
    pieces.append((kv - tq, tq, 0))
    out = []
    for start, size, tile in pieces:
        s = _dot_nt(k_ref[start:start + size, cs], q2)
        if tile is not None:
            s = s + bias_ref[hh, tile]
        out.append((start, size, s))
    return out


def _attn_values(scored, vt, hh):
    m = None
    for _, _, s in scored:
        mx = jnp.max(s, axis=0, keepdims=True)
        m = mx if m is None else jnp.maximum(m, mx)
    l = None
    acc = None
    for start, size, s in scored:
        p = jnp.exp2(s - m)
        ls = jnp.sum(p, axis=0, keepdims=True)
        pv = _dot(vt[hh, :, start:start + size], p.astype(BF16))
        l = ls if l is None else l + ls
        acc = pv if acc is None else acc + pv
    return acc, l


def _attn_kernel(q_ref, k_ref, v_ref, bias_ref, lam_ref, subln_ref, ma_ref, gb_ref,
                 out_ref, vt, *, tq, nq, hps, lam_init):
    cols = [slice(hh * HEAD, (hh + 1) * HEAD) for hh in range(hps)]
    for hh in range(hps):
        vt[hh] = v_ref[:, cols[hh]].T
    lam_val = _lam_value(lam_ref, lam_init)
    subln = subln_ref[...]

    def scores(job):
        hh, n = job
        q2 = _stack_maps(q_ref[(n - 1) * tq:n * tq, cols[hh]])
        return _attn_scores(n, q2, k_ref, cols[hh], bias_ref, hh, tq)

    jobs = [(hh, n) for hh in range(hps) for n in range(1, nq + 1)]
    nxt = scores(jobs[0])
    for pos, (hh, n) in enumerate(jobs):
        cur = nxt
        if pos + 1 < len(jobs):
            nxt = scores(jobs[pos + 1])
        rows = slice((n - 1) * tq, n * tq)
        acc, l = _attn_values(cur, vt, hh)
        o_t = acc[:, 0:tq] / l[:, 0:tq] - lam_val * (acc[:, tq:2 * tq] / l[:, tq:2 * tq])
        out_ref[rows, cols[hh]] = _merge_out(o_t.T, lam_init, subln, ma_ref[rows, cols[hh]],
                                             gb_ref[rows, cols[hh]])


def _attn_call(q, k, v, bias, lam, subln, ma, main, layer, nb, t, tq, lam_init):
    n, d = q.shape
    h = d // HEAD
    nq = t // tq
    assert tq + 1 >= _far_distance() and tq % CHUNK == 0
    hps = ATTN_HPS
    width = hps * HEAD
    gb_col = (N_MAIN_TILES - 1) * (h // hps)
    col = pl.BlockSpec((t, width), lambda b, g: (b, g))
    in_specs = [
        col, col, col,
        pl.BlockSpec((hps, 2, tq, 2 * tq), lambda b, g: (g, 0, 0, 0)),
        pl.BlockSpec((None, 4, HEAD // 2), lambda b, g: (layer, 0, 0)),
        pl.BlockSpec((None, 1, HEAD), lambda b, g: (layer, 0, 0)),
        col,
        pl.BlockSpec((t, width), lambda b, g: (b, gb_col + g)),
    ]
    return pl.pallas_call(
        functools.partial(_attn_kernel, tq=tq, nq=nq, hps=hps, lam_init=lam_init),
        out_shape=jax.ShapeDtypeStruct((n, d), BF16),
        grid=(nb, h // hps),
        in_specs=in_specs,
        out_specs=col,
        scratch_shapes=[pltpu.VMEM((hps, HEAD, t), BF16)],
        compiler_params=_cparams(("parallel", "parallel")),
        name="attn",
    )(q, k, v, bias, lam, subln, ma, main)


def _attn_step_kernel(q_ref, kc_ref, vc_ref, kn_ref, vn_ref, bias_ref, lam_ref, subln_ref,
                      ma_ref, gb_ref, out_ref, *, ts, past, lam_init):
    n_heads = kc_ref.shape[1]
    d = n_heads * HEAD
    kc_all = kc_ref[...].reshape(past, d).astype(BF16)
    vc_all = vc_ref[...].reshape(past, d).astype(BF16)
    pad = jnp.zeros((LANES - ts, HEAD), BF16)
    lam_val = _lam_value(lam_ref, lam_init)
    subln = subln_ref[...]
    for hh in range(n_heads):
        cs = slice(hh * HEAD, (hh + 1) * HEAD)
        q2 = _stack_maps(q_ref[:, cs])
        kc = kc_all[:, cs]
        vc = vc_all[:, cs]
        kn = jnp.concatenate([kn_ref[:, cs], pad], axis=0)
        vn = jnp.concatenate([vn_ref[:, cs], pad], axis=0)
        bias = bias_ref[hh]
        bias2 = jnp.concatenate([bias, bias], axis=0)
        s_c = _dot_nt(q2, kc) + bias2[:, 0:past]
        s_n = _dot_nt(q2, kn) + bias2[:, past:past + LANES]
        m = jnp.maximum(jnp.max(s_c, axis=-1, keepdims=True), jnp.max(s_n, axis=-1, keepdims=True))
        p_c = jnp.exp2(s_c - m)
        p_n = jnp.exp2(s_n - m)
        l = jnp.sum(p_c, axis=-1, keepdims=True) + jnp.sum(p_n, axis=-1, keepdims=True)
        acc = _dot(p_c.astype(BF16), vc) + _dot(p_n.astype(BF16), vn)
        o = acc[0:ts] / l[0:ts] - lam_val * (acc[ts:2 * ts] / l[ts:2 * ts])
        out_ref[:, cs] = _merge_out(o, lam_init, subln, ma_ref[:, cs], gb_ref[:, cs])


def _attn_step_call(q, cache_k, cache_v, k_new, v_new, bias, lam, subln, ma, main, layer, ts, lam_init):
    n, d = q.shape
    _, nb, past, h, _ = cache_k.shape
    wk = past + LANES
    row = pl.BlockSpec((ts, d), lambda b: (b, 0))
    cache = pl.BlockSpec((None, None, past, h, HEAD), lambda b: (layer, b, 0, 0, 0))
    in_specs = [
        row, cache, cache, row, row,
        _resident((h, ts, wk), lambda b: (0, 0, 0)),
        pl.BlockSpec((None, 4, HEAD // 2), lambda b: (layer, 0, 0)),
        pl.BlockSpec((None, 1, HEAD), lambda b: (layer, 0, 0)),
        row,
        pl.BlockSpec((ts, d), lambda b: (b, N_MAIN_TILES - 1)),
    ]
    return pl.pallas_call(
        functools.partial(_attn_step_kernel, ts=ts, past=past, lam_init=lam_init),
        out_shape=jax.ShapeDtypeStruct((n, d), BF16),
        grid=(nb,),
        in_specs=in_specs,
        out_specs=row,
        compiler_params=_cparams(("parallel",)),
        name="attn_step",
    )(q, cache_k, cache_v, k_new, v_new, bias, lam, subln, ma, main)


def _wo_ffn_kernel(m_ref, x_ref, gt1_ref, sc_ref, sh_ref, gt2_ref, w_o_ref, wi_ref, wo_ref,
                   g1_ref, b1_ref, g2_ref, b2_ref, o_ref, *, alpha, nf):
    y = _dot(m_ref[...], w_o_ref[...])
    x = _layer_norm(alpha * x_ref[...] + gt1_ref[...] * y, g1_ref[...], b1_ref[...])
    h = (x * (1.0 + sc_ref[...]) + sh_ref[...]).astype(BF16)
    dff = wo_ref.shape[0]
    tf = dff // nf
    y = None
    for j in range(nf):
        u = _dot(h, wi_ref[:, j * tf:(j + 1) * tf])
        v = _dot(h, wi_ref[:, dff + j * tf:dff + (j + 1) * tf])
        part = _dot((_silu(u) * v).astype(BF16), wo_ref[j * tf:(j + 1) * tf, :])
        y = part if y is None else y + part
    o_ref[...] = _layer_norm(alpha * x + gt2_ref[...] * y, g2_ref[...], b2_ref[...])


def _wo_ffn_call(merged, x, mod, w_o, w_ff_in, w_ff_out, ln1_g, ln1_b, ln2_g, ln2_b, layer, tm,
                 tiles_per_group, alpha, nf):
    n, d = x.shape
    dff = w_ff_out.shape[1]
    vec = pl.BlockSpec((None, 1, d), lambda i: (layer, 0, 0))
    rows = pl.BlockSpec((tm, d), lambda i: (i, 0))
    return pl.pallas_call(
        functools.partial(_wo_ffn_kernel, alpha=alpha, nf=nf),
        out_shape=jax.ShapeDtypeStruct((n, d), F32),
        grid=(n // tm,),
        in_specs=[rows, rows,
                  _mod_spec(mod, d, tiles_per_group, 2),
                  _mod_spec(mod, d, tiles_per_group, 4),
                  _mod_spec(mod, d, tiles_per_group, 3),
                  _mod_spec(mod, d, tiles_per_group, 5),
                  _resident((None, d, d), lambda i: (layer, 0, 0)),
                  _resident((None, d, 2 * dff), lambda i: (layer, 0, 0)),
                  _resident((None, dff, d), lambda i: (layer, 0, 0)),
                  vec, vec, vec, vec],
        out_specs=rows,
        compiler_params=_cparams(("parallel",)),
        name="wo_ffn",
    )(merged, x, mod, mod, mod, mod, w_o, w_ff_in, w_ff_out, ln1_g, ln1_b, ln2_g, ln2_b)


def _pick_tile(n, pref):
    t = min(n, pref)
    while n % t:
        t //= 2
    return t


def kernel(x_prompt, x_sample, cache_k, cache_v, state_conv, state_delta, c_prompt, c_sample,
           ln_in_g, ln_in_b, rel_bias, w_ada, b_ada, w_in, conv_w, a_log, dt_bias, norm_a,
           lam, subln_g, w_o, ln1_g, ln1_b, w_ff_in, w_ff_out, ln2_g, ln2_b):
    bp, tp, d = x_prompt.shape
    bs, ts, _ = x_sample.shape
    depth = w_in.shape[0]
    h = d // HEAD
    past = cache_k.shape[2]
    dff = w_ff_out.shape[1]
    n_p = bp * tp
    n_s = bs * ts
    alpha = (2 * depth) ** 0.25
    hb = h
    ng = h // hb

    o1 = 4 * d + 2 * h
    qkvz = w_in[:, :, 0:4 * d]
    gates = w_in[:, :, o1 + 3 * d:o1 + 5 * d]
    qb = w_in[:, :, o1:o1 + d] * ((HEAD // 2) ** -0.5 * LOG2E)
    kvb = w_in[:, :, o1 + d:o1 + 3 * d]
    w_all = jnp.concatenate([qkvz, gates, qb, kvb], axis=-1).astype(BF16)
    wb = w_in[:, :, 4 * d:4 * d + h].reshape(depth, d, ng, hb)
    wa = w_in[:, :, 4 * d + h:4 * d + 2 * h].reshape(depth, d, ng, hb)
    w_ba = jnp.concatenate([wb, wa, jnp.zeros((depth, d, ng, LANES - 2 * hb), F32)], axis=-1)
    w_ba = w_ba.reshape(depth, d, ng * LANES).astype(BF16)
    w_ada_b = w_ada.astype(BF16)
    w_o_b = w_o.astype(BF16)
    w_ff_in_b = w_ff_in.astype(BF16)
    w_ff_out_b = w_ff_out.astype(BF16)

    def lane_rows(v):
        v = v.reshape(depth, ng, hb)
        z = jnp.zeros((depth, ng, hb), F32)
        zz = jnp.zeros((depth, ng, LANES - 2 * hb), F32)
        return jnp.concatenate([z, v, zz], axis=-1).reshape(depth, ng * LANES)

    par = jnp.zeros((depth, 8, ng * LANES), F32)
    par = par.at[:, 0].set(lane_rows(dt_bias)).at[:, 1].set(lane_rows(a_log))
    par = par.at[:, 2].set(jnp.tile(norm_a, (1, ng)))
    subln = subln_g.reshape(depth, 1, HEAD)
    ln1g, ln1b = ln1_g.reshape(depth, 1, d), ln1_b.reshape(depth, 1, d)
    ln2g, ln2b = ln2_g.reshape(depth, 1, d), ln2_b.reshape(depth, 1, d)

    mod = _ada_call(jnp.concatenate([c_prompt, c_sample], axis=0), w_ada_b, b_ada)
    mod_p = mod[:, :bp].reshape(depth, bp, 1, 6 * d)
    mod_s = jnp.repeat(mod[:, bp:], ts, axis=1).reshape(depth, 1, n_s, 6 * d)

    bias_p, bias_s = _bias_call(rel_bias, TQ, past, ts)

    tm_p = _pick_tile(tp, TM_PROJ)
    tpg_p = tp // tm_p
    xp = _ln_call(x_prompt.reshape(n_p, d), ln_in_g, ln_in_b, tm_p)
    xs = _ln_call(x_sample.reshape(n_s, d), ln_in_g, ln_in_b, n_s)

    conv0 = jnp.zeros((bp, CONV_W - 1, 3 * d), F32)
    s0 = jnp.zeros((bp, h, HEAD, HEAD), F32)
    nf = 1

    kp, vp = (jnp.zeros((depth, n_p, h, HEAD), F32) for _ in range(2))
    ks, vs = (jnp.zeros((depth, n_s, h, HEAD), F32) for _ in range(2))
    conv_p, conv_s, st_p, st_s = [], [], [], []
    for l in range(depth):
        lam_init = 0.8 - 0.6 * math.exp(-0.3 * l)
        main, qb_p, kb_l, vb_l, kp, vp, ba, tail = _inproj_call(
            xp, mod_p[l], w_all, w_ba, conv_w, conv0, kp, vp, l, tm_p, tpg_p, BF16, True)
        ma, sn = _delta_call(main, ba, conv_w, par, conv0, s0, l, bp, tp, CHUNK, DELTA_CPS, hb, True)
        merged = _attn_call(qb_p, kb_l, vb_l, bias_p, lam, subln, ma, main, l, bp, tp, TQ, lam_init)
        xp = _wo_ffn_call(merged, xp, mod_p[l], w_o_b, w_ff_in_b, w_ff_out_b, ln1g, ln1b, ln2g, ln2b,
                          l, tm_p, tpg_p, alpha, nf)
        conv_p.append(tail.reshape(bp, tpg_p, CONV_PAD, 3 * d)[:, tpg_p - 1, CONV_PAD - (CONV_W - 1):, :])
        st_p.append(sn)
        main, qb_s, kb_l, vb_l, ks, vs, ba, _ = _inproj_call(
            xs, mod_s[l], w_all, w_ba, conv_w, state_conv[l], ks, vs, l, n_s, 1, F32, False)
        ma, sn = _delta_call(main, ba, conv_w, par, state_conv[l], state_delta[l], l, bs, ts, ts, 1, hb, False)
        merged = _attn_step_call(qb_s, cache_k, cache_v, kb_l, vb_l, bias_s, lam, subln, ma, main,
                                 l, ts, lam_init)
        xs = _wo_ffn_call(merged, xs, mod_s[l], w_o_b, w_ff_in_b, w_ff_out_b, ln1g, ln1b, ln2g, ln2b,
                          l, n_s, 1, alpha, nf)
        conv_s.append(main.reshape(bs, ts, -1)[:, ts - (CONV_W - 1):, 0:3 * d])
        st_s.append(sn)

    def heads_out(buf, nb, t):
        return buf.reshape(depth, nb, t, h, HEAD)

    return (xp.reshape(bp, tp, d), xs.reshape(bs, ts, d),
            heads_out(kp, bp, tp), heads_out(vp, bp, tp), jnp.stack(conv_p), jnp.stack(st_p),
            heads_out(ks, bs, ts), heads_out(vs, bs, ts), jnp.stack(conv_s), jnp.stack(st_s))
```

```python
import functools
import math

import jax
import jax.numpy as jnp
from jax import lax
from jax.experimental import pallas as pl
from jax.experimental.pallas import tpu as pltpu

F32 = jnp.float32
BF16 = jnp.bfloat16
HIGHEST = lax.Precision.HIGHEST

LN_EPS = 1e-5
CHUNK = 64
HEAD = 128
CONV_W = 4
N_BUCKETS = 32
MAX_DIST = 128
NEG = -1e30
LOG2E = math.log2(math.e)
LANES = 128
CONV_PAD = 8
VMEM_LIMIT = 56 * 1024 * 1024

TM_PROJ = 512
TQ = 256
DELTA_CPS = 8
ATTN_HPS = 4


def _cparams(sem):
    return pltpu.CompilerParams(dimension_semantics=sem, vmem_limit_bytes=VMEM_LIMIT)


def _sigmoid(x):
    return 1.0 / (1.0 + jnp.exp(-x))


def _silu(x):
    return x * _sigmoid(x)


def _layer_norm(x, g, b):
    mu = jnp.mean(x, axis=-1, keepdims=True)
    xc = x - mu
    var = jnp.mean(xc * xc, axis=-1, keepdims=True)
    return xc * lax.rsqrt(var + LN_EPS) * g + b


def _dot(a, b):
    return jnp.dot(a, b, preferred_element_type=F32)


def _hdot(a, b):
    return jnp.dot(a, b, preferred_element_type=F32, precision=HIGHEST)


def _bdot(a, b):
    return _dot(a.astype(BF16), b.astype(BF16))


def _dot_nt(a, b):
    return lax.dot_general(a, b, (((1,), (1,)), ((), ())), preferred_element_type=F32)


def _dot_tn(a, b):
    return lax.dot_general(a, b, (((0,), (0,)), ((), ())), preferred_element_type=F32)


def _ln_kernel(x_ref, g_ref, b_ref, o_ref):
    o_ref[...] = _layer_norm(x_ref[...], g_ref[...], b_ref[...])


def _ln_call(x, g, b, tm):
    n, d = x.shape
    return pl.pallas_call(
        _ln_kernel,
        out_shape=jax.ShapeDtypeStruct((n, d), F32),
        grid=(n // tm,),
        in_specs=[pl.BlockSpec((tm, d), lambda i: (i, 0)),
                  pl.BlockSpec((1, d), lambda i: (0, 0)),
                  pl.BlockSpec((1, d), lambda i: (0, 0))],
        out_specs=pl.BlockSpec((tm, d), lambda i: (i, 0)),
        compiler_params=_cparams(("parallel",)),
        name="ln_in",
    )(x, g.reshape(1, d), b.reshape(1, d))


def _ada_kernel(c_ref, w_ref, b_ref, o_ref):
    c = c_ref[...]
    o_ref[...] = _dot(_silu(c).astype(BF16), w_ref[...]) + b_ref[...]


def _ada_call(c_all, w_ada, b_ada):
    depth, d, d6 = w_ada.shape
    nb = c_all.shape[0]
    return pl.pallas_call(
        _ada_kernel,
        out_shape=jax.ShapeDtypeStruct((depth, nb, d6), F32),
        grid=(depth, d6 // d),
        in_specs=[pl.BlockSpec((nb, d), lambda l, j: (0, 0)),
                  pl.BlockSpec((None, d, d), lambda l, j: (l, 0, j)),
                  pl.BlockSpec((None, 1, d), lambda l, j: (l, 0, j))],
        out_specs=pl.BlockSpec((None, nb, d), lambda l, j: (l, 0, j)),
        compiler_params=_cparams(("parallel", "parallel")),
        name="ada",
    )(c_all, w_ada, b_ada.reshape(depth, 1, d6))


def _mod_spec(mod, d, tiles_per_group, chunk):
    rows = mod.shape[1]
    return pl.BlockSpec((None, rows, d), lambda i, *_: (i // tiles_per_group, 0, chunk))


N_MAIN_TILES = 6
N_W_TILES = 9
N_CONV_TILES = 3


def _conv_taps(cat_ref, cw, rows):
    acc = None
    for i in range(CONV_W):
        term = cat_ref[pl.ds(CONV_PAD - (CONV_W - 1) + i, rows), :] * cw[i:i + 1, :]
        acc = term if acc is None else acc + term
    return acc


def _conv_taps_rolled(prev, cur, cw):
    assert CONV_W == 4 and CONV_PAD >= 2 * (CONV_W - 1)
    rows = cur.shape[0]
    cat = jnp.concatenate([prev, cur], axis=0)
    cat1 = pltpu.roll(cat, 1, axis=0)
    near = cat * cw[3:4, :] + cat1 * cw[2:3, :]
    far = cat * cw[1:2, :] + cat1 * cw[0:1, :]
    return (near + pltpu.roll(far, 2, axis=0))[CONV_PAD:CONV_PAD + rows, :]


def _l2norm_heads(y, scale):
    outs = []
    for hh in range(y.shape[1] // HEAD):
        seg = y[:, hh * HEAD:(hh + 1) * HEAD]
        outs.append(seg * (lax.rsqrt(jnp.sum(seg * seg, axis=-1, keepdims=True) + 1e-6) * scale))
    return jnp.concatenate(outs, axis=1)


def _inproj_kernel(x_ref, sc_ref, sh_ref, w_ref, wba_ref, cw_ref, cb_ref, kin_ref, vin_ref,
                   main_ref, q_ref, kb_ref, vb_ref, k_ref, v_ref, ba_ref, tail_ref,
                   carry_scr, *, fuse_conv, tiles_per_group):
    tm, d = x_ref.shape
    h = (x_ref[...] * (1.0 + sc_ref[...]) + sh_ref[...]).astype(BF16)

    def cols(j):
        return _dot(h, w_ref[:, j * d:(j + 1) * d])

    if fuse_conv:
        @pl.when(pl.program_id(0) % tiles_per_group == 0)
        def _():
            for j in range(N_CONV_TILES):
                carry_scr[j, CONV_PAD - (CONV_W - 1):CONV_PAD, :] = cb_ref[:, j * d:(j + 1) * d]

    def finish(j, res):
        if j < N_CONV_TILES:
            last = res[tm - CONV_PAD:tm, :]
            tail_ref[:, j * d:(j + 1) * d] = last
            if fuse_conv:
                prev = carry_scr[j]
                carry_scr[j] = last
                res = _silu(_conv_taps_rolled(prev, res, cw_ref[:, j * d:(j + 1) * d]))
                if j == 0:
                    res = _l2norm_heads(res, HEAD ** -0.5)
                elif j == 1:
                    res = _l2norm_heads(res, 1.0)
        if j < N_MAIN_TILES:
            main_ref[:, j * d:(j + 1) * d] = res.astype(main_ref.dtype)
        elif j == N_MAIN_TILES:
            q_ref[...] = res.astype(BF16)
        else:
            b_ref, o_ref = ((kb_ref, k_ref), (vb_ref, v_ref))[j - N_MAIN_TILES - 1]
            b_ref[...] = res.astype(BF16)
            o_ref[...] = res.reshape(o_ref.shape)

    order = []
    for j in range(N_CONV_TILES):
        order += [j, N_CONV_TILES + j]
    order += list(range(2 * N_CONV_TILES, N_W_TILES))
    nxt = cols(order[0])
    for pos, j in enumerate(order):
        res = nxt
        if pos + 1 < len(order):
            nxt = cols(order[pos + 1])
        finish(j, res)
    ba_ref[...] = _dot(h, wba_ref[...])


def _resident(block_shape, index_map):
    return pl.BlockSpec(block_shape, index_map, pipeline_mode=pl.Buffered(1))


def _inproj_call(x, mod, w_all, w_ba, conv_w, conv_buf, kbuf, vbuf, layer, tm, tiles_per_group,
                 main_dtype, fuse_conv):
    n, d = x.shape
    nba = w_ba.shape[-1]
    wide = N_W_TILES * d
    cwide = N_CONV_TILES * d
    in_specs = [
        pl.BlockSpec((tm, d), lambda i: (i, 0)),
        _mod_spec(mod, d, tiles_per_group, 1),
        _mod_spec(mod, d, tiles_per_group, 0),
        _resident((None, d, wide), lambda i: (layer, 0, 0)),
        _resident((None, d, nba), lambda i: (layer, 0, 0)),
        _resident((None, CONV_W, cwide), lambda i: (layer, 0, 0)),
        pl.BlockSpec((None, CONV_W - 1, cwide), lambda i: (i // tiles_per_group if fuse_conv else 0, 0, 0)),
        pl.BlockSpec(memory_space=pl.ANY),
        pl.BlockSpec(memory_space=pl.ANY),
    ]
    out_shape = [
        jax.ShapeDtypeStruct((n, N_MAIN_TILES * d), main_dtype),
        jax.ShapeDtypeStruct((n, d), BF16),
        jax.ShapeDtypeStruct((n, d), BF16),
        jax.ShapeDtypeStruct((n, d), BF16),
        jax.ShapeDtypeStruct(kbuf.shape, F32),
        jax.ShapeDtypeStruct(vbuf.shape, F32),
        jax.ShapeDtypeStruct((n, nba), F32),
        jax.ShapeDtypeStruct((n // tm, CONV_PAD, N_CONV_TILES * d), F32),
    ]
    out_specs = [
        pl.BlockSpec((tm, N_MAIN_TILES * d), lambda i: (i, 0)),
        pl.BlockSpec((tm, d), lambda i: (i, 0)),
        pl.BlockSpec((tm, d), lambda i: (i, 0)),
        pl.BlockSpec((tm, d), lambda i: (i, 0)),
        pl.BlockSpec((None, tm, d // HEAD, HEAD), lambda i: (layer, i, 0, 0)),
        pl.BlockSpec((None, tm, d // HEAD, HEAD), lambda i: (layer, i, 0, 0)),
        pl.BlockSpec((tm, nba), lambda i: (i, 0)),
        pl.BlockSpec((None, CONV_PAD, N_CONV_TILES * d), lambda i: (i, 0, 0)),
    ]
    return pl.pallas_call(
        functools.partial(_inproj_kernel, fuse_conv=fuse_conv, tiles_per_group=tiles_per_group),
        out_shape=out_shape,
        grid=(n // tm,),
        in_specs=in_specs,
        out_specs=out_specs,
        scratch_shapes=[pltpu.VMEM((N_CONV_TILES, CONV_PAD, d), F32)],
        input_output_aliases={7: 4, 8: 5},
        compiler_params=_cparams(("arbitrary",)),
        name="inproj",
    )(x, mod, mod, w_all, w_ba, conv_w, conv_buf, kbuf, vbuf)


def _tri_inverse(a_list, eye, blk, size):
    s = 8
    d = [jnp.where(blk[s], a, 0.0) for a in a_list]
    d2 = [_bdot(x, x) for x in d]
    p = [_bdot(eye - x, eye + y) for x, y in zip(d, d2)]
    d4 = [_bdot(x, x) for x in d2]
    t = [_bdot(x, eye + y) for x, y in zip(p, d4)]
    while s < size:
        lower = jnp.logical_and(blk[2 * s], jnp.logical_not(blk[s]))
        off = [jnp.where(lower, a, 0.0) for a in a_list]
        x = [_bdot(ti, oi) for ti, oi in zip(t, off)]
        t = [ti - _bdot(xi, ti) for ti, xi in zip(t, x)]
        s *= 2
    return t


def _delta_kernel(qu_ref, ku_ref, vu_ref, z_ref, ga_ref, ba_ref,
                  cwq_ref, cwk_ref, cwv_ref, par_ref, cbq_ref, cbk_ref, cbv_ref, s0_ref,
                  ma_ref, sn_ref, s_scr, cat_scr, *, chunk, cps, hb, pre_conv):
    c = pl.program_id(2)
    rows_blk = chunk * cps
    tail = CONV_W - 1

    @pl.when(c == 0)
    def _():
        s_scr[...] = s0_ref[...]
        if not pre_conv:
            for s, cb in enumerate((cbq_ref, cbk_ref, cbv_ref)):
                cat_scr[s, CONV_PAD - tail:CONV_PAD, :] = cb[...]

    if pre_conv:
        convs = [u_ref[...].astype(F32) for u_ref in (qu_ref, ku_ref, vu_ref)]
    else:
        convs = []
        for s, (u_ref, cw_ref) in enumerate(((qu_ref, cwq_ref), (ku_ref, cwk_ref), (vu_ref, cwv_ref))):
            cat_scr[s, CONV_PAD:CONV_PAD + rows_blk, :] = u_ref[...].astype(F32)
            y = _silu(_conv_taps(cat_scr.at[s], cw_ref[...], rows_blk))
            convs.append(y if s == 2 else _l2norm_heads(y, HEAD ** -0.5 if s == 0 else 1.0))
            cat_scr[s, CONV_PAD - tail:CONV_PAD, :] = cat_scr[s, CONV_PAD + rows_blk - tail:CONV_PAD + rows_blk, :]

    ba = ba_ref[...]
    dt_row = par_ref[0:1, :]
    alog_row = par_ref[1:2, :]
    norm_row = par_ref[2:3, 0:HEAD]
    beta_all = _sigmoid(ba)
    xa = ba + dt_row
    softplus = jnp.maximum(xa, 0.0) + jnp.log(1.0 + jnp.exp(-jnp.abs(xa)))
    g_all = -jnp.exp(alog_row) * softplus

    r = lax.broadcasted_iota(jnp.int32, (chunk, chunk), 0)
    cc = lax.broadcasted_iota(jnp.int32, (chunk, chunk), 1)
    causal = r >= cc
    strict = r > cc
    eye = jnp.where(r == cc, 1.0, 0.0).astype(F32)
    tril = jnp.where(causal, 1.0, 0.0).astype(F32)
    blk = {}
    s = 8
    while s <= chunk:
        blk[s] = jnp.bitwise_xor(r, cc) < s
        s *= 2

    heads = range(hb)
    pairs = [(ci, hh) for ci in range(cps) for hh in heads]
    rsl = [slice(ci * chunk, (ci + 1) * chunk) for ci in range(cps)]
    csl = [slice(hh * HEAD, (hh + 1) * HEAD) for hh in heads]
    gcum_all = [_hdot(tril, g_all[rs, :]) for rs in rsl]
    gcum_t = [g.T for g in gcum_all]

    q, k, v, beta, gcum, g_last, decay = [], [], [], [], [], [], []
    for ci, hh in pairs:
        rs, cs = rsl[ci], csl[hh]
        q.append(convs[0][rs, cs])
        k.append(convs[1][rs, cs])
        v.append(convs[2][rs, cs])
        beta.append(beta_all[rs, hh:hh + 1])
        gc = gcum_all[ci][:, hb + hh:hb + hh + 1]
        gr = gcum_t[ci][hb + hh:hb + hh + 1, :]
        gcum.append(gc)
        g_last.append(gc[chunk - 1:chunk, :])
        decay.append(jnp.where(causal, jnp.exp(jnp.where(causal, gc - gr, 0.0)), 0.0))
    idx = range(len(pairs))
    e_g = [jnp.exp(gcum[i]) for i in idx]
    kb = [k[i] * beta[i] for i in idx]
    k_b16 = [k[i].astype(BF16) for i in idx]
    kk_mat = [_dot_nt(kb[i].astype(BF16), k_b16[i]) for i in idx]
    qk = [_dot_nt(q[i].astype(BF16), k_b16[i]) * decay[i] for i in idx]
    a_mat = [jnp.where(strict, kk_mat[i] * decay[i], 0.0) for i in idx]
    t_inv = _tri_inverse(a_mat, eye, blk, chunk)
    uw = [_bdot(t_inv[i], jnp.concatenate([v[i] * beta[i], kb[i] * e_g[i]], axis=1)) for i in idx]
    q_g = [(q[i] * e_g[i]).astype(BF16) for i in idx]
    k_tail = [(k[i] * jnp.exp(g_last[i] - gcum[i])).astype(BF16) for i in idx]
    qk = [x.astype(BF16) for x in qk]

    for ci in range(cps):
        ids = [ci * hb + hh for hh in heads]
        state = [s_scr[hh] for hh in heads]
        state_b = [x.astype(BF16) for x in state]
        ws = [_dot(uw[i][:, HEAD:2 * HEAD].astype(BF16), state_b[hh]) for hh, i in zip(heads, ids)]
        u = [uw[i][:, 0:HEAD] - ws[hh] for hh, i in zip(heads, ids)]
        u_b = [x.astype(BF16) for x in u]
        qs = [_dot(q_g[i], state_b[hh]) for hh, i in zip(heads, ids)]
        ku = [_dot_tn(k_tail[i], u_b[hh]) for hh, i in zip(heads, ids)]
        qu = [_dot(qk[i], u_b[hh]) for hh, i in zip(heads, ids)]
        for hh, i in zip(heads, ids):
            s_scr[hh] = jnp.exp(g_last[i]) * state[hh] + ku[hh]
        for hh in heads:
            rs, cs = rsl[ci], csl[hh]
            o = qs[hh] + qu[hh]
            o = o * lax.rsqrt(jnp.mean(o * o, axis=-1, keepdims=True) + LN_EPS) * norm_row
            o = o * _silu(z_ref[rs, cs].astype(F32))
            ma_ref[rs, cs] = (_sigmoid(ga_ref[rs, cs].astype(F32)) * o).astype(ma_ref.dtype)

    @pl.when(c == pl.num_programs(2) - 1)
    def _():
        sn_ref[...] = s_scr[...]


def _delta_call(main, ba, conv_w, par, conv_buf, s0, layer, nb, t, chunk, cps, hb, pre_conv):
    n = main.shape[0]
    d = main.shape[1] // N_MAIN_TILES
    h = d // HEAD
    ng = h // hb
    wb = hb * HEAD
    rows_blk = chunk * cps
    ncb = t // rows_blk
    grid = (nb, ng, ncb)

    def seg(s):
        return pl.BlockSpec((rows_blk, wb), lambda b, g, c: (b * ncb + c, s * ng + g))

    def cw(s):
        return pl.BlockSpec((None, CONV_W, wb), lambda b, g, c: (layer, 0, s * ng + g))

    def cb(s):
        return pl.BlockSpec((None, CONV_W - 1, wb), lambda b, g, c: (b, 0, s * ng + g))

    in_specs = [seg(0), seg(1), seg(2), seg(3), seg(4),
                pl.BlockSpec((rows_blk, LANES), lambda b, g, c: (b * ncb + c, g)),
                cw(0), cw(1), cw(2),
                pl.BlockSpec((None, 8, LANES), lambda b, g, c: (layer, 0, g)),
                cb(0), cb(1), cb(2),
                pl.BlockSpec((None, hb, HEAD, HEAD), lambda b, g, c: (b, g, 0, 0))]
    out_shape = [jax.ShapeDtypeStruct((n, d), BF16),
                 jax.ShapeDtypeStruct((nb, h, HEAD, HEAD), F32)]
    out_specs = [pl.BlockSpec((rows_blk, wb), lambda b, g, c: (b * ncb + c, g)),
                 pl.BlockSpec((None, hb, HEAD, HEAD), lambda b, g, c: (b, g, 0, 0))]
    return pl.pallas_call(
        functools.partial(_delta_kernel, chunk=chunk, cps=cps, hb=hb, pre_conv=pre_conv),
        out_shape=out_shape,
        grid=grid,
        in_specs=in_specs,
        out_specs=out_specs,
        scratch_shapes=[pltpu.VMEM((hb, HEAD, HEAD), F32),
                        pltpu.VMEM((3, CONV_PAD + rows_blk, wb), F32)],
        compiler_params=_cparams(("parallel", "parallel", "arbitrary")),
        name="delta",
    )(main, main, main, main, main, ba, conv_w, conv_w, conv_w, par,
      conv_buf, conv_buf, conv_buf, s0)


def _bucket_thresholds():
    nb = N_BUCKETS // 2
    max_exact = nb // 2
    ratio = MAX_DIST // max_exact
    steps = nb - max_exact
    out = []
    for kk in range(1, steps):
        n = max_exact
        while n ** steps < (max_exact ** steps) * (ratio ** kk):
            n += 1
        out.append(n)
    return out


def _far_distance():
    return _bucket_thresholds()[-1]


def _rel_bias_tile(tab_ref, head, qpos, kpos):
    nb = N_BUCKETS // 2
    max_exact = nb // 2
    rel = kpos - qpos
    n = jnp.abs(rel)
    large = jnp.full(rel.shape, max_exact, jnp.int32)
    for thr in _bucket_thresholds():
        large = large + jnp.where(n >= thr, 1, 0)
    bucket = jnp.where(rel > 0, nb, 0) + jnp.where(n < max_exact, n, large)
    far = tab_ref[nb - 1, head]
    bias = jnp.zeros(rel.shape, F32)
    for b in range(N_BUCKETS):
        bias = jnp.where(bucket == b, (tab_ref[b, head] - far) * LOG2E, bias)
    shift = CHUNK.bit_length() - 1
    mask = lax.shift_right_logical(kpos, shift) <= lax.shift_right_logical(qpos, shift)
    return jnp.where(mask, bias, NEG)


def _bias_kernel(tab_ref, pt_ref, st_ref, *, tq, past, ts):
    head = pl.program_id(0)
    ki = lax.broadcasted_iota(jnp.int32, (tq, 2 * tq), 0)
    qi = lax.broadcasted_iota(jnp.int32, (tq, 2 * tq), 1)
    qi = jnp.where(qi >= tq, qi - tq, qi)
    pt_ref[0] = _rel_bias_tile(tab_ref, head, qi + tq, ki + tq)
    pt_ref[1] = _rel_bias_tile(tab_ref, head, qi + tq, ki)
    wk = past + LANES
    qs = lax.broadcasted_iota(jnp.int32, (ts, wk), 0) + past
    ks = lax.broadcasted_iota(jnp.int32, (ts, wk), 1)
    st = _rel_bias_tile(tab_ref, head, qs, ks)
    st_ref[...] = jnp.where(ks < past + ts, st, NEG)


def _bias_call(rel_bias, tq, past, ts):
    h = rel_bias.shape[1]
    wk = past + LANES
    return pl.pallas_call(
        functools.partial(_bias_kernel, tq=tq, past=past, ts=ts),
        out_shape=[jax.ShapeDtypeStruct((h, 2, tq, 2 * tq), F32),
                   jax.ShapeDtypeStruct((h, ts, wk), F32)],
        grid=(h,),
        in_specs=[pl.BlockSpec(memory_space=pltpu.SMEM)],
        out_specs=[pl.BlockSpec((None, 2, tq, 2 * tq), lambda i: (i, 0, 0, 0)),
                   pl.BlockSpec((None, ts, wk), lambda i: (i, 0, 0))],
        compiler_params=_cparams(("arbitrary",)),
        name="rel_bias",
    )(rel_bias)


def _lam_value(lam_ref, lam_init):
    lp = lam_ref[...]
    s1 = jnp.sum(lp[0:1, :] * lp[1:2, :], axis=-1, keepdims=True)
    s2 = jnp.sum(lp[2:3, :] * lp[3:4, :], axis=-1, keepdims=True)
    return jnp.exp(s1) - jnp.exp(s2) + lam_init


def _stack_maps(q):
    lane = lax.broadcasted_iota(jnp.int32, q.shape, 1)
    zero = jnp.zeros_like(q)
    half = HEAD // 2
    return jnp.concatenate([jnp.where(lane < half, q, zero), jnp.where(lane >= half, q, zero)], axis=0)


def _merge_out(o, lam_init, subln, ma, gb):
    o = o * lax.rsqrt(jnp.mean(o * o, axis=-1, keepdims=True) + LN_EPS) * subln * (1.0 - lam_init)
    return (ma.astype(F32) + _sigmoid(gb.astype(F32)) * o).astype(BF16)


def _attn_scores(n_blocks, q2, k_ref, cs, bias_ref, hh, tq):
    kv = n_blocks * tq
    pieces = []
    if n_blocks > 2:
        pieces.append((0, kv - 2 * tq, None))
    if n_blocks >= 2:
        pieces.append((kv - 2 * tq, tq, 1))
    pieces.append((kv - tq, tq, 0))
    out = []
    for start, size, tile in pieces:
        s = _dot_nt(k_ref[start:start + size, cs], q2)
        if tile is not None:
            s = s + bias_ref[hh, tile]
        out.append((start, size, s))
    return out


def _attn_values(scored, vt, hh):
    m = None
    for _, _, s in scored:
        mx = jnp.max(s, axis=0, keepdims=True)
        m = mx if m is None else jnp.maximum(m, mx)
    l = None
    acc = None
    for start, size, s in scored:
        p = jnp.exp2(s - m)
        ls = jnp.sum(p, axis=0, keepdims=True)
        pv = _dot(vt[hh, :, start:start + size], p.astype(BF16))
        l = ls if l is None else l + ls
        acc = pv if acc is None else acc + pv
    return acc, l


def _attn_kernel(q_ref, k_ref, v_ref, bias_ref, lam_ref, subln_ref, ma_ref, gb_ref,
                 out_ref, vt, *, tq, nq, hps, lam_init):
    cols = [slice(hh * HEAD, (hh + 1) * HEAD) for hh in range(hps)]
    for hh in range(hps):
        vt[hh] = v_ref[:, cols[hh]].T
    lam_val = _lam_value(lam_ref, lam_init)
    subln = subln_ref[...]

    def scores(job):
        hh, n = job
        q2 = _stack_maps(q_ref[(n - 1) * tq:n * tq, cols[hh]])
        return _attn_scores(n, q2, k_ref, cols[hh], bias_ref, hh, tq)

    jobs = [(hh, n) for hh in range(hps) for n in range(1, nq + 1)]
    nxt = scores(jobs[0])
    for pos, (hh, n) in enumerate(jobs):
        cur = nxt
        if pos + 1 < len(jobs):
            nxt = scores(jobs[pos + 1])
        rows = slice((n - 1) * tq, n * tq)
        acc, l = _attn_values(cur, vt, hh)
        o_t = acc[:, 0:tq] / l[:, 0:tq] - lam_val * (acc[:, tq:2 * tq] / l[:, tq:2 * tq])
        out_ref[rows, cols[hh]] = _merge_out(o_t.T, lam_init, subln, ma_ref[rows, cols[hh]],
                                             gb_ref[rows, cols[hh]])


def _attn_call(q, k, v, bias, lam, subln, ma, main, layer, nb, t, tq, lam_init):
    n, d = q.shape
    h = d // HEAD
    nq = t // tq
    assert tq + 1 >= _far_distance() and tq % CHUNK == 0
    hps = ATTN_HPS
    width = hps * HEAD
    gb_col = (N_MAIN_TILES - 1) * (h // hps)
    col = pl.BlockSpec((t, width), lambda b, g: (b, g))
    in_specs = [
        col, col, col,
        pl.BlockSpec((hps, 2, tq, 2 * tq), lambda b, g: (g, 0, 0, 0)),
        pl.BlockSpec((None, 4, HEAD // 2), lambda b, g: (layer, 0, 0)),
        pl.BlockSpec((None, 1, HEAD), lambda b, g: (layer, 0, 0)),
        col,
        pl.BlockSpec((t, width), lambda b, g: (b, gb_col + g)),
    ]
    return pl.pallas_call(
        functools.partial(_attn_kernel, tq=tq, nq=nq, hps=hps, lam_init=lam_init),
        out_shape=jax.ShapeDtypeStruct((n, d), BF16),
        grid=(nb, h // hps),
        in_specs=in_specs,
        out_specs=col,
        scratch_shapes=[pltpu.VMEM((hps, HEAD, t), BF16)],
        compiler_params=_cparams(("parallel", "parallel")),
        name="attn",
    )(q, k, v, bias, lam, subln, ma, main)


def _attn_step_kernel(q_ref, kc_ref, vc_ref, kn_ref, vn_ref, bias_ref, lam_ref, subln_ref,
                      ma_ref, gb_ref, out_ref, *, ts, past, lam_init):
    n_heads = kc_ref.shape[1]
    d = n_heads * HEAD
    kc_all = kc_ref[...].reshape(past, d).astype(BF16)
    vc_all = vc_ref[...].reshape(past, d).astype(BF16)
    pad = jnp.zeros((LANES - ts, HEAD), BF16)
    lam_val = _lam_value(lam_ref, lam_init)
    subln = subln_ref[...]
    for hh in range(n_heads):
        cs = slice(hh * HEAD, (hh + 1) * HEAD)
        q2 = _stack_maps(q_ref[:, cs])
        kc = kc_all[:, cs]
        vc = vc_all[:, cs]
        kn = jnp.concatenate([kn_ref[:, cs], pad], axis=0)
        vn = jnp.concatenate([vn_ref[:, cs], pad], axis=0)
        bias = bias_ref[hh]
        bias2 = jnp.concatenate([bias, bias], axis=0)
        s_c = _dot_nt(q2, kc) + bias2[:, 0:past]
        s_n = _dot_nt(q2, kn) + bias2[:, past:past + LANES]
        m = jnp.maximum(jnp.max(s_c, axis=-1, keepdims=True), jnp.max(s_n, axis=-1, keepdims=True))
        p_c = jnp.exp2(s_c - m)
        p_n = jnp.exp2(s_n - m)
        l = jnp.sum(p_c, axis=-1, keepdims=True) + jnp.sum(p_n, axis=-1, keepdims=True)
        acc = _dot(p_c.astype(BF16), vc) + _dot(p_n.astype(BF16), vn)
        o = acc[0:ts] / l[0:ts] - lam_val * (acc[ts:2 * ts] / l[ts:2 * ts])
        out_ref[:, cs] = _merge_out(o, lam_init, subln, ma_ref[:, cs], gb_ref[:, cs])


def _attn_step_call(q, cache_k, cache_v, k_new, v_new, bias, lam, subln, ma, main, layer, ts, lam_init):
    n, d = q.shape
    _, nb, past, h, _ = cache_k.shape
    wk = past + LANES
    row = pl.BlockSpec((ts, d), lambda b: (b, 0))
    cache = pl.BlockSpec((None, None, past, h, HEAD), lambda b: (layer, b, 0, 0, 0))
    in_specs = [
        row, cache, cache, row, row,
        _resident((h, ts, wk), lambda b: (0, 0, 0)),
        pl.BlockSpec((None, 4, HEAD // 2), lambda b: (layer, 0, 0)),
        pl.BlockSpec((None, 1, HEAD), lambda b: (layer, 0, 0)),
        row,
        pl.BlockSpec((ts, d), lambda b: (b, N_MAIN_TILES - 1)),
    ]
    return pl.pallas_call(
        functools.partial(_attn_step_kernel, ts=ts, past=past, lam_init=lam_init),
        out_shape=jax.ShapeDtypeStruct((n, d), BF16),
        grid=(nb,),
        in_specs=in_specs,
        out_specs=row,
        compiler_params=_cparams(("parallel",)),
        name="attn_step",
    )(q, cache_k, cache_v, k_new, v_new, bias, lam, subln, ma, main)


def _wo_ffn_kernel(m_ref, x_ref, gt1_ref, sc_ref, sh_ref, gt2_ref, w_o_ref, wi_ref, wo_ref,
                   g1_ref, b1_ref, g2_ref, b2_ref, o_ref, *, alpha, nf):
    y = _dot(m_ref[...], w_o_ref[...])
    x = _layer_norm(alpha * x_ref[...] + gt1_ref[...] * y, g1_ref[...], b1_ref[...])
    h = (x * (1.0 + sc_ref[...]) + sh_ref[...]).astype(BF16)
    dff = wo_ref.shape[0]
    tf = dff // nf
    y = None
    for j in range(nf):
        u = _dot(h, wi_ref[:, j * tf:(j + 1) * tf])
        v = _dot(h, wi_ref[:, dff + j * tf:dff + (j + 1) * tf])
        part = _dot((_silu(u) * v).astype(BF16), wo_ref[j * tf:(j + 1) * tf, :])
        y = part if y is None else y + part
    o_ref[...] = _layer_norm(alpha * x + gt2_ref[...] * y, g2_ref[...], b2_ref[...])


def _wo_ffn_call(merged, x, mod, w_o, w_ff_in, w_ff_out, ln1_g, ln1_b, ln2_g, ln2_b, layer, tm,
                 tiles_per_group, alpha, nf):
    n, d = x.shape
    dff = w_ff_out.shape[1]
    vec = pl.BlockSpec((None, 1, d), lambda i: (layer, 0, 0))
    rows = pl.BlockSpec((tm, d), lambda i: (i, 0))
    return pl.pallas_call(
        functools.partial(_wo_ffn_kernel, alpha=alpha, nf=nf),
        out_shape=jax.ShapeDtypeStruct((n, d), F32),
        grid=(n // tm,),
        in_specs=[rows, rows,
                  _mod_spec(mod, d, tiles_per_group, 2),
                  _mod_spec(mod, d, tiles_per_group, 4),
                  _mod_spec(mod, d, tiles_per_group, 3),
                  _mod_spec(mod, d, tiles_per_group, 5),
                  _resident((None, d, d), lambda i: (layer, 0, 0)),
                  _resident((None, d, 2 * dff), lambda i: (layer, 0, 0)),
                  _resident((None, dff, d), lambda i: (layer, 0, 0)),
                  vec, vec, vec, vec],
        out_specs=rows,
        compiler_params=_cparams(("parallel",)),
        name="wo_ffn",
    )(merged, x, mod, mod, mod, mod, w_o, w_ff_in, w_ff_out, ln1_g, ln1_b, ln2_g, ln2_b)


def _pick_tile(n, pref):
    t = min(n, pref)
    while n % t:
        t //= 2
    return t


def kernel(x_prompt, x_sample, cache_k, cache_v, state_conv, state_delta, c_prompt, c_sample,
           ln_in_g, ln_in_b, rel_bias, w_ada, b_ada, w_in, conv_w, a_log, dt_bias, norm_a,
           lam, subln_g, w_o, ln1_g, ln1_b, w_ff_in, w_ff_out, ln2_g, ln2_b):
    bp, tp, d = x_prompt.shape
    bs, ts, _ = x_sample.shape
    depth = w_in.shape[0]
    h = d // HEAD
    past = cache_k.shape[2]
    dff = w_ff_out.shape[1]
    n_p = bp * tp
    n_s = bs * ts
    alpha = (2 * depth) ** 0.25
    hb = h
    ng = h // hb

    o1 = 4 * d + 2 * h
    qkvz = w_in[:, :, 0:4 * d]
    gates = w_in[:, :, o1 + 3 * d:o1 + 5 * d]
    qb = w_in[:, :, o1:o1 + d] * ((HEAD // 2) ** -0.5 * LOG2E)
    kvb = w_in[:, :, o1 + d:o1 + 3 * d]
    w_all = jnp.concatenate([qkvz, gates, qb, kvb], axis=-1).astype(BF16)
    wb = w_in[:, :, 4 * d:4 * d + h].reshape(depth, d, ng, hb)
    wa = w_in[:, :, 4 * d + h:4 * d + 2 * h].reshape(depth, d, ng, hb)
    w_ba = jnp.concatenate([wb, wa, jnp.zeros((depth, d, ng, LANES - 2 * hb), F32)], axis=-1)
    w_ba = w_ba.reshape(depth, d, ng * LANES).astype(BF16)
    w_ada_b = w_ada.astype(BF16)
    w_o_b = w_o.astype(BF16)
    w_ff_in_b = w_ff_in.astype(BF16)
    w_ff_out_b = w_ff_out.astype(BF16)

    def lane_rows(v):
        v = v.reshape(depth, ng, hb)
        z = jnp.zeros((depth, ng, hb), F32)
        zz = jnp.zeros((depth, ng, LANES - 2 * hb), F32)
        return jnp.concatenate([z, v, zz], axis=-1).reshape(depth, ng * LANES)

    par = jnp.zeros((depth, 8, ng * LANES), F32)
    par = par.at[:, 0].set(lane_rows(dt_bias)).at[:, 1].set(lane_rows(a_log))
    par = par.at[:, 2].set(jnp.tile(norm_a, (1, ng)))
    subln = subln_g.reshape(depth, 1, HEAD)
    ln1g, ln1b = ln1_g.reshape(depth, 1, d), ln1_b.reshape(depth, 1, d)
    ln2g, ln2b = ln2_g.reshape(depth, 1, d), ln2_b.reshape(depth, 1, d)

    mod = _ada_call(jnp.concatenate([c_prompt, c_sample], axis=0), w_ada_b, b_ada)
    mod_p = mod[:, :bp].reshape(depth, bp, 1, 6 * d)
    mod_s = jnp.repeat(mod[:, bp:], ts, axis=1).reshape(depth, 1, n_s, 6 * d)

    bias_p, bias_s = _bias_call(rel_bias, TQ, past, ts)

    tm_p = _pick_tile(tp, TM_PROJ)
    tpg_p = tp // tm_p
    xp = _ln_call(x_prompt.reshape(n_p, d), ln_in_g, ln_in_b, tm_p)
    xs = _ln_call(x_sample.reshape(n_s, d), ln_in_g, ln_in_b, n_s)

    conv0 = jnp.zeros((bp, CONV_W - 1, 3 * d), F32)
    s0 = jnp.zeros((bp, h, HEAD, HEAD), F32)
    nf = 1

    kp, vp = (jnp.zeros((depth, n_p, h, HEAD), F32) for _ in range(2))
    ks, vs = (jnp.zeros((depth, n_s, h, HEAD), F32) for _ in range(2))
    conv_p, conv_s, st_p, st_s = [], [], [], []
    for l in range(depth):
        lam_init = 0.8 - 0.6 * math.exp(-0.3 * l)
        main, qb_p, kb_l, vb_l, kp, vp, ba, tail = _inproj_call(
            xp, mod_p[l], w_all, w_ba, conv_w, conv0, kp, vp, l, tm_p, tpg_p, BF16, True)
        ma, sn = _delta_call(main, ba, conv_w, par, conv0, s0, l, bp, tp, CHUNK, DELTA_CPS, hb, True)
        merged = _attn_call(qb_p, kb_l, vb_l, bias_p, lam, subln, ma, main, l, bp, tp, TQ, lam_init)
        xp = _wo_ffn_call(merged, xp, mod_p[l], w_o_b, w_ff_in_b, w_ff_out_b, ln1g, ln1b, ln2g, ln2b,
                          l, tm_p, tpg_p, alpha, nf)
        conv_p.append(tail.reshape(bp, tpg_p, CONV_PAD, 3 * d)[:, tpg_p - 1, CONV_PAD - (CONV_W - 1):, :])
        st_p.append(sn)
        main, qb_s, kb_l, vb_l, ks, vs, ba, _ = _inproj_call(
            xs, mod_s[l], w_all, w_ba, conv_w, state_conv[l], ks, vs, l, n_s, 1, F32, False)
        ma, sn = _delta_call(main, ba, conv_w, par, state_conv[l], state_delta[l], l, bs, ts, ts, 1, hb, False)
        merged = _attn_step_call(qb_s, cache_k, cache_v, kb_l, vb_l, bias_s, lam, subln, ma, main,
                                 l, ts, lam_init)
        xs = _wo_ffn_call(merged, xs, mod_s[l], w_o_b, w_ff_in_b, w_ff_out_b, ln1g, ln1b, ln2g, ln2b,
                          l, n_s, 1, alpha, nf)
        conv_s.append(main.reshape(bs, ts, -1)[:, ts - (CONV_W - 1):, 0:3 * d])
        st_s.append(sn)

    def heads_out(buf, nb, t):
        return buf.reshape(depth, nb, t, h, HEAD)

    return (xp.reshape(bp, tp, d), xs.reshape(bs, ts, d),
            heads_out(kp, bp, tp), heads_out(vp, bp, tp), jnp.stack(conv_p), jnp.stack(st_p),
            heads_out(ks, bs, ts), heads_out(vs, bs, ts), jnp.stack(conv_s), jnp.stack(st_s))
```
